```python
import math
import jax
import jax.numpy as jnp
from jax import lax
import numpy as np

D_MODEL = 4096
BATCH = 2
SEQ = 8192
DEPTH = 2

HEAD_DIM = 128
ATTN_WIDTH = D_MODEL // 2
N_Q_HEADS = ATTN_WIDTH // HEAD_DIM
N_KV_HEADS = N_Q_HEADS // 4
KV_WIDTH = N_KV_HEADS * HEAD_DIM
CONV_WIDTH = D_MODEL // 2
CONV_K = 3
GRID_W = 64
ROPE_AXIS_DIM = HEAD_DIM // 2
ROPE_THETA = 10000.0
Q_BLOCK = 128
N_EXPERTS = 16
N_GROUPS = 4
EXPERTS_PER_GROUP = N_EXPERTS // N_GROUPS
TOP_K = 2
D_FF_EXPERT = 3 * D_MODEL // 8
EXPERT_BLOCK = 512
ALPHA = (2 * DEPTH) ** 0.25
BETA = (8 * DEPTH) ** -0.25
RMS_EPS = 1e-6
LN_EPS = 1e-5
Q_END = ATTN_WIDTH
K_END = Q_END + KV_WIDTH
V_END = K_END + KV_WIDTH
CB_END = V_END + CONV_WIDTH
CC_END = CB_END + CONV_WIDTH
CH_END = CC_END + CONV_WIDTH
GA_END = CH_END + D_MODEL
W_IN_COLS = GA_END + D_MODEL
IN_SPLITS = [Q_END, K_END, V_END, CB_END, CC_END, CH_END, GA_END]

kernel_name = "hybrid_gqa_shortconv_grouped_moe_deepnorm"


def layer_norm(x, g, b):
    xf = x.astype(jnp.float32)
    mu = jnp.mean(xf, axis=-1, keepdims=True)
    var = jnp.mean(jnp.square(xf - mu), axis=-1, keepdims=True)
    y = (xf - mu) * lax.rsqrt(var + LN_EPS) * g.astype(jnp.float32) + b.astype(jnp.float32)
    return y.astype(x.dtype)


def rms_norm(x, g):
    xf = x.astype(jnp.float32)
    y = xf * lax.rsqrt(jnp.mean(jnp.square(xf), axis=-1, keepdims=True) + RMS_EPS)
    return (y * g.astype(jnp.float32)).astype(x.dtype)


def axial_rope_tables(seq_len, dtype):
    rows = seq_len // GRID_W
    n_freq = ROPE_AXIS_DIM // 2
    inv = ROPE_THETA ** (-jnp.arange(0, ROPE_AXIS_DIM, 2, dtype=jnp.float32) / ROPE_AXIS_DIM)
    ang_r = jnp.arange(rows, dtype=jnp.float32)[:, None] * inv[None, :]
    ang_c = jnp.arange(GRID_W, dtype=jnp.float32)[:, None] * inv[None, :]
    ang = jnp.stack([
        jnp.broadcast_to(ang_r[:, None, :], (rows, GRID_W, n_freq)),
        jnp.broadcast_to(ang_c[None, :, :], (rows, GRID_W, n_freq)),
    ], axis=2).reshape(seq_len, 2, n_freq)
    return jnp.cos(ang).astype(dtype), jnp.sin(ang).astype(dtype)


def apply_axial_rope(x, cos, sin):
    b, s, h, d = x.shape
    xr = x.reshape(b, s, h, 2, 2, ROPE_AXIS_DIM // 2)
    x1 = xr[..., 0, :]
    x2 = xr[..., 1, :]
    c = cos[None, :, None]
    sn = sin[None, :, None]
    out = jnp.stack([x1 * c - x2 * sn, x2 * c + x1 * sn], axis=-2)
    return out.reshape(b, s, h, d)


def blocked_gqa_attention(q, k, v):
    b, s, hq, dh = q.shape
    hkv = k.shape[2]
    grp = hq // hkv
    nb = s // Q_BLOCK
    scale = dh ** -0.5
    qb = q.reshape(b, nb, Q_BLOCK, hkv, grp, dh).transpose(1, 0, 2, 3, 4, 5)

    def one_block(q_blk):
        sc = jnp.einsum('bqhgd,bkhd->bhgqk', q_blk, k).astype(jnp.float32) * scale
        p = jax.nn.softmax(sc, axis=-1).astype(v.dtype)
        return jnp.einsum('bhgqk,bkhd->bqhgd', p, v)

    o = lax.map(one_block, qb)
    return o.transpose(1, 0, 2, 3, 4, 5).reshape(b, s, hq * dh)


def short_conv_mixer(cb, cc, ch, conv_w):
    bx = cb * ch
    y = lax.conv_general_dilated(
        bx, conv_w[:, None, :].astype(bx.dtype), window_strides=(1,),
        padding=[(CONV_K // 2, CONV_K // 2)],
        dimension_numbers=('NWC', 'WIO', 'NWC'),
        feature_group_count=bx.shape[-1])
    return cc * y


def grouped_moe(x, w_router, router_bias, w_gate, w_up, w_down):
    b, s, d = x.shape
    xf = x.reshape(-1, d)
    n_tok = xf.shape[0]
    n_asg = n_tok * TOP_K
    scores = jax.nn.softmax((xf @ w_router).astype(jnp.float32), axis=-1)
    sel = scores + router_bias.astype(jnp.float32)
    group_score = lax.top_k(sel.reshape(n_tok, N_GROUPS, EXPERTS_PER_GROUP), TOP_K)[0].sum(-1)
    top_group = jnp.argmax(group_score, axis=-1)
    in_group = (jnp.arange(N_EXPERTS) // EXPERTS_PER_GROUP)[None, :] == top_group[:, None]
    _, idx = lax.top_k(jnp.where(in_group, sel, -jnp.inf), TOP_K)
    gate = jnp.take_along_axis(scores, idx, axis=-1)
    gate = gate / jnp.sum(gate, axis=-1, keepdims=True)
    e_flat = idx.reshape(-1)
    order = jnp.argsort(e_flat, stable=True)
    e_sorted = e_flat[order]
    counts = jnp.bincount(e_flat, length=N_EXPERTS)
    starts = jnp.cumsum(counts) - counts
    pcounts = (counts + EXPERT_BLOCK - 1) // EXPERT_BLOCK * EXPERT_BLOCK
    pends = jnp.cumsum(pcounts)
    pstarts = pends - pcounts
    dest = (pstarts[e_sorted] + jnp.arange(n_asg) - starts[e_sorted]).astype(jnp.int32)
    n_blocks = -(-n_asg // EXPERT_BLOCK) + N_EXPERTS
    n_rows = n_blocks * EXPERT_BLOCK
    row_token = jnp.zeros((n_rows,), jnp.int32).at[dest].set((order // TOP_K).astype(jnp.int32))
    block_start = jnp.arange(n_blocks) * EXPERT_BLOCK
    block_e = jnp.minimum(jnp.sum(pends[None, :] <= block_start[:, None], axis=-1), N_EXPERTS - 1)
    x_blocks = xf[row_token].reshape(n_blocks, EXPERT_BLOCK, d)

    def expert_block(args):
        xb, e = args
        hid = jax.nn.silu(xb @ w_gate[e]) * (xb @ w_up[e])
        return hid @ w_down[e]

    y_rows = lax.map(expert_block, (x_blocks, block_e)).reshape(n_rows, d)
    slot = jnp.zeros((n_asg,), jnp.int32).at[order].set(dest)
    y = y_rows[slot].reshape(n_tok, TOP_K, d)
    out = jnp.einsum('tk,tkd->td', gate.astype(x.dtype), y)
    return out.reshape(b, s, d)


def setup_inputs(seed: int = 0) -> dict:
    key = jax.random.key(seed)
    ks = jax.random.split(key, 18)

    def nrm(k, shape, scale):
        return jax.random.normal(k, shape, jnp.float32) * scale

    return {
        "x": nrm(ks[0], (BATCH, SEQ, D_MODEL), 1.0),
        "w_in": nrm(ks[1], (DEPTH, D_MODEL, W_IN_COLS), D_MODEL ** -0.5),
        "q_norm_g": 1.0 + nrm(ks[2], (DEPTH, HEAD_DIM), 0.1),
        "k_norm_g": 1.0 + nrm(ks[3], (DEPTH, HEAD_DIM), 0.1),
        "conv_w": nrm(ks[4], (DEPTH, CONV_K, CONV_WIDTH), CONV_K ** -0.5),
        "w_attn_proj": nrm(ks[5], (DEPTH, ATTN_WIDTH, D_MODEL), ATTN_WIDTH ** -0.5),
        "w_conv_proj": nrm(ks[6], (DEPTH, CONV_WIDTH, D_MODEL), CONV_WIDTH ** -0.5),
        "w_out": nrm(ks[7], (DEPTH, D_MODEL, D_MODEL), BETA * D_MODEL ** -0.5),
        "ln1_g": 1.0 + nrm(ks[8], (DEPTH, D_MODEL), 0.1),
        "ln1_b": nrm(ks[9], (DEPTH, D_MODEL), 0.02),
        "w_router": nrm(ks[10], (D_MODEL, N_EXPERTS), D_MODEL ** -0.5),
        "router_bias": nrm(ks[11], (N_EXPERTS,), 0.01),
        "w_gate": nrm(ks[12], (DEPTH, N_EXPERTS, D_MODEL, D_FF_EXPERT), D_MODEL ** -0.5),
        "w_up": nrm(ks[13], (DEPTH, N_EXPERTS, D_MODEL, D_FF_EXPERT), D_MODEL ** -0.5),
        "w_down": nrm(ks[14], (DEPTH, N_EXPERTS, D_FF_EXPERT, D_MODEL), BETA * D_FF_EXPERT ** -0.5),
        "ln2_g": 1.0 + nrm(ks[15], (DEPTH, D_MODEL), 0.1),
        "ln2_b": nrm(ks[16], (DEPTH, D_MODEL), 0.02),
    }


def reference(x, w_in, q_norm_g, k_norm_g, conv_w, w_attn_proj, w_conv_proj, w_out,
              ln1_g, ln1_b, w_router, router_bias, w_gate, w_up, w_down, ln2_g, ln2_b):
    b, s, d = x.shape
    cos, sin = axial_rope_tables(s, x.dtype)
    for l in range(DEPTH):
        p = x @ w_in[l]
        q, k, v, cb, cc, ch, ga, gc = jnp.split(p, IN_SPLITS, axis=-1)
        q = apply_axial_rope(rms_norm(q.reshape(b, s, N_Q_HEADS, HEAD_DIM), q_norm_g[l]), cos, sin)
        k = apply_axial_rope(rms_norm(k.reshape(b, s, N_KV_HEADS, HEAD_DIM), k_norm_g[l]), cos, sin)
        v = v.reshape(b, s, N_KV_HEADS, HEAD_DIM)
        attn = blocked_gqa_attention(q, k, v)
        conv = short_conv_mixer(cb, cc, ch, conv_w[l])
        merged = jax.nn.sigmoid(ga) * (attn @ w_attn_proj[l]) + jax.nn.sigmoid(gc) * (conv @ w_conv_proj[l])
        x = layer_norm(ALPHA * x + merged @ w_out[l], ln1_g[l], ln1_b[l])
        ffn = grouped_moe(x, w_router, router_bias, w_gate[l], w_up[l], w_down[l])
        x = layer_norm(ALPHA * x + ffn, ln2_g[l], ln2_b[l])
    return x
```

```python
import functools
import math

import jax
import jax.numpy as jnp
from jax import lax
from jax.experimental import pallas as pl
from jax.experimental.pallas import tpu as pltpu

HEAD_DIM = 128
GQA_RATIO = 4
CONV_K = 3
GRID_W = 64
ROPE_THETA = 10000.0
N_GROUPS = 4
TOP_K = 2
RMS_EPS = 1e-6
LN_EPS = 1e-5
LANES = 128
SUBLANES = 8
VMEM_LIMIT = 52 * 1024 * 1024

F32 = jnp.float32
BF16 = jnp.bfloat16


def _params(semantics):
    return pltpu.CompilerParams(dimension_semantics=semantics, vmem_limit_bytes=VMEM_LIMIT)


def _tile(dim, pref):
    t = min(dim, pref)
    while dim % t:
        t //= 2
    return t


def _mm_kernel(x_ref, w_ref, o_ref):
    o_ref[...] = jnp.dot(x_ref[...], w_ref[...], preferred_element_type=F32).astype(o_ref.dtype)


def _mm_res_kernel(x_ref, w_ref, r_ref, o_ref, *, alpha):
    acc = jnp.dot(x_ref[...], w_ref[...], preferred_element_type=F32)
    o_ref[...] = (alpha * r_ref[...] + acc).astype(o_ref.dtype)


def _matmul(x, w, layer, out_dtype, *, res=None, alpha=1.0, tm=1024, tn=1024, name="matmul"):
    m, k = x.shape
    n = w.shape[2]
    tm, tn = _tile(m, tm), _tile(n, tn)
    in_specs = [
        pl.BlockSpec((tm, k), lambda i, j: (i, 0)),
        pl.BlockSpec((None, k, tn), lambda i, j: (layer, 0, j)),
    ]
    args = [x, w]
    body = _mm_kernel
    if res is not None:
        in_specs.append(pl.BlockSpec((tm, tn), lambda i, j: (i, j)))
        args.append(res)
        body = functools.partial(_mm_res_kernel, alpha=alpha)
    return pl.pallas_call(
        body,
        grid=(m // tm, n // tn),
        in_specs=in_specs,
        out_specs=pl.BlockSpec((tm, tn), lambda i, j: (i, j)),
        out_shape=jax.ShapeDtypeStruct((m, n), out_dtype),
        compiler_params=_params(("parallel", "parallel")),
        name=name,
    )(*args)


def _swap_halves(y):
    lane = lax.broadcasted_iota(jnp.int32, y.shape, 1)
    fwd = pltpu.roll(y, LANES - 32, axis=1)
    bwd = pltpu.roll(y, 32, axis=1)
    return jnp.where((lane % 64) < 32, fwd, bwd)


def _qk_prep_kernel(p_ref, g_ref, c_ref, s_ref, o_ref, *, heads):
    cos = c_ref[...]
    sin = s_ref[...]
    for h in range(heads):
        sl = slice(h * HEAD_DIM, (h + 1) * HEAD_DIM)
        xh = p_ref[:, sl].astype(F32)
        ms = jnp.mean(xh * xh, axis=-1, keepdims=True)
        y = xh * lax.rsqrt(ms + RMS_EPS) * g_ref[:, sl]
        o_ref[:, sl] = (y * cos + _swap_halves(y) * sin).astype(o_ref.dtype)


def _qk_prep(p, gains, cos_t, sin_t, seq, qk_width, *, tm=512, heads=4):
    t = p.shape[0]
    tm = _tile(seq, tm)
    n_heads = qk_width // HEAD_DIM
    heads = _tile(n_heads, heads)
    wblk = heads * HEAD_DIM
    sblocks = seq // tm
    return pl.pallas_call(
        functools.partial(_qk_prep_kernel, heads=heads),
        grid=(t // tm, n_heads // heads),
        in_specs=[
            pl.BlockSpec((tm, wblk), lambda i, j: (i, j)),
            pl.BlockSpec((1, wblk), lambda i, j: (0, j)),
            pl.BlockSpec((tm, HEAD_DIM), lambda i, j: (i % sblocks, 0)),
            pl.BlockSpec((tm, HEAD_DIM), lambda i, j: (i % sblocks, 0)),
        ],
        out_specs=pl.BlockSpec((tm, wblk), lambda i, j: (i, j)),
        out_shape=jax.ShapeDtypeStruct((t, qk_width), BF16),
        compiler_params=_params(("parallel", "parallel")),
        name="qk_prep",
    )(p, gains, cos_t, sin_t)


def _flash_kernel(q_ref, k_ref, v_ref, o_ref, qs_ref, *, tq, tk, seq):
    for g in range(GQA_RATIO):
        qs_ref[g * tq:(g + 1) * tq, :] = q_ref[:, g * HEAD_DIM:(g + 1) * HEAD_DIM]
    q = qs_ref[...]
    rows = GQA_RATIO * tq

    def body(j, carry):
        m, l, acc = carry
        off = pl.multiple_of(j * tk, tk)
        k = k_ref[pl.ds(off, tk), :]
        v = v_ref[pl.ds(off, tk), :]
        s = lax.dot_general(q, k, (((1,), (1,)), ((), ())), preferred_element_type=F32)
        m_new = jnp.maximum(m, jnp.max(s, axis=-1, keepdims=True))
        alpha = jnp.exp(m - m_new)
        p = jnp.exp(s - m_new)
        l = alpha * l + jnp.sum(p, axis=-1, keepdims=True)
        acc = alpha * acc + jnp.dot(p.astype(BF16), v, preferred_element_type=F32)
        return m_new, l, acc

    m0 = jnp.full((rows, 1), -jnp.inf, F32)
    l0 = jnp.zeros((rows, 1), F32)
    a0 = jnp.zeros((rows, HEAD_DIM), F32)
    _, l, acc = lax.fori_loop(0, seq // tk, body, (m0, l0, a0))
    out = acc / l
    for g in range(GQA_RATIO):
        o_ref[:, g * HEAD_DIM:(g + 1) * HEAD_DIM] = out[g * tq:(g + 1) * tq, :].astype(o_ref.dtype)


def _attention(qk, p, batch, seq, n_kv, k_col0, v_col0, *, tq=512, tk=512):
    tq, tk = _tile(seq, tq), _tile(seq, tk)
    qk3 = qk.reshape(batch, seq, qk.shape[1])
    p3 = p.reshape(batch, seq, p.shape[1])
    gw = GQA_RATIO * HEAD_DIM
    out = pl.pallas_call(
        functools.partial(_flash_kernel, tq=tq, tk=tk, seq=seq),
        grid=(batch, n_kv, seq // tq),
        in_specs=[
            pl.BlockSpec((None, tq, gw), lambda b, h, i: (b, i, h)),
            pl.BlockSpec((None, seq, HEAD_DIM), lambda b, h, i: (b, 0, k_col0 + h)),
            pl.BlockSpec((None, seq, HEAD_DIM), lambda b, h, i: (b, 0, v_col0 + h)),
        ],
        out_specs=pl.BlockSpec((None, tq, gw), lambda b, h, i: (b, i, h)),
        out_shape=jax.ShapeDtypeStruct((batch, seq, n_kv * gw), BF16),
        scratch_shapes=[pltpu.VMEM((GQA_RATIO * tq, HEAD_DIM), BF16)],
        compiler_params=_params(("parallel", "parallel", "arbitrary")),
        name="flash_gqa",
    )(qk3, qk3, p3)
    return out.reshape(batch * seq, n_kv * gw)


def _conv_kernel(cb_ref, cc_ref, ch_ref, cbp_ref, chp_ref, cbn_ref, chn_ref, w_ref, o_ref, *, tm, sblocks):
    i = pl.program_id(0)
    first = (i % sblocks) == 0
    last = (i % sblocks) == sblocks - 1
    bx = cb_ref[...].astype(F32) * ch_ref[...].astype(F32)
    prev_row = cbp_ref[SUBLANES - 1:SUBLANES, :].astype(F32) * chp_ref[SUBLANES - 1:SUBLANES, :].astype(F32)
    next_row = cbn_ref[0:1, :].astype(F32) * chn_ref[0:1, :].astype(F32)
    prev_row = jnp.where(first, 0.0, prev_row)
    next_row = jnp.where(last, 0.0, next_row)
    row = lax.broadcasted_iota(jnp.int32, bx.shape, 0)
    prev = jnp.where(row == 0, prev_row, pltpu.roll(bx, 1, axis=0))
    nxt = jnp.where(row == tm - 1, next_row, pltpu.roll(bx, tm - 1, axis=0))
    w = w_ref[...]
    y = w[0:1, :] * prev + w[1:2, :] * bx + w[2:3, :] * nxt
    o_ref[...] = (cc_ref[...].astype(F32) * y).astype(o_ref.dtype)


def _short_conv(p, conv_w, layer, seq, cb0, cc0, ch0, width, *, tm=512, tc=1024):
    t = p.shape[0]
    tm, tc = _tile(seq, tm), math.gcd(cb0, cc0, ch0, width, tc)
    sblocks = seq // tm
    rpb = tm // SUBLANES
    nrb = t // SUBLANES

    def main(c0):
        return pl.BlockSpec((tm, tc), lambda i, j: (i, c0 // tc + j))

    def halo_prev(c0):
        return pl.BlockSpec((SUBLANES, tc), lambda i, j: (jnp.maximum(i * rpb - 1, 0), c0 // tc + j))

    def halo_next(c0):
        return pl.BlockSpec((SUBLANES, tc), lambda i, j: (jnp.minimum((i + 1) * rpb, nrb - 1), c0 // tc + j))

    return pl.pallas_call(
        functools.partial(_conv_kernel, tm=tm, sblocks=sblocks),
        grid=(t // tm, width // tc),
        in_specs=[main(cb0), main(cc0), main(ch0), halo_prev(cb0), halo_prev(ch0),
                  halo_next(cb0), halo_next(ch0),
                  pl.BlockSpec((None, CONV_K, tc), lambda i, j: (layer, 0, j))],
        out_specs=pl.BlockSpec((tm, tc), lambda i, j: (i, j)),
        out_shape=jax.ShapeDtypeStruct((t, width), BF16),
        compiler_params=_params(("parallel", "parallel")),
        name="short_conv",
    )(p, p, p, p, p, p, p, conv_w)


def _merge_kernel(a_ref, c_ref, wa_ref, wc_ref, ga_ref, gc_ref, o_ref):
    ya = jnp.dot(a_ref[...], wa_ref[...], preferred_element_type=F32)
    yc = jnp.dot(c_ref[...], wc_ref[...], preferred_element_type=F32)
    o_ref[...] = (jax.nn.sigmoid(ga_ref[...].astype(F32)) * ya
                  + jax.nn.sigmoid(gc_ref[...].astype(F32)) * yc).astype(o_ref.dtype)


def _merge(attn, conv, wa, wc, p, layer, ga0, gc0, *, tm=512, tn=1024):
    t, ka = attn.shape
    kc = conv.shape[1]
    d = wa.shape[2]
    tm, tn = _tile(t, tm), math.gcd(ga0, gc0, d, tn)
    return pl.pallas_call(
        _merge_kernel,
        grid=(t // tm, d // tn),
        in_specs=[
            pl.BlockSpec((tm, ka), lambda i, j: (i, 0)),
            pl.BlockSpec((tm, kc), lambda i, j: (i, 0)),
            pl.BlockSpec((None, ka, tn), lambda i, j: (layer, 0, j)),
            pl.BlockSpec((None, kc, tn), lambda i, j: (layer, 0, j)),
            pl.BlockSpec((tm, tn), lambda i, j: (i, ga0 // tn + j)),
            pl.BlockSpec((tm, tn), lambda i, j: (i, gc0 // tn + j)),
        ],
        out_specs=pl.BlockSpec((tm, tn), lambda i, j: (i, j)),
        out_shape=jax.ShapeDtypeStruct((t, d), BF16),
        compiler_params=_params(("parallel", "parallel")),
        name="gated_merge",
    )(attn, conv, wa, wc, p, p)


def _ln_rows(h, g, b):
    mu = jnp.mean(h, axis=-1, keepdims=True)
    hc = h - mu
    var = jnp.mean(hc * hc, axis=-1, keepdims=True)
    return hc * lax.rsqrt(var + LN_EPS) * g + b


def _ln_kernel(h_ref, g_ref, b_ref, o_ref, ob_ref):
    y = _ln_rows(h_ref[...], g_ref[...], b_ref[...])
    o_ref[...] = y
    ob_ref[...] = y.astype(BF16)


def _layer_norm(h, g, b, layer, *, tm=256):
    t, d = h.shape
    tm = _tile(t, tm)
    row = pl.BlockSpec((tm, d), lambda i: (i, 0))
    vec = pl.BlockSpec((None, 1, d), lambda i: (layer, 0, 0))
    return pl.pallas_call(
        _ln_kernel,
        grid=(t // tm,),
        in_specs=[row, vec, vec],
        out_specs=[row, row],
        out_shape=[jax.ShapeDtypeStruct((t, d), F32), jax.ShapeDtypeStruct((t, d), BF16)],
        compiler_params=_params(("parallel",)),
        name="layer_norm",
    )(h, g, b)


def _router_kernel(x_ref, w_ref, b_ref, idx_ref, gate_ref, *, n_exp):
    epg = n_exp // N_GROUPS
    logits = lax.dot_general(w_ref[...], x_ref[...], (((1,), (1,)), ((), ())),
                             precision=lax.Precision.HIGHEST, preferred_element_type=F32)
    mx = jnp.max(logits, axis=0, keepdims=True)
    ex = jnp.exp(logits - mx)
    scores = ex / jnp.sum(ex, axis=0, keepdims=True)
    sel = scores + b_ref[...]
    rows_sel = [sel[e:e + 1, :] for e in range(n_exp)]
    rows_sc = [scores[e:e + 1, :] for e in range(n_exp)]
    best = None
    for g in range(N_GROUPS):
        mem = rows_sel[g * epg:(g + 1) * epg]
        gs = None
        for a in range(epg):
            for c in range(a + 1, epg):
                pair = mem[a] + mem[c]
                gs = pair if gs is None else jnp.maximum(gs, pair)
        if best is None:
            best, grp = gs, jnp.zeros(gs.shape, jnp.int32)
        else:
            upd = gs > best
            best = jnp.where(upd, gs, best)
            grp = jnp.where(upd, g, grp)
    cand_sel, cand_sc = [], []
    for j in range(epg):
        cs, cc = rows_sel[j], rows_sc[j]
        for g in range(1, N_GROUPS):
            cs = jnp.where(grp == g, rows_sel[g * epg + j], cs)
            cc = jnp.where(grp == g, rows_sc[g * epg + j], cc)
        cand_sel.append(cs)
        cand_sc.append(cc)

    def first_argmax(vals, skip):
        bv = bi = bs = None
        for j in range(epg):
            v = vals[j] if skip is None else jnp.where(skip == j, -jnp.inf, vals[j])
            if bv is None:
                bv, bi, bs = v, jnp.zeros(v.shape, jnp.int32), cand_sc[0]
            else:
                upd = v > bv
                bv = jnp.where(upd, v, bv)
                bi = jnp.where(upd, j, bi)
                bs = jnp.where(upd, cand_sc[j], bs)
        return bi, bs

    i1, s1 = first_argmax(cand_sel, None)
    i2, s2 = first_argmax(cand_sel, i1)
    tot = s1 + s2
    zi = jnp.zeros((SUBLANES - TOP_K,) + i1.shape[1:], jnp.int32)
    zf = jnp.zeros((SUBLANES - TOP_K,) + i1.shape[1:], F32)
    idx_ref[...] = jnp.concatenate([grp * epg + i1, grp * epg + i2, zi], axis=0)
    gate_ref[...] = jnp.concatenate([s1 / tot, s2 / tot, zf], axis=0)


def _router(x, w_router_t, bias_col, *, tm=512):
    t, d = x.shape
    n_exp = w_router_t.shape[0]
    tm = _tile(t, tm)
    out = pl.BlockSpec((SUBLANES, tm), lambda i: (0, i))
    return pl.pallas_call(
        functools.partial(_router_kernel, n_exp=n_exp),
        grid=(t // tm,),
        in_specs=[
            pl.BlockSpec((tm, d), lambda i: (i, 0)),
            pl.BlockSpec((n_exp, d), lambda i: (0, 0)),
            pl.BlockSpec((n_exp, 1), lambda i: (0, 0)),
        ],
        out_specs=[out, out],
        out_shape=[jax.ShapeDtypeStruct((SUBLANES, t), jnp.int32),
                   jax.ShapeDtypeStruct((SUBLANES, t), F32)],
        compiler_params=_params(("parallel",)),
        name="router",
    )(x, w_router_t, bias_col)


def _expert_kernel(be_ref, nu_ref, x_ref, wg_ref, wu_ref, wd_ref, o_ref):
    i = pl.program_id(0)
    f = pl.program_id(1)

    @pl.when(i < nu_ref[0])
    def _():
        x = x_ref[...]
        hg = jnp.dot(x, wg_ref[...], preferred_element_type=F32)
        hu = jnp.dot(x, wu_ref[...], preferred_element_type=F32)
        hid = (hg * jax.nn.sigmoid(hg) * hu).astype(BF16)
        y = jnp.dot(hid, wd_ref[...], preferred_element_type=F32)

        @pl.when(f == 0)
        def _():
            o_ref[...] = y

        @pl.when(f != 0)
        def _():
            o_ref[...] += y

    @pl.when(jnp.logical_and(i >= nu_ref[0], f == 0))
    def _():
        o_ref[...] = jnp.zeros_like(o_ref)


def _experts(block_e, n_used, xs, wg, wu, wd, layer, tm, *, tf=256):
    r, d = xs.shape
    ff = wg.shape[3]
    tf = _tile(ff, tf)
    nblk = r // tm
    grid_spec = pltpu.PrefetchScalarGridSpec(
        num_scalar_prefetch=2,
        grid=(nblk, ff // tf),
        in_specs=[
            pl.BlockSpec((tm, d), lambda i, f, be, nu: (i, 0)),
            pl.BlockSpec((None, None, d, tf), lambda i, f, be, nu: (layer, be[i], 0, f)),
            pl.BlockSpec((None, None, d, tf), lambda i, f, be, nu: (layer, be[i], 0, f)),
            pl.BlockSpec((None, None, tf, d), lambda i, f, be, nu: (layer, be[i], f, 0)),
        ],
        out_specs=pl.BlockSpec((tm, d), lambda i, f, be, nu: (i, 0)),
    )
    return pl.pallas_call(
        _expert_kernel,
        grid_spec=grid_spec,
        out_shape=jax.ShapeDtypeStruct((r, d), F32),
        compiler_params=_params(("parallel", "arbitrary")),
        name="expert_ffn",
    )(block_e, n_used, xs, wg, wu, wd)


def _combine_kernel(x_ref, ya_ref, yb_ref, gt_ref, g_ref, b_ref, o_ref, ob_ref, *, alpha):
    gt = gt_ref[...]
    h = alpha * x_ref[...] + gt[:, 0:1] * ya_ref[...] + gt[:, 1:2] * yb_ref[...]
    y = _ln_rows(h, g_ref[...], b_ref[...])
    o_ref[...] = y
    ob_ref[...] = y.astype(BF16)


def _combine_ln(x, ya, yb, gates_t, g, b, layer, alpha, *, tm=256):
    t, d = x.shape
    tm = _tile(t, tm)
    row = pl.BlockSpec((tm, d), lambda i: (i, 0))
    vec = pl.BlockSpec((None, 1, d), lambda i: (layer, 0, 0))
    return pl.pallas_call(
        functools.partial(_combine_kernel, alpha=alpha),
        grid=(t // tm,),
        in_specs=[row, row, row, pl.BlockSpec((tm, SUBLANES), lambda i: (i, 0)), vec, vec],
        out_specs=[row, row],
        out_shape=[jax.ShapeDtypeStruct((t, d), F32), jax.ShapeDtypeStruct((t, d), BF16)],
        compiler_params=_params(("parallel",)),
        name="combine_ln",
    )(x, ya, yb, gates_t, g, b)


def _rope_tables(seq):
    axis_dim = HEAD_DIM // 2
    rows = seq // GRID_W
    inv = ROPE_THETA ** (-jnp.arange(0, axis_dim, 2, dtype=F32) / axis_dim)
    ang_r = jnp.repeat(jnp.arange(rows, dtype=F32), GRID_W)[:, None] * inv[None, :]
    ang_c = jnp.tile(jnp.arange(GRID_W, dtype=F32), rows)[:, None] * inv[None, :]
    cr, sr, cc, sc = jnp.cos(ang_r), jnp.sin(ang_r), jnp.cos(ang_c), jnp.sin(ang_c)
    cos_t = jnp.concatenate([cr, cr, cc, cc], axis=1)
    sin_t = jnp.concatenate([-sr, sr, -sc, sc], axis=1)
    return cos_t, sin_t


def _dispatch(idx, n_exp, tm):
    t = idx.shape[1]
    n_asg = t * TOP_K
    e_flat = idx.T.reshape(-1)
    onehot = (e_flat[:, None] == jnp.arange(n_exp, dtype=jnp.int32)[None, :]).astype(jnp.int32)
    csum = jnp.cumsum(onehot, axis=0)
    rank = jnp.sum(csum * onehot, axis=1) - 1
    counts = csum[-1]
    pcounts = (counts + tm - 1) // tm * tm
    pends = jnp.cumsum(pcounts)
    pstarts = pends - pcounts
    slot = (pstarts[e_flat] + rank).astype(jnp.int32)
    nblk = n_asg // tm + n_exp
    row_token = jnp.zeros((nblk * tm,), jnp.int32).at[slot].set(jnp.arange(n_asg, dtype=jnp.int32) // TOP_K)
    block_start = jnp.arange(nblk, dtype=jnp.int32) * tm
    block_e = jnp.minimum(jnp.sum(pends[None, :] <= block_start[:, None], axis=-1), n_exp - 1).astype(jnp.int32)
    n_used = (pends[-1] // tm).astype(jnp.int32).reshape(1)
    return slot.reshape(t, TOP_K), row_token, block_e, n_used


def kernel(x, w_in, q_norm_g, k_norm_g, conv_w, w_attn_proj, w_conv_proj, w_out, ln1_g, ln1_b,
           w_router, router_bias, w_gate, w_up, w_down, ln2_g, ln2_b):
    batch, seq, d = x.shape
    depth = w_in.shape[0]
    t = batch * seq
    attn_w = d // 2
    n_q = attn_w // HEAD_DIM
    n_kv = n_q // GQA_RATIO
    kv_w = n_kv * HEAD_DIM
    conv_wd = d // 2
    n_exp = w_router.shape[1]
    alpha = (2 * depth) ** 0.25
    q_end = attn_w
    k_end = q_end + kv_w
    v_end = k_end + kv_w
    cb_end = v_end + conv_wd
    cc_end = cb_end + conv_wd
    ch_end = cc_end + conv_wd
    ga_end = ch_end + d
    expert_tm = min(512, t)

    cos_t, sin_t = _rope_tables(seq)
    scale = HEAD_DIM ** -0.5
    w_in_b = w_in.astype(BF16)
    wa_b = w_attn_proj.astype(BF16)
    wc_b = w_conv_proj.astype(BF16)
    wo_b = w_out.astype(BF16)
    wg_b = w_gate.astype(BF16)
    wu_b = w_up.astype(BF16)
    wd_b = w_down.astype(BF16)
    w_router_t = w_router.T
    bias_col = router_bias.reshape(n_exp, 1).astype(F32)
    ln1_g3, ln1_b3 = ln1_g.reshape(depth, 1, d), ln1_b.reshape(depth, 1, d)
    ln2_g3, ln2_b3 = ln2_g.reshape(depth, 1, d), ln2_b.reshape(depth, 1, d)

    xf = x.reshape(t, d)
    xb = xf.astype(BF16)
    for l in range(depth):
        p = _matmul(xb, w_in_b, l, BF16, name="in_proj")
        gains = jnp.concatenate([jnp.tile(q_norm_g[l] * scale, n_q), jnp.tile(k_norm_g[l], n_kv)]).reshape(1, k_end)
        qk = _qk_prep(p, gains.astype(F32), cos_t, sin_t, seq, k_end)
        attn = _attention(qk, p, batch, seq, n_kv, q_end // HEAD_DIM, k_end // HEAD_DIM)
        conv = _short_conv(p, conv_w, l, seq, v_end, cb_end, cc_end, conv_wd)
        merged = _merge(attn, conv, wa_b, wc_b, p, l, ch_end, ga_end)
        h1 = _matmul(merged, wo_b, l, F32, res=xf, alpha=alpha, tn=512, name="out_proj")
        x1, x1b = _layer_norm(h1, ln1_g3, ln1_b3, l)
        idx, gate = _router(x1, w_router_t, bias_col)
        slot, row_token, block_e, n_used = _dispatch(idx[:TOP_K], n_exp, expert_tm)
        xs = jnp.take(x1b, row_token, axis=0)
        y = _experts(block_e, n_used, xs, wg_b, wu_b, wd_b, l, expert_tm)
        ya = jnp.take(y, slot[:, 0], axis=0)
        yb = jnp.take(y, slot[:, 1], axis=0)
        xf, xb = _combine_ln(x1, ya, yb, gate.T, ln2_g3, ln2_b3, l, alpha)
    return xf.reshape(batch, seq, d)
```

```python
import functools
import math

import jax
import jax.numpy as jnp
from jax import lax
from jax.experimental import pallas as pl
from jax.experimental.pallas import tpu as pltpu

HEAD_DIM = 128
GQA_RATIO = 4
CONV_K = 3
GRID_W = 64
ROPE_THETA = 10000.0
N_GROUPS = 4
TOP_K = 2
RMS_EPS = 1e-6
LN_EPS = 1e-5
LANES = 128
SUBLANES = 8
ONES_ROWS = 2 * SUBLANES
VMEM_LIMIT = 52 * 1024 * 1024

F32 = jnp.float32
BF16 = jnp.bfloat16


def _params(semantics):
    return pltpu.CompilerParams(dimension_semantics=semantics, vmem_limit_bytes=VMEM_LIMIT)


def _tile(dim, pref):
    t = min(dim, pref)
    while dim % t:
        t //= 2
    return t


def _mm_kernel(x_ref, w_ref, o_ref):
    o_ref[...] = jnp.dot(x_ref[...], w_ref[...], preferred_element_type=F32).astype(o_ref.dtype)


def _mm_res_kernel(x_ref, w_ref, r_ref, o_ref, *, alpha):
    acc = jnp.dot(x_ref[...], w_ref[...], preferred_element_type=F32)
    o_ref[...] = (alpha * r_ref[...] + acc).astype(o_ref.dtype)


def _matmul(x, w, layer, out_dtype, *, res=None, alpha=1.0, tm=1024, tn=1024, name="matmul"):
    m, k = x.shape
    n = w.shape[2]
    tm, tn = _tile(m, tm), _tile(n, tn)
    in_specs = [
        pl.BlockSpec((tm, k), lambda i, j: (i, 0)),
        pl.BlockSpec((None, k, tn), lambda i, j: (layer, 0, j)),
    ]
    args = [x, w]
    body = _mm_kernel
    if res is not None:
        in_specs.append(pl.BlockSpec((tm, tn), lambda i, j: (i, j)))
        args.append(res)
        body = functools.partial(_mm_res_kernel, alpha=alpha)
    return pl.pallas_call(
        body,
        grid=(m // tm, n // tn),
        in_specs=in_specs,
        out_specs=pl.BlockSpec((tm, tn), lambda i, j: (i, j)),
        out_shape=jax.ShapeDtypeStruct((m, n), out_dtype),
        compiler_params=_params(("parallel", "parallel")),
        name=name,
    )(*args)


def _swap_halves(y):
    lane = lax.broadcasted_iota(jnp.int32, y.shape, 1)
    fwd = pltpu.roll(y, LANES - 32, axis=1)
    bwd = pltpu.roll(y, 32, axis=1)
    return jnp.where((lane % 64) < 32, fwd, bwd)


def _qk_prep_kernel(p_ref, g_ref, c_ref, s_ref, o_ref, *, heads):
    cos = c_ref[...]
    sin = s_ref[...]
    for h in range(heads):
        sl = slice(h * HEAD_DIM, (h + 1) * HEAD_DIM)
        xh = p_ref[:, sl].astype(F32)
        ms = jnp.mean(xh * xh, axis=-1, keepdims=True)
        y = xh * lax.rsqrt(ms + RMS_EPS) * g_ref[:, sl]
        o_ref[:, sl] = (y * cos + _swap_halves(y) * sin).astype(o_ref.dtype)


def _qk_prep(p, gains, cos_t, sin_t, seq, qk_width, *, tm=512, heads=4):
    t = p.shape[0]
    tm = _tile(seq, tm)
    n_heads = qk_width // HEAD_DIM
    heads = _tile(n_heads, heads)
    wblk = heads * HEAD_DIM
    sblocks = seq // tm
    return pl.pallas_call(
        functools.partial(_qk_prep_kernel, heads=heads),
        grid=(t // tm, n_heads // heads),
        in_specs=[
            pl.BlockSpec((tm, wblk), lambda i, j: (i, j)),
            pl.BlockSpec((1, wblk), lambda i, j: (0, j)),
            pl.BlockSpec((tm, HEAD_DIM), lambda i, j: (i % sblocks, 0)),
            pl.BlockSpec((tm, HEAD_DIM), lambda i, j: (i % sblocks, 0)),
        ],
        out_specs=pl.BlockSpec((tm, wblk), lambda i, j: (i, j)),
        out_shape=jax.ShapeDtypeStruct((t, qk_width), BF16),
        compiler_params=_params(("parallel", "parallel")),
        name="qk_prep",
    )(p, gains, cos_t, sin_t)


def _flash_kernel(q_ref, k_ref, v_ref, o_ref, qt_ref, vt_ref, *stat_refs, tq, tk, cb, seq, unroll, ahead):
    nkv = seq // tk
    ncb = GQA_RATIO * tq // cb
    acc_refs, m_refs = stat_refs[:ncb], stat_refs[ncb:]

    @pl.when(pl.program_id(2) == 0)
    def _():
        for c in range(nkv):
            vt_ref[c, :HEAD_DIM, :] = v_ref[c * tk:(c + 1) * tk, :].T
            vt_ref[c, HEAD_DIM:, :] = jnp.ones((ONES_ROWS, tk), BF16)

    for g in range(GQA_RATIO):
        qt_ref[:, g * tq:(g + 1) * tq] = q_ref[:, g * HEAD_DIM:(g + 1) * HEAD_DIM].T
    for c in range(ncb):
        m_refs[c][...] = jnp.full(m_refs[c].shape, -jnp.inf, F32)
        acc_refs[c][...] = jnp.zeros(acc_refs[c].shape, F32)

    def body(j, carry):
        tiles = [(u, c) for u in range(unroll) for c in range(ncb)]
        ks, vts = [], []
        for u in range(unroll):
            off = pl.multiple_of((j * unroll + u) * tk, tk)
            ks.append(k_ref[pl.ds(off, tk), :])
            vts.append(vt_ref[j * unroll + u])

        def scores(t):
            u, c = tiles[t]
            return jnp.dot(ks[u], qt_ref[:, c * cb:(c + 1) * cb], preferred_element_type=F32)

        pending = [scores(t) for t in range(min(ahead, len(tiles)))]
        for t, (u, c) in enumerate(tiles):
            st = pending.pop(0)
            if t + ahead < len(tiles):
                pending.append(scores(t + ahead))
            m_old = m_refs[c][...]
            m_new = jnp.maximum(m_old, jnp.max(st, axis=0, keepdims=True))
            alpha = jnp.exp2(m_old - m_new)
            pt = jnp.exp2(st - m_new).astype(BF16)
            m_refs[c][...] = m_new
            acc_refs[c][...] = alpha * acc_refs[c][...] + jnp.dot(vts[u], pt, preferred_element_type=F32)
        return carry

    lax.fori_loop(0, nkv // unroll, body, 0)
    per_g = tq // cb
    for c in range(ncb):
        g, r = divmod(c, per_g)
        acc = acc_refs[c][...]
        out_t = acc[:HEAD_DIM, :] / acc[HEAD_DIM:HEAD_DIM + 1, :]
        o_ref[r * cb:(r + 1) * cb, g * HEAD_DIM:(g + 1) * HEAD_DIM] = out_t.T.astype(o_ref.dtype)


def _attention(qk, p, batch, seq, n_kv, k_col0, v_col0, *, tq=512, tk=256, cb=256, unroll=4, ahead=4):
    tq, tk = _tile(seq, tq), _tile(seq, tk)
    rows = GQA_RATIO * tq
    cb = _tile(rows, cb)
    unroll = _tile(seq // tk, unroll)
    qk3 = qk.reshape(batch, seq, qk.shape[1])
    p3 = p.reshape(batch, seq, p.shape[1])
    gw = GQA_RATIO * HEAD_DIM
    out = pl.pallas_call(
        functools.partial(_flash_kernel, tq=tq, tk=tk, cb=cb, seq=seq, unroll=unroll, ahead=ahead),
        grid=(batch, n_kv, seq // tq),
        in_specs=[
            pl.BlockSpec((None, tq, gw), lambda b, h, i: (b, i, h)),
            pl.BlockSpec((None, seq, HEAD_DIM), lambda b, h, i: (b, 0, k_col0 + h)),
            pl.BlockSpec((None, seq, HEAD_DIM), lambda b, h, i: (b, 0, v_col0 + h)),
        ],
        out_specs=pl.BlockSpec((None, tq, gw), lambda b, h, i: (b, i, h)),
        out_shape=jax.ShapeDtypeStruct((batch, seq, n_kv * gw), BF16),
        scratch_shapes=[
            pltpu.VMEM((HEAD_DIM, rows), BF16),
            pltpu.VMEM((seq // tk, HEAD_DIM + ONES_ROWS, tk), BF16),
        ] + [pltpu.VMEM((HEAD_DIM + ONES_ROWS, cb), F32)] * (rows // cb)
        + [pltpu.VMEM((1, cb), F32)] * (rows // cb),
        compiler_params=_params(("parallel", "parallel", "arbitrary")),
        name="flash_gqa",
    )(qk3, qk3, p3)
    return out.reshape(batch * seq, n_kv * gw)


def _conv_kernel(cb_ref, cc_ref, ch_ref, cbp_ref, chp_ref, cbn_ref, chn_ref, w_ref, o_ref, *, tm, sblocks):
    i = pl.program_id(0)
    first = (i % sblocks) == 0
    last = (i % sblocks) == sblocks - 1
    bx = cb_ref[...].astype(F32) * ch_ref[...].astype(F32)
    prev_row = cbp_ref[SUBLANES - 1:SUBLANES, :].astype(F32) * chp_ref[SUBLANES - 1:SUBLANES, :].astype(F32)
    next_row = cbn_ref[0:1, :].astype(F32) * chn_ref[0:1, :].astype(F32)
    prev_row = jnp.where(first, 0.0, prev_row)
    next_row = jnp.where(last, 0.0, next_row)
    row = lax.broadcasted_iota(jnp.int32, bx.shape, 0)
    prev = jnp.where(row == 0, prev_row, pltpu.roll(bx, 1, axis=0))
    nxt = jnp.where(row == tm - 1, next_row, pltpu.roll(bx, tm - 1, axis=0))
    w = w_ref[...]
    y = w[0:1, :] * prev + w[1:2, :] * bx + w[2:3, :] * nxt
    o_ref[...] = (cc_ref[...].astype(F32) * y).astype(o_ref.dtype)


def _short_conv(p, conv_w, layer, seq, cb0, cc0, ch0, width, *, tm=512, tc=1024):
    t = p.shape[0]
    tm, tc = _tile(seq, tm), math.gcd(cb0, cc0, ch0, width, tc)
    sblocks = seq // tm
    rpb = tm // SUBLANES
    nrb = t // SUBLANES

    def main(c0):
        return pl.BlockSpec((tm, tc), lambda i, j: (i, c0 // tc + j))

    def halo_prev(c0):
        return pl.BlockSpec((SUBLANES, tc), lambda i, j: (jnp.maximum(i * rpb - 1, 0), c0 // tc + j))

    def halo_next(c0):
        return pl.BlockSpec((SUBLANES, tc), lambda i, j: (jnp.minimum((i + 1) * rpb, nrb - 1), c0 // tc + j))

    return pl.pallas_call(
        functools.partial(_conv_kernel, tm=tm, sblocks=sblocks),
        grid=(t // tm, width // tc),
        in_specs=[main(cb0), main(cc0), main(ch0), halo_prev(cb0), halo_prev(ch0),
                  halo_next(cb0), halo_next(ch0),
                  pl.BlockSpec((None, CONV_K, tc), lambda i, j: (layer, 0, j))],
        out_specs=pl.BlockSpec((tm, tc), lambda i, j: (i, j)),
        out_shape=jax.ShapeDtypeStruct((t, width), BF16),
        compiler_params=_params(("parallel", "parallel")),
        name="short_conv",
    )(p, p, p, p, p, p, p, conv_w)


def _merge_kernel(a_ref, c_ref, wa_ref, wc_ref, ga_ref, gc_ref, o_ref):
    ya = jnp.dot(a_ref[...], wa_ref[...], preferred_element_type=F32)
    yc = jnp.dot(c_ref[...], wc_ref[...], preferred_element_type=F32)
    o_ref[...] = (jax.nn.sigmoid(ga_ref[...].astype(F32)) * ya
                  + jax.nn.sigmoid(gc_ref[...].astype(F32)) * yc).astype(o_ref.dtype)


def _merge(attn, conv, wa, wc, p, layer, ga0, gc0, *, tm=512, tn=1024):
    t, ka = attn.shape
    kc = conv.shape[1]
    d = wa.shape[2]
    tm, tn = _tile(t, tm), math.gcd(ga0, gc0, d, tn)
    return pl.pallas_call(
        _merge_kernel,
        grid=(t // tm, d // tn),
        in_specs=[
            pl.BlockSpec((tm, ka), lambda i, j: (i, 0)),
            pl.BlockSpec((tm, kc), lambda i, j: (i, 0)),
            pl.BlockSpec((None, ka, tn), lambda i, j: (layer, 0, j)),
            pl.BlockSpec((None, kc, tn), lambda i, j: (layer, 0, j)),
            pl.BlockSpec((tm, tn), lambda i, j: (i, ga0 // tn + j)),
            pl.BlockSpec((tm, tn), lambda i, j: (i, gc0 // tn + j)),
        ],
        out_specs=pl.BlockSpec((tm, tn), lambda i, j: (i, j)),
        out_shape=jax.ShapeDtypeStruct((t, d), BF16),
        compiler_params=_params(("parallel", "parallel")),
        name="gated_merge",
    )(attn, conv, wa, wc, p, p)


def _ln_rows(h, g, b):
    mu = jnp.mean(h, axis=-1, keepdims=True)
    hc = h - mu
    var = jnp.mean(hc * hc, axis=-1, keepdims=True)
    return hc * lax.rsqrt(var + LN_EPS) * g + b


def _ln_kernel(h_ref, g_ref, b_ref, o_ref, ob_ref):
    y = _ln_rows(h_ref[...], g_ref[...], b_ref[...])
    o_ref[...] = y
    ob_ref[...] = y.astype(BF16)


def _layer_norm(h, g, b, layer, *, tm=256):
    t, d = h.shape
    tm = _tile(t, tm)
    row = pl.BlockSpec((tm, d), lambda i: (i, 0))
    vec = pl.BlockSpec((None, 1, d), lambda i: (layer, 0, 0))
    return pl.pallas_call(
        _ln_kernel,
        grid=(t // tm,),
        in_specs=[row, vec, vec],
        out_specs=[row, row],
        out_shape=[jax.ShapeDtypeStruct((t, d), F32), jax.ShapeDtypeStruct((t, d), BF16)],
        compiler_params=_params(("parallel",)),
        name="layer_norm",
    )(h, g, b)


def _router_kernel(x_ref, w_ref, b_ref, idx_ref, gate_ref, *, n_exp):
    epg = n_exp // N_GROUPS
    logits = lax.dot_general(w_ref[...], x_ref[...], (((1,), (1,)), ((), ())),
                             precision=lax.Precision.HIGHEST, preferred_element_type=F32)
    mx = jnp.max(logits, axis=0, keepdims=True)
    ex = jnp.exp(logits - mx)
    scores = ex / jnp.sum(ex, axis=0, keepdims=True)
    sel = scores + b_ref[...]
    rows_sel = [sel[e:e + 1, :] for e in range(n_exp)]
    rows_sc = [scores[e:e + 1, :] for e in range(n_exp)]
    best = None
    for g in range(N_GROUPS):
        mem = rows_sel[g * epg:(g + 1) * epg]
        gs = None
        for a in range(epg):
            for c in range(a + 1, epg):
                pair = mem[a] + mem[c]
                gs = pair if gs is None else jnp.maximum(gs, pair)
        if best is None:
            best, grp = gs, jnp.zeros(gs.shape, jnp.int32)
        else:
            upd = gs > best
            best = jnp.where(upd, gs, best)
            grp = jnp.where(upd, g, grp)
    cand_sel, cand_sc = [], []
    for j in range(epg):
        cs, cc = rows_sel[j], rows_sc[j]
        for g in range(1, N_GROUPS):
            cs = jnp.where(grp == g, rows_sel[g * epg + j], cs)
            cc = jnp.where(grp == g, rows_sc[g * epg + j], cc)
        cand_sel.append(cs)
        cand_sc.append(cc)

    def first_argmax(vals, skip):
        bv = bi = bs = None
        for j in range(epg):
            v = vals[j] if skip is None else jnp.where(skip == j, -jnp.inf, vals[j])
            if bv is None:
                bv, bi, bs = v, jnp.zeros(v.shape, jnp.int32), cand_sc[0]
            else:
                upd = v > bv
                bv = jnp.where(upd, v, bv)
                bi = jnp.where(upd, j, bi)
                bs = jnp.where(upd, cand_sc[j], bs)
        return bi, bs

    i1, s1 = first_argmax(cand_sel, None)
    i2, s2 = first_argmax(cand_sel, i1)
    tot = s1 + s2
    zi = jnp.zeros((SUBLANES - TOP_K,) + i1.shape[1:], jnp.int32)
    zf = jnp.zeros((SUBLANES - TOP_K,) + i1.shape[1:], F32)
    idx_ref[...] = jnp.concatenate([grp * epg + i1, grp * epg + i2, zi], axis=0)
    gate_ref[...] = jnp.concatenate([s1 / tot, s2 / tot, zf], axis=0)


def _router(x, w_router_t, bias_col, *, tm=512):
    t, d = x.shape
    n_exp = w_router_t.shape[0]
    tm = _tile(t, tm)
    out = pl.BlockSpec((SUBLANES, tm), lambda i: (0, i))
    return pl.pallas_call(
        functools.partial(_router_kernel, n_exp=n_exp),
        grid=(t // tm,),
        in_specs=[
            pl.BlockSpec((tm, d), lambda i: (i, 0)),
            pl.BlockSpec((n_exp, d), lambda i: (0, 0)),
            pl.BlockSpec((n_exp, 1), lambda i: (0, 0)),
        ],
        out_specs=[out, out],
        out_shape=[jax.ShapeDtypeStruct((SUBLANES, t), jnp.int32),
                   jax.ShapeDtypeStruct((SUBLANES, t), F32)],
        compiler_params=_params(("parallel",)),
        name="router",
    )(x, w_router_t, bias_col)


def _expert_kernel(be_ref, nu_ref, x_ref, wg_ref, wu_ref, wd_ref, o_ref, *, kc, nc):
    i = pl.program_id(0)
    f = pl.program_id(1)
    d = x_ref.shape[1]

    @pl.when(f == 0)
    def _():
        o_ref[...] = jnp.zeros_like(o_ref)

    @pl.when(i < nu_ref[0])
    def _():
        hg = hu = None
        for k0 in range(0, d, kc):
            xk = x_ref[:, k0:k0 + kc]
            g = jnp.dot(xk, wg_ref[k0:k0 + kc, :].astype(BF16), preferred_element_type=F32)
            u = jnp.dot(xk, wu_ref[k0:k0 + kc, :].astype(BF16), preferred_element_type=F32)
            hg = g if hg is None else hg + g
            hu = u if hu is None else hu + u
        hid = (hg * jax.nn.sigmoid(hg) * hu).astype(BF16)
        for n0 in range(0, d, nc):
            o_ref[:, n0:n0 + nc] += jnp.dot(hid, wd_ref[:, n0:n0 + nc].astype(BF16),
                                            preferred_element_type=F32)


def _experts(block_e, n_used, xs, wg, wu, wd, layer, tm, *, tf=256, kc=512, nc=1024):
    r, d = xs.shape
    ff = wg.shape[3]
    tf = _tile(ff, tf)
    nblk = r // tm
    grid_spec = pltpu.PrefetchScalarGridSpec(
        num_scalar_prefetch=2,
        grid=(nblk, ff // tf),
        in_specs=[
            pl.BlockSpec((tm, d), lambda i, f, be, nu: (i, 0)),
            pl.BlockSpec((None, None, d, tf), lambda i, f, be, nu: (layer, be[i], 0, f)),
            pl.BlockSpec((None, None, d, tf), lambda i, f, be, nu: (layer, be[i], 0, f)),
            pl.BlockSpec((None, None, tf, d), lambda i, f, be, nu: (layer, be[i], f, 0)),
        ],
        out_specs=pl.BlockSpec((tm, d), lambda i, f, be, nu: (i, 0)),
    )
    return pl.pallas_call(
        functools.partial(_expert_kernel, kc=_tile(d, kc), nc=_tile(d, nc)),
        grid_spec=grid_spec,
        out_shape=jax.ShapeDtypeStruct((r, d), F32),
        compiler_params=_params(("parallel", "arbitrary")),
        name="expert_ffn",
    )(block_e, n_used, xs, wg, wu, wd)


def _combine_kernel(x_ref, ya_ref, yb_ref, gt_ref, g_ref, b_ref, o_ref, ob_ref, *, alpha):
    gt = gt_ref[...]
    h = alpha * x_ref[...] + gt[:, 0:1] * ya_ref[...] + gt[:, 1:2] * yb_ref[...]
    y = _ln_rows(h, g_ref[...], b_ref[...])
    o_ref[...] = y
    ob_ref[...] = y.astype(BF16)


def _combine_ln(x, ya, yb, gates_t, g, b, layer, alpha, *, tm=256):
    t, d = x.shape
    tm = _tile(t, tm)
    row = pl.BlockSpec((tm, d), lambda i: (i, 0))
    vec = pl.BlockSpec((None, 1, d), lambda i: (layer, 0, 0))
    return pl.pallas_call(
        functools.partial(_combine_kernel, alpha=alpha),
        grid=(t // tm,),
        in_specs=[row, row, row, pl.BlockSpec((tm, SUBLANES), lambda i: (i, 0)), vec, vec],
        out_specs=[row, row],
        out_shape=[jax.ShapeDtypeStruct((t, d), F32), jax.ShapeDtypeStruct((t, d), BF16)],
        compiler_params=_params(("parallel",)),
        name="combine_ln",
    )(x, ya, yb, gates_t, g, b)


def _rope_tables(seq):
    axis_dim = HEAD_DIM // 2
    rows = seq // GRID_W
    inv = ROPE_THETA ** (-jnp.arange(0, axis_dim, 2, dtype=F32) / axis_dim)
    ang_r = jnp.repeat(jnp.arange(rows, dtype=F32), GRID_W)[:, None] * inv[None, :]
    ang_c = jnp.tile(jnp.arange(GRID_W, dtype=F32), rows)[:, None] * inv[None, :]
    cr, sr, cc, sc = jnp.cos(ang_r), jnp.sin(ang_r), jnp.cos(ang_c), jnp.sin(ang_c)
    cos_t = jnp.concatenate([cr, cr, cc, cc], axis=1)
    sin_t = jnp.concatenate([-sr, sr, -sc, sc], axis=1)
    return cos_t, sin_t


def _dispatch(idx, n_exp, tm):
    t = idx.shape[1]
    n_asg = t * TOP_K
    e_flat = idx.T.reshape(-1)
    onehot = (e_flat[:, None] == jnp.arange(n_exp, dtype=jnp.int32)[None, :]).astype(jnp.int32)
    csum = jnp.cumsum(onehot, axis=0)
    rank = jnp.sum(csum * onehot, axis=1) - 1
    counts = csum[-1]
    pcounts = (counts + tm - 1) // tm * tm
    pends = jnp.cumsum(pcounts)
    pstarts = pends - pcounts
    slot = (pstarts[e_flat] + rank).astype(jnp.int32)
    nblk = n_asg // tm + n_exp
    row_token = jnp.zeros((nblk * tm,), jnp.int32).at[slot].set(jnp.arange(n_asg, dtype=jnp.int32) // TOP_K)
    block_start = jnp.arange(nblk, dtype=jnp.int32) * tm
    block_e = jnp.minimum(jnp.sum(pends[None, :] <= block_start[:, None], axis=-1), n_exp - 1).astype(jnp.int32)
    n_used = (pends[-1] // tm).astype(jnp.int32).reshape(1)
    return slot.reshape(t, TOP_K), row_token, block_e, n_used


def kernel(x, w_in, q_norm_g, k_norm_g, conv_w, w_attn_proj, w_conv_proj, w_out, ln1_g, ln1_b,
           w_router, router_bias, w_gate, w_up, w_down, ln2_g, ln2_b):
    batch, seq, d = x.shape
    depth = w_in.shape[0]
    t = batch * seq
    attn_w = d // 2
    n_q = attn_w // HEAD_DIM
    n_kv = n_q // GQA_RATIO
    kv_w = n_kv * HEAD_DIM
    conv_wd = d // 2
    n_exp = w_router.shape[1]
    alpha = (2 * depth) ** 0.25
    q_end = attn_w
    k_end = q_end + kv_w
    v_end = k_end + kv_w
    cb_end = v_end + conv_wd
    cc_end = cb_end + conv_wd
    ch_end = cc_end + conv_wd
    ga_end = ch_end + d
    expert_tm = min(512, t)

    cos_t, sin_t = _rope_tables(seq)
    scale = HEAD_DIM ** -0.5 * math.log2(math.e)
    w_in_b = w_in.astype(BF16)
    wa_b = w_attn_proj.astype(BF16)
    wc_b = w_conv_proj.astype(BF16)
    wo_b = w_out.astype(BF16)
    w_router_t = w_router.T
    bias_col = router_bias.reshape(n_exp, 1).astype(F32)
    ln1_g3, ln1_b3 = ln1_g.reshape(depth, 1, d), ln1_b.reshape(depth, 1, d)
    ln2_g3, ln2_b3 = ln2_g.reshape(depth, 1, d), ln2_b.reshape(depth, 1, d)

    xf = x.reshape(t, d)
    xb = xf.astype(BF16)
    for l in range(depth):
        p = _matmul(xb, w_in_b, l, BF16, name="in_proj")
        gains = jnp.concatenate([jnp.tile(q_norm_g[l] * scale, n_q), jnp.tile(k_norm_g[l], n_kv)]).reshape(1, k_end)
        qk = _qk_prep(p, gains.astype(F32), cos_t, sin_t, seq, k_end)
        attn = _attention(qk, p, batch, seq, n_kv, q_end // HEAD_DIM, k_end // HEAD_DIM)
        conv = _short_conv(p, conv_w, l, seq, v_end, cb_end, cc_end, conv_wd)
        merged = _merge(attn, conv, wa_b, wc_b, p, l, ch_end, ga_end)
        h1 = _matmul(merged, wo_b, l, F32, res=xf, alpha=alpha, tn=512, name="out_proj")
        x1, x1b = _layer_norm(h1, ln1_g3, ln1_b3, l)
        idx, gate = _router(x1, w_router_t, bias_col)
        slot, row_token, block_e, n_used = _dispatch(idx[:TOP_K], n_exp, expert_tm)
        xs = x1b.at[row_token].get(mode="promise_in_bounds")
        y = _experts(block_e, n_used, xs, w_gate, w_up, w_down, l, expert_tm)
        ya = y.at[slot[:, 0]].get(mode="promise_in_bounds")
        yb = y.at[slot[:, 1]].get(mode="promise_in_bounds")
        xf, xb = _combine_ln(x1, ya, yb, gate.T, ln2_g3, ln2_b3, l, alpha)
    return xf.reshape(batch, seq, d)
```

```python
import functools
import math

import jax
import jax.numpy as jnp
from jax import lax
from jax.experimental import pallas as pl
from jax.experimental.pallas import tpu as pltpu

HEAD_DIM = 128
GQA_RATIO = 4
CONV_K = 3
GRID_W = 64
ROPE_THETA = 10000.0
N_GROUPS = 4
TOP_K = 2
RMS_EPS = 1e-6
LN_EPS = 1e-5
LANES = 128
SUBLANES = 8
ONES_ROWS = 2 * SUBLANES
VMEM_LIMIT = 56 * 1024 * 1024

F32 = jnp.float32
BF16 = jnp.bfloat16


def _params(semantics):
    return pltpu.CompilerParams(dimension_semantics=semantics, vmem_limit_bytes=VMEM_LIMIT)


def _tile(dim, pref):
    t = min(dim, pref)
    while dim % t:
        t //= 2
    return t


def _mm_kernel(x_ref, w_ref, o_ref):
    o_ref[...] = jnp.dot(x_ref[...], w_ref[...], preferred_element_type=F32).astype(o_ref.dtype)


def _mm_res_kernel(x_ref, w_ref, r_ref, o_ref, *, alpha):
    acc = jnp.dot(x_ref[...], w_ref[...], preferred_element_type=F32)
    o_ref[...] = (alpha * r_ref[...] + acc).astype(o_ref.dtype)


def _matmul(x, w, layer, out_dtype, *, res=None, alpha=1.0, tm=1024, tn=1024, name="matmul"):
    m, k = x.shape
    n = w.shape[2]
    tm, tn = _tile(m, tm), _tile(n, tn)
    in_specs = [
        pl.BlockSpec((tm, k), lambda i, j: (i, 0)),
        pl.BlockSpec((None, k, tn), lambda i, j: (layer, 0, j)),
    ]
    args = [x, w]
    body = _mm_kernel
    if res is not None:
        in_specs.append(pl.BlockSpec((tm, tn), lambda i, j: (i, j)))
        args.append(res)
        body = functools.partial(_mm_res_kernel, alpha=alpha)
    return pl.pallas_call(
        body,
        grid=(m // tm, n // tn),
        in_specs=in_specs,
        out_specs=pl.BlockSpec((tm, tn), lambda i, j: (i, j)),
        out_shape=jax.ShapeDtypeStruct((m, n), out_dtype),
        compiler_params=_params(("parallel", "parallel")),
        name=name,
    )(*args)


def _swap_halves(y):
    lane = lax.broadcasted_iota(jnp.int32, y.shape, 1)
    fwd = pltpu.roll(y, LANES - 32, axis=1)
    bwd = pltpu.roll(y, 32, axis=1)
    return jnp.where((lane % 64) < 32, fwd, bwd)


def _qk_prep_kernel(p_ref, g_ref, c_ref, s_ref, o_ref, *, heads):
    cos = c_ref[...]
    sin = s_ref[...]
    for h in range(heads):
        sl = slice(h * HEAD_DIM, (h + 1) * HEAD_DIM)
        xh = p_ref[:, sl].astype(F32)
        ms = jnp.mean(xh * xh, axis=-1, keepdims=True)
        y = xh * lax.rsqrt(ms + RMS_EPS) * g_ref[:, sl]
        o_ref[:, sl] = (y * cos + _swap_halves(y) * sin).astype(o_ref.dtype)


def _qk_prep(p, gains, cos_t, sin_t, seq, qk_width, *, tm=512, heads=4):
    t = p.shape[0]
    tm = _tile(seq, tm)
    n_heads = qk_width // HEAD_DIM
    heads = _tile(n_heads, heads)
    wblk = heads * HEAD_DIM
    sblocks = seq // tm
    return pl.pallas_call(
        functools.partial(_qk_prep_kernel, heads=heads),
        grid=(t // tm, n_heads // heads),
        in_specs=[
            pl.BlockSpec((tm, wblk), lambda i, j: (i, j)),
            pl.BlockSpec((1, wblk), lambda i, j: (0, j)),
            pl.BlockSpec((tm, HEAD_DIM), lambda i, j: (i % sblocks, 0)),
            pl.BlockSpec((tm, HEAD_DIM), lambda i, j: (i % sblocks, 0)),
        ],
        out_specs=pl.BlockSpec((tm, wblk), lambda i, j: (i, j)),
        out_shape=jax.ShapeDtypeStruct((t, qk_width), BF16),
        compiler_params=_params(("parallel", "parallel")),
        name="qk_prep",
    )(p, gains, cos_t, sin_t)


def _flash_kernel(q_ref, k_ref, v_ref, o_ref, qt_ref, vt_ref, *stat_refs, tq, tk, cb, seq, unroll, ahead):
    nkv = seq // tk
    ncb = GQA_RATIO * tq // cb
    acc_refs, m_refs = stat_refs[:ncb], stat_refs[ncb:]

    @pl.when(pl.program_id(2) == 0)
    def _():
        for c in range(nkv):
            vt_ref[c, :HEAD_DIM, :] = v_ref[c * tk:(c + 1) * tk, :].T
            vt_ref[c, HEAD_DIM:, :] = jnp.ones((ONES_ROWS, tk), BF16)

    for g in range(GQA_RATIO):
        qt_ref[:, g * tq:(g + 1) * tq] = q_ref[:, g * HEAD_DIM:(g + 1) * HEAD_DIM].T
    for c in range(ncb):
        m_refs[c][...] = jnp.full(m_refs[c].shape, -jnp.inf, F32)
        acc_refs[c][...] = jnp.zeros(acc_refs[c].shape, F32)

    def body(j, carry):
        tiles = [(u, c) for u in range(unroll) for c in range(ncb)]
        ks, vts = [], []
        for u in range(unroll):
            off = pl.multiple_of((j * unroll + u) * tk, tk)
            ks.append(k_ref[pl.ds(off, tk), :])
            vts.append(vt_ref[j * unroll + u])

        def scores(t):
            u, c = tiles[t]
            return jnp.dot(ks[u], qt_ref[:, c * cb:(c + 1) * cb], preferred_element_type=F32)

        pending = [scores(t) for t in range(min(ahead, len(tiles)))]
        for t, (u, c) in enumerate(tiles):
            st = pending.pop(0)
            if t + ahead < len(tiles):
                pending.append(scores(t + ahead))
            m_old = m_refs[c][...]
            m_new = jnp.maximum(m_old, jnp.max(st, axis=0, keepdims=True))
            alpha = jnp.exp2(m_old - m_new)
            pt = jnp.exp2(st - m_new).astype(BF16)
            m_refs[c][...] = m_new
            acc_refs[c][...] = alpha * acc_refs[c][...] + jnp.dot(vts[u], pt, preferred_element_type=F32)
        return carry

    lax.fori_loop(0, nkv // unroll, body, 0)
    per_g = tq // cb
    for c in range(ncb):
        g, r = divmod(c, per_g)
        acc = acc_refs[c][...]
        out_t = acc[:HEAD_DIM, :] / acc[HEAD_DIM:HEAD_DIM + 1, :]
        o_ref[r * cb:(r + 1) * cb, g * HEAD_DIM:(g + 1) * HEAD_DIM] = out_t.T.astype(o_ref.dtype)


def _attention(qk, p, batch, seq, n_kv, k_col0, v_col0, *, tq=512, tk=256, cb=256, unroll=8, ahead=5):
    tq, tk = _tile(seq, tq), _tile(seq, tk)
    rows = GQA_RATIO * tq
    cb = _tile(rows, cb)
    unroll = _tile(seq // tk, unroll)
    qk3 = qk.reshape(batch, seq, qk.shape[1])
    p3 = p.reshape(batch, seq, p.shape[1])
    gw = GQA_RATIO * HEAD_DIM
    out = pl.pallas_call(
        functools.partial(_flash_kernel, tq=tq, tk=tk, cb=cb, seq=seq, unroll=unroll, ahead=ahead),
        grid=(batch, n_kv, seq // tq),
        in_specs=[
            pl.BlockSpec((None, tq, gw), lambda b, h, i: (b, i, h)),
            pl.BlockSpec((None, seq, HEAD_DIM), lambda b, h, i: (b, 0, k_col0 + h)),
            pl.BlockSpec((None, seq, HEAD_DIM), lambda b, h, i: (b, 0, v_col0 + h)),
        ],
        out_specs=pl.BlockSpec((None, tq, gw), lambda b, h, i: (b, i, h)),
        out_shape=jax.ShapeDtypeStruct((batch, seq, n_kv * gw), BF16),
        scratch_shapes=[
            pltpu.VMEM((HEAD_DIM, rows), BF16),
            pltpu.VMEM((seq // tk, HEAD_DIM + ONES_ROWS, tk), BF16),
        ] + [pltpu.VMEM((HEAD_DIM + ONES_ROWS, cb), F32)] * (rows // cb)
        + [pltpu.VMEM((1, cb), F32)] * (rows // cb),
        compiler_params=_params(("parallel", "parallel", "arbitrary")),
        name="flash_gqa",
    )(qk3, qk3, p3)
    return out.reshape(batch * seq, n_kv * gw)


def _conv_kernel(cb_ref, cc_ref, ch_ref, cbp_ref, chp_ref, cbn_ref, chn_ref, w_ref, o_ref, *, tm, sblocks):
    i = pl.program_id(0)
    first = (i % sblocks) == 0
    last = (i % sblocks) == sblocks - 1
    bx = cb_ref[...].astype(F32) * ch_ref[...].astype(F32)
    prev_row = cbp_ref[SUBLANES - 1:SUBLANES, :].astype(F32) * chp_ref[SUBLANES - 1:SUBLANES, :].astype(F32)
    next_row = cbn_ref[0:1, :].astype(F32) * chn_ref[0:1, :].astype(F32)
    prev_row = jnp.where(first, 0.0, prev_row)
    next_row = jnp.where(last, 0.0, next_row)
    row = lax.broadcasted_iota(jnp.int32, bx.shape, 0)
    prev = jnp.where(row == 0, prev_row, pltpu.roll(bx, 1, axis=0))
    nxt = jnp.where(row == tm - 1, next_row, pltpu.roll(bx, tm - 1, axis=0))
    w = w_ref[...]
    y = w[0:1, :] * prev + w[1:2, :] * bx + w[2:3, :] * nxt
    o_ref[...] = (cc_ref[...].astype(F32) * y).astype(o_ref.dtype)


def _short_conv(p, conv_w, layer, seq, cb0, cc0, ch0, width, *, tm=512, tc=1024):
    t = p.shape[0]
    tm, tc = _tile(seq, tm), math.gcd(cb0, cc0, ch0, width, tc)
    sblocks = seq // tm
    rpb = tm // SUBLANES
    nrb = t // SUBLANES

    def main(c0):
        return pl.BlockSpec((tm, tc), lambda i, j: (i, c0 // tc + j))

    def halo_prev(c0):
        return pl.BlockSpec((SUBLANES, tc), lambda i, j: (jnp.maximum(i * rpb - 1, 0), c0 // tc + j))

    def halo_next(c0):
        return pl.BlockSpec((SUBLANES, tc), lambda i, j: (jnp.minimum((i + 1) * rpb, nrb - 1), c0 // tc + j))

    return pl.pallas_call(
        functools.partial(_conv_kernel, tm=tm, sblocks=sblocks),
        grid=(t // tm, width // tc),
        in_specs=[main(cb0), main(cc0), main(ch0), halo_prev(cb0), halo_prev(ch0),
                  halo_next(cb0), halo_next(ch0),
                  pl.BlockSpec((None, CONV_K, tc), lambda i, j: (layer, 0, j))],
        out_specs=pl.BlockSpec((tm, tc), lambda i, j: (i, j)),
        out_shape=jax.ShapeDtypeStruct((t, width), BF16),
        compiler_params=_params(("parallel", "parallel")),
        name="short_conv",
    )(p, p, p, p, p, p, p, conv_w)


def _merge_kernel(a_ref, c_ref, wa_ref, wc_ref, ga_ref, gc_ref, o_ref):
    ya = jnp.dot(a_ref[...], wa_ref[...], preferred_element_type=F32)
    yc = jnp.dot(c_ref[...], wc_ref[...], preferred_element_type=F32)
    o_ref[...] = (jax.nn.sigmoid(ga_ref[...].astype(F32)) * ya
                  + jax.nn.sigmoid(gc_ref[...].astype(F32)) * yc).astype(o_ref.dtype)


def _merge(attn, conv, wa, wc, p, layer, ga0, gc0, *, tm=512, tn=1024):
    t, ka = attn.shape
    kc = conv.shape[1]
    d = wa.shape[2]
    tm, tn = _tile(t, tm), math.gcd(ga0, gc0, d, tn)
    return pl.pallas_call(
        _merge_kernel,
        grid=(t // tm, d // tn),
        in_specs=[
            pl.BlockSpec((tm, ka), lambda i, j: (i, 0)),
            pl.BlockSpec((tm, kc), lambda i, j: (i, 0)),
            pl.BlockSpec((None, ka, tn), lambda i, j: (layer, 0, j)),
            pl.BlockSpec((None, kc, tn), lambda i, j: (layer, 0, j)),
            pl.BlockSpec((tm, tn), lambda i, j: (i, ga0 // tn + j)),
            pl.BlockSpec((tm, tn), lambda i, j: (i, gc0 // tn + j)),
        ],
        out_specs=pl.BlockSpec((tm, tn), lambda i, j: (i, j)),
        out_shape=jax.ShapeDtypeStruct((t, d), BF16),
        compiler_params=_params(("parallel", "parallel")),
        name="gated_merge",
    )(attn, conv, wa, wc, p, p)


def _ln_rows(h, g, b):
    mu = jnp.mean(h, axis=-1, keepdims=True)
    hc = h - mu
    var = jnp.mean(hc * hc, axis=-1, keepdims=True)
    return hc * lax.rsqrt(var + LN_EPS) * g + b


def _ln_kernel(h_ref, g_ref, b_ref, o_ref):
    o_ref[...] = _ln_rows(h_ref[...], g_ref[...], b_ref[...])


def _layer_norm(h, g, b, layer, *, tm=256):
    t, d = h.shape
    tm = _tile(t, tm)
    row = pl.BlockSpec((tm, d), lambda i: (i, 0))
    vec = pl.BlockSpec((None, 1, d), lambda i: (layer, 0, 0))
    return pl.pallas_call(
        _ln_kernel,
        grid=(t // tm,),
        in_specs=[row, vec, vec],
        out_specs=row,
        out_shape=jax.ShapeDtypeStruct((t, d), F32),
        compiler_params=_params(("parallel",)),
        name="layer_norm",
    )(h, g, b)


def _router_kernel(x_ref, w_ref, b_ref, idx_ref, gate_ref, *, n_exp):
    epg = n_exp // N_GROUPS
    logits = lax.dot_general(w_ref[...], x_ref[...], (((1,), (1,)), ((), ())),
                             precision=lax.Precision.HIGHEST, preferred_element_type=F32)
    mx = jnp.max(logits, axis=0, keepdims=True)
    ex = jnp.exp(logits - mx)
    scores = ex / jnp.sum(ex, axis=0, keepdims=True)
    sel = scores + b_ref[...]
    rows_sel = [sel[e:e + 1, :] for e in range(n_exp)]
    rows_sc = [scores[e:e + 1, :] for e in range(n_exp)]
    best = None
    for g in range(N_GROUPS):
        mem = rows_sel[g * epg:(g + 1) * epg]
        gs = None
        for a in range(epg):
            for c in range(a + 1, epg):
                pair = mem[a] + mem[c]
                gs = pair if gs is None else jnp.maximum(gs, pair)
        if best is None:
            best, grp = gs, jnp.zeros(gs.shape, jnp.int32)
        else:
            upd = gs > best
            best = jnp.where(upd, gs, best)
            grp = jnp.where(upd, g, grp)
    cand_sel, cand_sc = [], []
    for j in range(epg):
        cs, cc = rows_sel[j], rows_sc[j]
        for g in range(1, N_GROUPS):
            cs = jnp.where(grp == g, rows_sel[g * epg + j], cs)
            cc = jnp.where(grp == g, rows_sc[g * epg + j], cc)
        cand_sel.append(cs)
        cand_sc.append(cc)

    def first_argmax(vals, skip):
        bv = bi = bs = None
        for j in range(epg):
            v = vals[j] if skip is None else jnp.where(skip == j, -jnp.inf, vals[j])
            if bv is None:
                bv, bi, bs = v, jnp.zeros(v.shape, jnp.int32), cand_sc[0]
            else:
                upd = v > bv
                bv = jnp.where(upd, v, bv)
                bi = jnp.where(upd, j, bi)
                bs = jnp.where(upd, cand_sc[j], bs)
        return bi, bs

    i1, s1 = first_argmax(cand_sel, None)
    i2, s2 = first_argmax(cand_sel, i1)
    tot = s1 + s2
    zi = jnp.zeros((SUBLANES - TOP_K,) + i1.shape[1:], jnp.int32)
    zf = jnp.zeros((SUBLANES - TOP_K,) + i1.shape[1:], F32)
    idx_ref[...] = jnp.concatenate([grp * epg + i1, grp * epg + i2, zi], axis=0)
    gate_ref[...] = jnp.concatenate([s1 / tot, s2 / tot, zf], axis=0)


def _router(x, w_router_t, bias_col, *, tm=512):
    t, d = x.shape
    n_exp = w_router_t.shape[0]
    tm = _tile(t, tm)
    out = pl.BlockSpec((SUBLANES, tm), lambda i: (0, i))
    return pl.pallas_call(
        functools.partial(_router_kernel, n_exp=n_exp),
        grid=(t // tm,),
        in_specs=[
            pl.BlockSpec((tm, d), lambda i: (i, 0)),
            pl.BlockSpec((n_exp, d), lambda i: (0, 0)),
            pl.BlockSpec((n_exp, 1), lambda i: (0, 0)),
        ],
        out_specs=[out, out],
        out_shape=[jax.ShapeDtypeStruct((SUBLANES, t), jnp.int32),
                   jax.ShapeDtypeStruct((SUBLANES, t), F32)],
        compiler_params=_params(("parallel",)),
        name="router",
    )(x, w_router_t, bias_col)


def _expert_kernel(be_ref, nu_ref, rt_ref, x_hbm, wg_ref, wu_ref, wd_ref, o_ref, xg_ref, xb_ref, sem,
                   *, tm, kc, nc):
    i = pl.program_id(0)
    f = pl.program_id(1)
    n_used = nu_ref[0]
    d = xb_ref.shape[1]

    def start_gather(blk):
        base = blk * tm

        def issue(r, c):
            pltpu.make_async_copy(x_hbm.at[pl.ds(rt_ref[base + r], 1), :], xg_ref.at[pl.ds(r, 1), :],
                                  sem.at[0]).start()
            return c

        lax.fori_loop(0, tm, issue, 0, unroll=8)

    @pl.when(f == 0)
    def _():
        o_ref[...] = jnp.zeros_like(o_ref)

        @pl.when(jnp.logical_and(i == 0, n_used > 0))
        def _():
            start_gather(0)

        @pl.when(i < n_used)
        def _():
            pltpu.make_async_copy(x_hbm.at[pl.ds(0, tm), :], xg_ref, sem.at[0]).wait()
            xb_ref[...] = xg_ref[...].astype(BF16)

        @pl.when(i + 1 < n_used)
        def _():
            start_gather(i + 1)

    @pl.when(i < n_used)
    def _():
        hg = hu = None
        for k0 in range(0, d, kc):
            xk = xb_ref[:, k0:k0 + kc]
            g = jnp.dot(xk, wg_ref[k0:k0 + kc, :].astype(BF16), preferred_element_type=F32)
            u = jnp.dot(xk, wu_ref[k0:k0 + kc, :].astype(BF16), preferred_element_type=F32)
            hg = g if hg is None else hg + g
            hu = u if hu is None else hu + u
        hid = (hg * jax.nn.sigmoid(hg) * hu).astype(BF16)
        for n0 in range(0, d, nc):
            o_ref[:, n0:n0 + nc] += jnp.dot(hid, wd_ref[:, n0:n0 + nc].astype(BF16),
                                            preferred_element_type=F32)


def _experts(block_e, n_used, row_token, x, wg, wu, wd, layer, tm, *, tf=256, kc=512, nc=1024):
    d = x.shape[1]
    r = row_token.shape[0]
    ff = wg.shape[3]
    tf = _tile(ff, tf)
    nblk = r // tm
    grid_spec = pltpu.PrefetchScalarGridSpec(
        num_scalar_prefetch=3,
        grid=(nblk, ff // tf),
        in_specs=[
            pl.BlockSpec(memory_space=pl.ANY),
            pl.BlockSpec((None, None, d, tf), lambda i, f, be, nu, rt: (layer, be[i], 0, f)),
            pl.BlockSpec((None, None, d, tf), lambda i, f, be, nu, rt: (layer, be[i], 0, f)),
            pl.BlockSpec((None, None, tf, d), lambda i, f, be, nu, rt: (layer, be[i], f, 0)),
        ],
        out_specs=pl.BlockSpec((tm, d), lambda i, f, be, nu, rt: (i, 0)),
        scratch_shapes=[pltpu.VMEM((tm, d), F32), pltpu.VMEM((tm, d), BF16), pltpu.SemaphoreType.DMA((1,))],
    )
    return pl.pallas_call(
        functools.partial(_expert_kernel, tm=tm, kc=_tile(d, kc), nc=_tile(d, nc)),
        grid_spec=grid_spec,
        out_shape=jax.ShapeDtypeStruct((r, d), F32),
        compiler_params=_params(("arbitrary", "arbitrary")),
        name="expert_ffn",
    )(block_e, n_used, row_token, x, wg, wu, wd)


def _combine_kernel(slot_ref, x_ref, y_hbm, gt_ref, g_ref, b_ref, o_ref, ob_ref, ybuf, sem, *, alpha, tm):
    i = pl.program_id(0)
    n = pl.num_programs(0)

    def row_copy(row, s, k, r):
        return pltpu.make_async_copy(y_hbm.at[pl.ds(row, 1), :], ybuf.at[s, k, pl.ds(r, 1), :], sem.at[s])

    def start_gather(step, s):
        base = step * (tm * TOP_K)

        def issue(r, c):
            for k in range(TOP_K):
                row_copy(slot_ref[base + r * TOP_K + k], s, k, r).start()
            return c

        lax.fori_loop(0, tm, issue, 0, unroll=4)

    def wait_gather(s):
        for k in range(TOP_K):
            pltpu.make_async_copy(y_hbm.at[pl.ds(0, tm), :], ybuf.at[s, k], sem.at[s]).wait()

    @pl.when(i == 0)
    def _():
        start_gather(0, 0)

    @pl.when(i + 1 < n)
    def _():
        start_gather(i + 1, (i + 1) % 2)

    s = i % 2
    wait_gather(s)
    gt = gt_ref[...]
    h = alpha * x_ref[...] + gt[:, 0:1] * ybuf[s, 0] + gt[:, 1:2] * ybuf[s, 1]
    y = _ln_rows(h, g_ref[...], b_ref[...])
    o_ref[...] = y
    ob_ref[...] = y.astype(BF16)


def _combine_ln(slot, x, y, gates_t, g, b, layer, alpha, *, tm=256):
    t, d = x.shape
    tm = _tile(t, tm)
    row = pl.BlockSpec((tm, d), lambda i, sl: (i, 0))
    vec = pl.BlockSpec((None, 1, d), lambda i, sl: (layer, 0, 0))
    grid_spec = pltpu.PrefetchScalarGridSpec(
        num_scalar_prefetch=1,
        grid=(t // tm,),
        in_specs=[row, pl.BlockSpec(memory_space=pl.ANY),
                  pl.BlockSpec((tm, SUBLANES), lambda i, sl: (i, 0)), vec, vec],
        out_specs=[row, row],
        scratch_shapes=[pltpu.VMEM((2, TOP_K, tm, d), F32), pltpu.SemaphoreType.DMA((2,))],
    )
    return pl.pallas_call(
        functools.partial(_combine_kernel, alpha=alpha, tm=tm),
        grid_spec=grid_spec,
        out_shape=[jax.ShapeDtypeStruct((t, d), F32), jax.ShapeDtypeStruct((t, d), BF16)],
        compiler_params=_params(("arbitrary",)),
        name="combine_ln",
    )(slot, x, y, gates_t, g, b)


def _rope_tables(seq):
    axis_dim = HEAD_DIM // 2
    rows = seq // GRID_W
    inv = ROPE_THETA ** (-jnp.arange(0, axis_dim, 2, dtype=F32) / axis_dim)
    ang_r = jnp.repeat(jnp.arange(rows, dtype=F32), GRID_W)[:, None] * inv[None, :]
    ang_c = jnp.tile(jnp.arange(GRID_W, dtype=F32), rows)[:, None] * inv[None, :]
    cr, sr, cc, sc = jnp.cos(ang_r), jnp.sin(ang_r), jnp.cos(ang_c), jnp.sin(ang_c)
    cos_t = jnp.concatenate([cr, cr, cc, cc], axis=1)
    sin_t = jnp.concatenate([-sr, sr, -sc, sc], axis=1)
    return cos_t, sin_t


def _dispatch(idx, n_exp, tm):
    t = idx.shape[1]
    n_asg = t * TOP_K
    e_flat = idx.T.reshape(-1)
    onehot = (e_flat[:, None] == jnp.arange(n_exp, dtype=jnp.int32)[None, :]).astype(jnp.int32)
    csum = jnp.cumsum(onehot, axis=0)
    rank = jnp.sum(csum * onehot, axis=1) - 1
    counts = csum[-1]
    pcounts = (counts + tm - 1) // tm * tm
    pends = jnp.cumsum(pcounts)
    pstarts = pends - pcounts
    slot = (pstarts[e_flat] + rank).astype(jnp.int32)
    nblk = n_asg // tm + n_exp
    row_token = jnp.zeros((nblk * tm,), jnp.int32).at[slot].set(jnp.arange(n_asg, dtype=jnp.int32) // TOP_K)
    block_start = jnp.arange(nblk, dtype=jnp.int32) * tm
    block_e = jnp.minimum(jnp.sum(pends[None, :] <= block_start[:, None], axis=-1), n_exp - 1).astype(jnp.int32)
    n_used = (pends[-1] // tm).astype(jnp.int32).reshape(1)
    return slot, row_token, block_e, n_used


def kernel(x, w_in, q_norm_g, k_norm_g, conv_w, w_attn_proj, w_conv_proj, w_out, ln1_g, ln1_b,
           w_router, router_bias, w_gate, w_up, w_down, ln2_g, ln2_b):
    batch, seq, d = x.shape
    depth = w_in.shape[0]
    t = batch * seq
    attn_w = d // 2
    n_q = attn_w // HEAD_DIM
    n_kv = n_q // GQA_RATIO
    kv_w = n_kv * HEAD_DIM
    conv_wd = d // 2
    n_exp = w_router.shape[1]
    alpha = (2 * depth) ** 0.25
    q_end = attn_w
    k_end = q_end + kv_w
    v_end = k_end + kv_w
    cb_end = v_end + conv_wd
    cc_end = cb_end + conv_wd
    ch_end = cc_end + conv_wd
    ga_end = ch_end + d
    expert_tm = min(512, t)

    cos_t, sin_t = _rope_tables(seq)
    scale = HEAD_DIM ** -0.5 * math.log2(math.e)
    w_in_b = w_in.astype(BF16)
    wa_b = w_attn_proj.astype(BF16)
    wc_b = w_conv_proj.astype(BF16)
    wo_b = w_out.astype(BF16)
    w_router_t = w_router.T
    bias_col = router_bias.reshape(n_exp, 1).astype(F32)
    ln1_g3, ln1_b3 = ln1_g.reshape(depth, 1, d), ln1_b.reshape(depth, 1, d)
    ln2_g3, ln2_b3 = ln2_g.reshape(depth, 1, d), ln2_b.reshape(depth, 1, d)

    xf = x.reshape(t, d)
    xb = xf.astype(BF16)
    for l in range(depth):
        p = _matmul(xb, w_in_b, l, BF16, name="in_proj")
        gains = jnp.concatenate([jnp.tile(q_norm_g[l] * scale, n_q), jnp.tile(k_norm_g[l], n_kv)]).reshape(1, k_end)
        qk = _qk_prep(p, gains.astype(F32), cos_t, sin_t, seq, k_end)
        attn = _attention(qk, p, batch, seq, n_kv, q_end // HEAD_DIM, k_end // HEAD_DIM)
        conv = _short_conv(p, conv_w, l, seq, v_end, cb_end, cc_end, conv_wd)
        merged = _merge(attn, conv, wa_b, wc_b, p, l, ch_end, ga_end)
        h1 = _matmul(merged, wo_b, l, F32, res=xf, alpha=alpha, tn=512, name="out_proj")
        x1 = _layer_norm(h1, ln1_g3, ln1_b3, l)
        idx, gate = _router(x1, w_router_t, bias_col)
        slot, row_token, block_e, n_used = _dispatch(idx[:TOP_K], n_exp, expert_tm)
        y = _experts(block_e, n_used, row_token, x1, w_gate, w_up, w_down, l, expert_tm)
        xf, xb = _combine_ln(slot, x1, y, gate.T, ln2_g3, ln2_b3, l, alpha)
    return xf.reshape(batch, seq, d)
```

```python
import functools
import math

import jax
import jax.numpy as jnp
from jax import lax
from jax.experimental import pallas as pl
from jax.experimental.pallas import tpu as pltpu

HEAD_DIM = 128
GQA_RATIO = 4
CONV_K = 3
GRID_W = 64
ROPE_THETA = 10000.0
N_GROUPS = 4
TOP_K = 2
RMS_EPS = 1e-6
LN_EPS = 1e-5
LANES = 128
SUBLANES = 8
ONES_ROWS = 2 * SUBLANES
VMEM_LIMIT = 56 * 1024 * 1024

F32 = jnp.float32
BF16 = jnp.bfloat16


def _params(semantics):
    return pltpu.CompilerParams(dimension_semantics=semantics, vmem_limit_bytes=VMEM_LIMIT)


def _tile(dim, pref):
    t = min(dim, pref)
    while dim % t:
        t //= 2
    return t


def _mm_kernel(x_ref, w_ref, o_ref):
    o_ref[...] = jnp.dot(x_ref[...], w_ref[...], preferred_element_type=F32).astype(o_ref.dtype)


def _mm_res_kernel(x_ref, w_ref, r_ref, o_ref, *, alpha):
    acc = jnp.dot(x_ref[...], w_ref[...], preferred_element_type=F32)
    o_ref[...] = (alpha * r_ref[...] + acc).astype(o_ref.dtype)


def _matmul(x, w, layer, out_dtype, *, res=None, alpha=1.0, tm=1024, tn=1024, name="matmul"):
    m, k = x.shape
    n = w.shape[2]
    tm, tn = _tile(m, tm), _tile(n, tn)
    in_specs = [
        pl.BlockSpec((tm, k), lambda i, j: (i, 0)),
        pl.BlockSpec((None, k, tn), lambda i, j: (layer, 0, j)),
    ]
    args = [x, w]
    body = _mm_kernel
    if res is not None:
        in_specs.append(pl.BlockSpec((tm, tn), lambda i, j: (i, j)))
        args.append(res)
        body = functools.partial(_mm_res_kernel, alpha=alpha)
    return pl.pallas_call(
        body,
        grid=(m // tm, n // tn),
        in_specs=in_specs,
        out_specs=pl.BlockSpec((tm, tn), lambda i, j: (i, j)),
        out_shape=jax.ShapeDtypeStruct((m, n), out_dtype),
        compiler_params=_params(("parallel", "parallel")),
        name=name,
    )(*args)


def _swap_halves(y):
    lane = lax.broadcasted_iota(jnp.int32, y.shape, 1)
    fwd = pltpu.roll(y, LANES - 32, axis=1)
    bwd = pltpu.roll(y, 32, axis=1)
    return jnp.where((lane % 64) < 32, fwd, bwd)


def _qk_prep_kernel(p_ref, g_ref, c_ref, s_ref, o_ref, *, heads):
    cos = c_ref[...]
    sin = s_ref[...]
    for h in range(heads):
        sl = slice(h * HEAD_DIM, (h + 1) * HEAD_DIM)
        xh = p_ref[:, sl].astype(F32)
        ms = jnp.mean(xh * xh, axis=-1, keepdims=True)
        y = xh * lax.rsqrt(ms + RMS_EPS) * g_ref[:, sl]
        o_ref[:, sl] = (y * cos + _swap_halves(y) * sin).astype(o_ref.dtype)


def _qk_prep(p, gains, cos_t, sin_t, seq, qk_width, *, tm=512, heads=4):
    t = p.shape[0]
    tm = _tile(seq, tm)
    n_heads = qk_width // HEAD_DIM
    heads = _tile(n_heads, heads)
    wblk = heads * HEAD_DIM
    sblocks = seq // tm
    return pl.pallas_call(
        functools.partial(_qk_prep_kernel, heads=heads),
        grid=(t // tm, n_heads // heads),
        in_specs=[
            pl.BlockSpec((tm, wblk), lambda i, j: (i, j)),
            pl.BlockSpec((1, wblk), lambda i, j: (0, j)),
            pl.BlockSpec((tm, HEAD_DIM), lambda i, j: (i % sblocks, 0)),
            pl.BlockSpec((tm, HEAD_DIM), lambda i, j: (i % sblocks, 0)),
        ],
        out_specs=pl.BlockSpec((tm, wblk), lambda i, j: (i, j)),
        out_shape=jax.ShapeDtypeStruct((t, qk_width), BF16),
        compiler_params=_params(("parallel", "parallel")),
        name="qk_prep",
    )(p, gains, cos_t, sin_t)


def _flash_kernel(q_ref, k_ref, v_ref, o_ref, qt_ref, vt_ref, *stat_refs, tq, tk, cb, seq, unroll, ahead):
    nkv = seq // tk
    ncb = GQA_RATIO * tq // cb
    acc_refs, m_refs = stat_refs[:ncb], stat_refs[ncb:]

    @pl.when(pl.program_id(2) == 0)
    def _():
        for c in range(nkv):
            vt_ref[c, :HEAD_DIM, :] = v_ref[c * tk:(c + 1) * tk, :].T
            vt_ref[c, HEAD_DIM:, :] = jnp.ones((ONES_ROWS, tk), BF16)

    for g in range(GQA_RATIO):
        qt_ref[:, g * tq:(g + 1) * tq] = q_ref[:, g * HEAD_DIM:(g + 1) * HEAD_DIM].T
    for c in range(ncb):
        m_refs[c][...] = jnp.full(m_refs[c].shape, -jnp.inf, F32)
        acc_refs[c][...] = jnp.zeros(acc_refs[c].shape, F32)

    def body(j, carry):
        tiles = [(u, c) for u in range(unroll) for c in range(ncb)]
        ks, vts = [], []
        for u in range(unroll):
            off = pl.multiple_of((j * unroll + u) * tk, tk)
            ks.append(k_ref[pl.ds(off, tk), :])
            vts.append(vt_ref[j * unroll + u])

        def scores(t):
            u, c = tiles[t]
            return jnp.dot(ks[u], qt_ref[:, c * cb:(c + 1) * cb], preferred_element_type=F32)

        pending = [scores(t) for t in range(min(ahead, len(tiles)))]
        for t, (u, c) in enumerate(tiles):
            st = pending.pop(0)
            if t + ahead < len(tiles):
                pending.append(scores(t + ahead))
            m_old = m_refs[c][...]
            m_new = jnp.maximum(m_old, jnp.max(st, axis=0, keepdims=True))
            alpha = jnp.exp2(m_old - m_new)
            pt = jnp.exp2(st - m_new).astype(BF16)
            m_refs[c][...] = m_new
            acc_refs[c][...] = alpha * acc_refs[c][...] + jnp.dot(vts[u], pt, preferred_element_type=F32)
        return carry

    lax.fori_loop(0, nkv // unroll, body, 0)
    per_g = tq // cb
    for c in range(ncb):
        g, r = divmod(c, per_g)
        acc = acc_refs[c][...]
        out_t = acc[:HEAD_DIM, :] / acc[HEAD_DIM:HEAD_DIM + 1, :]
        o_ref[r * cb:(r + 1) * cb, g * HEAD_DIM:(g + 1) * HEAD_DIM] = out_t.T.astype(o_ref.dtype)


def _attention(qk, p, batch, seq, n_kv, k_col0, v_col0, *, tq=512, tk=256, cb=256, unroll=8, ahead=5):
    tq, tk = _tile(seq, tq), _tile(seq, tk)
    rows = GQA_RATIO * tq
    cb = _tile(rows, cb)
    unroll = _tile(seq // tk, unroll)
    qk3 = qk.reshape(batch, seq, qk.shape[1])
    p3 = p.reshape(batch, seq, p.shape[1])
    gw = GQA_RATIO * HEAD_DIM
    out = pl.pallas_call(
        functools.partial(_flash_kernel, tq=tq, tk=tk, cb=cb, seq=seq, unroll=unroll, ahead=ahead),
        grid=(batch, n_kv, seq // tq),
        in_specs=[
            pl.BlockSpec((None, tq, gw), lambda b, h, i: (b, i, h)),
            pl.BlockSpec((None, seq, HEAD_DIM), lambda b, h, i: (b, 0, k_col0 + h)),
            pl.BlockSpec((None, seq, HEAD_DIM), lambda b, h, i: (b, 0, v_col0 + h)),
        ],
        out_specs=pl.BlockSpec((None, tq, gw), lambda b, h, i: (b, i, h)),
        out_shape=jax.ShapeDtypeStruct((batch, seq, n_kv * gw), BF16),
        scratch_shapes=[
            pltpu.VMEM((HEAD_DIM, rows), BF16),
            pltpu.VMEM((seq // tk, HEAD_DIM + ONES_ROWS, tk), BF16),
        ] + [pltpu.VMEM((HEAD_DIM + ONES_ROWS, cb), F32)] * (rows // cb)
        + [pltpu.VMEM((1, cb), F32)] * (rows // cb),
        compiler_params=_params(("parallel", "parallel", "arbitrary")),
        name="flash_gqa",
    )(qk3, qk3, p3)
    return out.reshape(batch * seq, n_kv * gw)


def _conv_kernel(cb_ref, cc_ref, ch_ref, cbp_ref, chp_ref, cbn_ref, chn_ref, w_ref, o_ref, *, tm, sblocks):
    i = pl.program_id(0)
    first = (i % sblocks) == 0
    last = (i % sblocks) == sblocks - 1
    bx = cb_ref[...].astype(F32) * ch_ref[...].astype(F32)
    prev_row = cbp_ref[SUBLANES - 1:SUBLANES, :].astype(F32) * chp_ref[SUBLANES - 1:SUBLANES, :].astype(F32)
    next_row = cbn_ref[0:1, :].astype(F32) * chn_ref[0:1, :].astype(F32)
    prev_row = jnp.where(first, 0.0, prev_row)
    next_row = jnp.where(last, 0.0, next_row)
    row = lax.broadcasted_iota(jnp.int32, bx.shape, 0)
    prev = jnp.where(row == 0, prev_row, pltpu.roll(bx, 1, axis=0))
    nxt = jnp.where(row == tm - 1, next_row, pltpu.roll(bx, tm - 1, axis=0))
    w = w_ref[...]
    y = w[0:1, :] * prev + w[1:2, :] * bx + w[2:3, :] * nxt
    o_ref[...] = (cc_ref[...].astype(F32) * y).astype(o_ref.dtype)


def _short_conv(p, conv_w, layer, seq, cb0, cc0, ch0, width, *, tm=512, tc=1024):
    t = p.shape[0]
    tm, tc = _tile(seq, tm), math.gcd(cb0, cc0, ch0, width, tc)
    sblocks = seq // tm
    rpb = tm // SUBLANES
    nrb = t // SUBLANES

    def main(c0):
        return pl.BlockSpec((tm, tc), lambda i, j: (i, c0 // tc + j))

    def halo_prev(c0):
        return pl.BlockSpec((SUBLANES, tc), lambda i, j: (jnp.maximum(i * rpb - 1, 0), c0 // tc + j))

    def halo_next(c0):
        return pl.BlockSpec((SUBLANES, tc), lambda i, j: (jnp.minimum((i + 1) * rpb, nrb - 1), c0 // tc + j))

    return pl.pallas_call(
        functools.partial(_conv_kernel, tm=tm, sblocks=sblocks),
        grid=(t // tm, width // tc),
        in_specs=[main(cb0), main(cc0), main(ch0), halo_prev(cb0), halo_prev(ch0),
                  halo_next(cb0), halo_next(ch0),
                  pl.BlockSpec((None, CONV_K, tc), lambda i, j: (layer, 0, j))],
        out_specs=pl.BlockSpec((tm, tc), lambda i, j: (i, j)),
        out_shape=jax.ShapeDtypeStruct((t, width), BF16),
        compiler_params=_params(("parallel", "parallel")),
        name="short_conv",
    )(p, p, p, p, p, p, p, conv_w)


def _merge_kernel(a_ref, c_ref, wa_ref, wc_ref, ga_ref, gc_ref, o_ref):
    ya = jnp.dot(a_ref[...], wa_ref[...], preferred_element_type=F32)
    yc = jnp.dot(c_ref[...], wc_ref[...], preferred_element_type=F32)
    o_ref[...] = (jax.nn.sigmoid(ga_ref[...].astype(F32)) * ya
                  + jax.nn.sigmoid(gc_ref[...].astype(F32)) * yc).astype(o_ref.dtype)


def _merge(attn, conv, wa, wc, p, layer, ga0, gc0, *, tm=512, tn=1024):
    t, ka = attn.shape
    kc = conv.shape[1]
    d = wa.shape[2]
    tm, tn = _tile(t, tm), math.gcd(ga0, gc0, d, tn)
    return pl.pallas_call(
        _merge_kernel,
        grid=(t // tm, d // tn),
        in_specs=[
            pl.BlockSpec((tm, ka), lambda i, j: (i, 0)),
            pl.BlockSpec((tm, kc), lambda i, j: (i, 0)),
            pl.BlockSpec((None, ka, tn), lambda i, j: (layer, 0, j)),
            pl.BlockSpec((None, kc, tn), lambda i, j: (layer, 0, j)),
            pl.BlockSpec((tm, tn), lambda i, j: (i, ga0 // tn + j)),
            pl.BlockSpec((tm, tn), lambda i, j: (i, gc0 // tn + j)),
        ],
        out_specs=pl.BlockSpec((tm, tn), lambda i, j: (i, j)),
        out_shape=jax.ShapeDtypeStruct((t, d), BF16),
        compiler_params=_params(("parallel", "parallel")),
        name="gated_merge",
    )(attn, conv, wa, wc, p, p)


def _ln_rows(h, g, b):
    mu = jnp.mean(h, axis=-1, keepdims=True)
    hc = h - mu
    var = jnp.mean(hc * hc, axis=-1, keepdims=True)
    return hc * lax.rsqrt(var + LN_EPS) * g + b


def _ln_kernel(h_ref, g_ref, b_ref, o_ref):
    o_ref[...] = _ln_rows(h_ref[...], g_ref[...], b_ref[...])


def _layer_norm(h, g, b, layer, *, tm=256):
    t, d = h.shape
    tm = _tile(t, tm)
    row = pl.BlockSpec((tm, d), lambda i: (i, 0))
    vec = pl.BlockSpec((None, 1, d), lambda i: (layer, 0, 0))
    return pl.pallas_call(
        _ln_kernel,
        grid=(t // tm,),
        in_specs=[row, vec, vec],
        out_specs=row,
        out_shape=jax.ShapeDtypeStruct((t, d), F32),
        compiler_params=_params(("parallel",)),
        name="layer_norm",
    )(h, g, b)


def _router_kernel(x_ref, w_ref, b_ref, idx_ref, gate_ref, *, n_exp):
    epg = n_exp // N_GROUPS
    logits = lax.dot_general(w_ref[...], x_ref[...], (((1,), (1,)), ((), ())),
                             precision=lax.Precision.HIGHEST, preferred_element_type=F32)
    mx = jnp.max(logits, axis=0, keepdims=True)
    ex = jnp.exp(logits - mx)
    scores = ex / jnp.sum(ex, axis=0, keepdims=True)
    sel = scores + b_ref[...]
    rows_sel = [sel[e:e + 1, :] for e in range(n_exp)]
    rows_sc = [scores[e:e + 1, :] for e in range(n_exp)]
    best = None
    for g in range(N_GROUPS):
        mem = rows_sel[g * epg:(g + 1) * epg]
        gs = None
        for a in range(epg):
            for c in range(a + 1, epg):
                pair = mem[a] + mem[c]
                gs = pair if gs is None else jnp.maximum(gs, pair)
        if best is None:
            best, grp = gs, jnp.zeros(gs.shape, jnp.int32)
        else:
            upd = gs > best
            best = jnp.where(upd, gs, best)
            grp = jnp.where(upd, g, grp)
    cand_sel, cand_sc = [], []
    for j in range(epg):
        cs, cc = rows_sel[j], rows_sc[j]
        for g in range(1, N_GROUPS):
            cs = jnp.where(grp == g, rows_sel[g * epg + j], cs)
            cc = jnp.where(grp == g, rows_sc[g * epg + j], cc)
        cand_sel.append(cs)
        cand_sc.append(cc)

    def first_argmax(vals, skip):
        bv = bi = bs = None
        for j in range(epg):
            v = vals[j] if skip is None else jnp.where(skip == j, -jnp.inf, vals[j])
            if bv is None:
                bv, bi, bs = v, jnp.zeros(v.shape, jnp.int32), cand_sc[0]
            else:
                upd = v > bv
                bv = jnp.where(upd, v, bv)
                bi = jnp.where(upd, j, bi)
                bs = jnp.where(upd, cand_sc[j], bs)
        return bi, bs

    i1, s1 = first_argmax(cand_sel, None)
    i2, s2 = first_argmax(cand_sel, i1)
    tot = s1 + s2
    zi = jnp.zeros((SUBLANES - TOP_K,) + i1.shape[1:], jnp.int32)
    zf = jnp.zeros((SUBLANES - TOP_K,) + i1.shape[1:], F32)
    idx_ref[...] = jnp.concatenate([grp * epg + i1, grp * epg + i2, zi], axis=0)
    gate_ref[...] = jnp.concatenate([s1 / tot, s2 / tot, zf], axis=0)


def _router(x, w_router_t, bias_col, *, tm=512):
    t, d = x.shape
    n_exp = w_router_t.shape[0]
    tm = _tile(t, tm)
    out = pl.BlockSpec((SUBLANES, tm), lambda i: (0, i))
    return pl.pallas_call(
        functools.partial(_router_kernel, n_exp=n_exp),
        grid=(t // tm,),
        in_specs=[
            pl.BlockSpec((tm, d), lambda i: (i, 0)),
            pl.BlockSpec((n_exp, d), lambda i: (0, 0)),
            pl.BlockSpec((n_exp, 1), lambda i: (0, 0)),
        ],
        out_specs=[out, out],
        out_shape=[jax.ShapeDtypeStruct((SUBLANES, t), jnp.int32),
                   jax.ShapeDtypeStruct((SUBLANES, t), F32)],
        compiler_params=_params(("parallel",)),
        name="router",
    )(x, w_router_t, bias_col)


def _gather_kernel(nu_ref, rt_ref, x_hbm, o_ref, xg_ref, sem, *, tm):
    i = pl.program_id(0)
    n_used = nu_ref[0]

    def start_gather(blk, s):
        base = blk * tm

        def issue(r, c):
            pltpu.make_async_copy(x_hbm.at[pl.ds(rt_ref[base + r], 1), :], xg_ref.at[s, pl.ds(r, 1), :],
                                  sem.at[s]).start()
            return c

        lax.fori_loop(0, tm, issue, 0, unroll=8)

    @pl.when(jnp.logical_and(i == 0, n_used > 0))
    def _():
        start_gather(0, 0)

    @pl.when(i + 1 < n_used)
    def _():
        start_gather(i + 1, (i + 1) % 2)

    @pl.when(i < n_used)
    def _():
        s = i % 2
        pltpu.make_async_copy(x_hbm.at[pl.ds(0, tm), :], xg_ref.at[s], sem.at[s]).wait()
        o_ref[...] = xg_ref[s].astype(BF16)

    @pl.when(i >= n_used)
    def _():
        o_ref[...] = jnp.zeros_like(o_ref)


def _gather_rows(n_used, row_token, x, tm):
    d = x.shape[1]
    r = row_token.shape[0]
    grid_spec = pltpu.PrefetchScalarGridSpec(
        num_scalar_prefetch=2,
        grid=(r // tm,),
        in_specs=[pl.BlockSpec(memory_space=pl.ANY)],
        out_specs=pl.BlockSpec((tm, d), lambda i, nu, rt: (i, 0)),
        scratch_shapes=[pltpu.VMEM((2, tm, d), F32), pltpu.SemaphoreType.DMA((2,))],
    )
    return pl.pallas_call(
        functools.partial(_gather_kernel, tm=tm),
        grid_spec=grid_spec,
        out_shape=jax.ShapeDtypeStruct((r, d), BF16),
        compiler_params=_params(("arbitrary",)),
        name="dispatch_gather",
    )(n_used, row_token, x)


def _new_expert(be_ref, i):
    return jnp.logical_or(i == 0, be_ref[i] != be_ref[jnp.maximum(i - 1, 0)])


def _expert_up_kernel(be_ref, nu_ref, x_ref, wg_ref, wu_ref, o_ref, wgb_ref, wub_ref):
    i = pl.program_id(1)

    @pl.when(_new_expert(be_ref, i))
    def _():
        wgb_ref[...] = wg_ref[...].astype(BF16)
        wub_ref[...] = wu_ref[...].astype(BF16)

    @pl.when(i < nu_ref[0])
    def _():
        x = x_ref[...]
        hg = jnp.dot(x, wgb_ref[...], preferred_element_type=F32)
        hu = jnp.dot(x, wub_ref[...], preferred_element_type=F32)
        o_ref[...] = (hg * jax.nn.sigmoid(hg) * hu).astype(o_ref.dtype)

    @pl.when(i >= nu_ref[0])
    def _():
        o_ref[...] = jnp.zeros_like(o_ref)


def _expert_down_kernel(be_ref, nu_ref, h_ref, wd_ref, o_ref, wdb_ref):
    i = pl.program_id(1)

    @pl.when(_new_expert(be_ref, i))
    def _():
        wdb_ref[...] = wd_ref[...].astype(BF16)

    @pl.when(i < nu_ref[0])
    def _():
        o_ref[...] = jnp.dot(h_ref[...], wdb_ref[...], preferred_element_type=F32)

    @pl.when(i >= nu_ref[0])
    def _():
        o_ref[...] = jnp.zeros_like(o_ref)


def _experts(block_e, n_used, xs, wg, wu, wd, layer, tm, *, tf=512, tn=2048):
    r, d = xs.shape
    ff = wg.shape[3]
    tf, tn = _tile(ff, tf), _tile(d, tn)
    nblk = r // tm
    hid = pl.pallas_call(
        _expert_up_kernel,
        grid_spec=pltpu.PrefetchScalarGridSpec(
            num_scalar_prefetch=2,
            grid=(ff // tf, nblk),
            in_specs=[
                pl.BlockSpec((tm, d), lambda f, i, be, nu: (i, 0)),
                pl.BlockSpec((None, None, d, tf), lambda f, i, be, nu: (layer, be[i], 0, f)),
                pl.BlockSpec((None, None, d, tf), lambda f, i, be, nu: (layer, be[i], 0, f)),
            ],
            out_specs=pl.BlockSpec((tm, tf), lambda f, i, be, nu: (i, f)),
            scratch_shapes=[pltpu.VMEM((d, tf), BF16), pltpu.VMEM((d, tf), BF16)],
        ),
        out_shape=jax.ShapeDtypeStruct((r, ff), BF16),
        compiler_params=_params(("arbitrary", "arbitrary")),
        name="expert_up",
    )(block_e, n_used, xs, wg, wu)
    return pl.pallas_call(
        _expert_down_kernel,
        grid_spec=pltpu.PrefetchScalarGridSpec(
            num_scalar_prefetch=2,
            grid=(d // tn, nblk),
            in_specs=[
                pl.BlockSpec((tm, ff), lambda n, i, be, nu: (i, 0)),
                pl.BlockSpec((None, None, ff, tn), lambda n, i, be, nu: (layer, be[i], 0, n)),
            ],
            out_specs=pl.BlockSpec((tm, tn), lambda n, i, be, nu: (i, n)),
            scratch_shapes=[pltpu.VMEM((ff, tn), BF16)],
        ),
        out_shape=jax.ShapeDtypeStruct((r, d), F32),
        compiler_params=_params(("arbitrary", "arbitrary")),
        name="expert_down",
    )(block_e, n_used, hid, wd)


def _combine_kernel(slot_ref, x_ref, y_hbm, gt_ref, g_ref, b_ref, o_ref, ob_ref, ybuf, sem, *, alpha, tm):
    i = pl.program_id(0)
    n = pl.num_programs(0)

    def row_copy(row, s, k, r):
        return pltpu.make_async_copy(y_hbm.at[pl.ds(row, 1), :], ybuf.at[s, k, pl.ds(r, 1), :], sem.at[s])

    def start_gather(step, s):
        base = step * (tm * TOP_K)

        def issue(r, c):
            for k in range(TOP_K):
                row_copy(slot_ref[base + r * TOP_K + k], s, k, r).start()
            return c

        lax.fori_loop(0, tm, issue, 0, unroll=4)

    def wait_gather(s):
        for k in range(TOP_K):
            pltpu.make_async_copy(y_hbm.at[pl.ds(0, tm), :], ybuf.at[s, k], sem.at[s]).wait()

    @pl.when(i == 0)
    def _():
        start_gather(0, 0)

    @pl.when(i + 1 < n)
    def _():
        start_gather(i + 1, (i + 1) % 2)

    s = i % 2
    wait_gather(s)
    gt = gt_ref[...]
    h = alpha * x_ref[...] + gt[:, 0:1] * ybuf[s, 0] + gt[:, 1:2] * ybuf[s, 1]
    y = _ln_rows(h, g_ref[...], b_ref[...])
    o_ref[...] = y
    ob_ref[...] = y.astype(BF16)


def _combine_ln(slot, x, y, gates_t, g, b, layer, alpha, *, tm=256):
    t, d = x.shape
    tm = _tile(t, tm)
    row = pl.BlockSpec((tm, d), lambda i, sl: (i, 0))
    vec = pl.BlockSpec((None, 1, d), lambda i, sl: (layer, 0, 0))
    grid_spec = pltpu.PrefetchScalarGridSpec(
        num_scalar_prefetch=1,
        grid=(t // tm,),
        in_specs=[row, pl.BlockSpec(memory_space=pl.ANY),
                  pl.BlockSpec((tm, SUBLANES), lambda i, sl: (i, 0)), vec, vec],
        out_specs=[row, row],
        scratch_shapes=[pltpu.VMEM((2, TOP_K, tm, d), F32), pltpu.SemaphoreType.DMA((2,))],
    )
    return pl.pallas_call(
        functools.partial(_combine_kernel, alpha=alpha, tm=tm),
        grid_spec=grid_spec,
        out_shape=[jax.ShapeDtypeStruct((t, d), F32), jax.ShapeDtypeStruct((t, d), BF16)],
        compiler_params=_params(("arbitrary",)),
        name="combine_ln",
    )(slot, x, y, gates_t, g, b)


def _rope_tables(seq):
    axis_dim = HEAD_DIM // 2
    rows = seq // GRID_W
    inv = ROPE_THETA ** (-jnp.arange(0, axis_dim, 2, dtype=F32) / axis_dim)
    ang_r = jnp.repeat(jnp.arange(rows, dtype=F32), GRID_W)[:, None] * inv[None, :]
    ang_c = jnp.tile(jnp.arange(GRID_W, dtype=F32), rows)[:, None] * inv[None, :]
    cr, sr, cc, sc = jnp.cos(ang_r), jnp.sin(ang_r), jnp.cos(ang_c), jnp.sin(ang_c)
    cos_t = jnp.concatenate([cr, cr, cc, cc], axis=1)
    sin_t = jnp.concatenate([-sr, sr, -sc, sc], axis=1)
    return cos_t, sin_t


def _dispatch(idx, n_exp, tm):
    t = idx.shape[1]
    n_asg = t * TOP_K
    e_flat = idx.T.reshape(-1)
    onehot = (e_flat[:, None] == jnp.arange(n_exp, dtype=jnp.int32)[None, :]).astype(jnp.int32)
    csum = jnp.cumsum(onehot, axis=0)
    rank = jnp.sum(csum * onehot, axis=1) - 1
    counts = csum[-1]
    pcounts = (counts + tm - 1) // tm * tm
    pends = jnp.cumsum(pcounts)
    pstarts = pends - pcounts
    slot = (pstarts[e_flat] + rank).astype(jnp.int32)
    nblk = n_asg // tm + n_exp
    row_token = jnp.zeros((nblk * tm,), jnp.int32).at[slot].set(jnp.arange(n_asg, dtype=jnp.int32) // TOP_K)
    block_start = jnp.arange(nblk, dtype=jnp.int32) * tm
    block_e = jnp.minimum(jnp.sum(pends[None, :] <= block_start[:, None], axis=-1), n_exp - 1).astype(jnp.int32)
    n_used = (pends[-1] // tm).astype(jnp.int32).reshape(1)
    return slot, row_token, block_e, n_used


def kernel(x, w_in, q_norm_g, k_norm_g, conv_w, w_attn_proj, w_conv_proj, w_out, ln1_g, ln1_b,
           w_router, router_bias, w_gate, w_up, w_down, ln2_g, ln2_b):
    batch, seq, d = x.shape
    depth = w_in.shape[0]
    t = batch * seq
    attn_w = d // 2
    n_q = attn_w // HEAD_DIM
    n_kv = n_q // GQA_RATIO
    kv_w = n_kv * HEAD_DIM
    conv_wd = d // 2
    n_exp = w_router.shape[1]
    alpha = (2 * depth) ** 0.25
    q_end = attn_w
    k_end = q_end + kv_w
    v_end = k_end + kv_w
    cb_end = v_end + conv_wd
    cc_end = cb_end + conv_wd
    ch_end = cc_end + conv_wd
    ga_end = ch_end + d
    expert_tm = min(512, t)

    cos_t, sin_t = _rope_tables(seq)
    scale = HEAD_DIM ** -0.5 * math.log2(math.e)
    w_in_b = w_in.astype(BF16)
    wa_b = w_attn_proj.astype(BF16)
    wc_b = w_conv_proj.astype(BF16)
    wo_b = w_out.astype(BF16)
    w_router_t = w_router.T
    bias_col = router_bias.reshape(n_exp, 1).astype(F32)
    ln1_g3, ln1_b3 = ln1_g.reshape(depth, 1, d), ln1_b.reshape(depth, 1, d)
    ln2_g3, ln2_b3 = ln2_g.reshape(depth, 1, d), ln2_b.reshape(depth, 1, d)

    xf = x.reshape(t, d)
    xb = xf.astype(BF16)
    for l in range(depth):
        p = _matmul(xb, w_in_b, l, BF16, name="in_proj")
        gains = jnp.concatenate([jnp.tile(q_norm_g[l] * scale, n_q), jnp.tile(k_norm_g[l], n_kv)]).reshape(1, k_end)
        qk = _qk_prep(p, gains.astype(F32), cos_t, sin_t, seq, k_end)
        attn = _attention(qk, p, batch, seq, n_kv, q_end // HEAD_DIM, k_end // HEAD_DIM)
        conv = _short_conv(p, conv_w, l, seq, v_end, cb_end, cc_end, conv_wd)
        merged = _merge(attn, conv, wa_b, wc_b, p, l, ch_end, ga_end)
        h1 = _matmul(merged, wo_b, l, F32, res=xf, alpha=alpha, tn=512, name="out_proj")
        x1 = _layer_norm(h1, ln1_g3, ln1_b3, l)
        idx, gate = _router(x1, w_router_t, bias_col)
        slot, row_token, block_e, n_used = _dispatch(idx[:TOP_K], n_exp, expert_tm)
        xs = _gather_rows(n_used, row_token, x1, expert_tm)
        y = _experts(block_e, n_used, xs, w_gate, w_up, w_down, l, expert_tm)
        xf, xb = _combine_ln(slot, x1, y, gate.T, ln2_g3, ln2_b3, l, alpha)
    return xf.reshape(batch, seq, d)
```

```python
import functools
import math

import jax
import jax.numpy as jnp
from jax import lax
from jax.experimental import pallas as pl
from jax.experimental.pallas import tpu as pltpu

HEAD_DIM = 128
GQA_RATIO = 4
CONV_K = 3
GRID_W = 64
ROPE_THETA = 10000.0
N_GROUPS = 4
TOP_K = 2
RMS_EPS = 1e-6
LN_EPS = 1e-5
LANES = 128
SUBLANES = 8
ONES_ROWS = 2 * SUBLANES
VMEM_LIMIT = 56 * 1024 * 1024

F32 = jnp.float32
BF16 = jnp.bfloat16


def _params(semantics):
    return pltpu.CompilerParams(dimension_semantics=semantics, vmem_limit_bytes=VMEM_LIMIT)


def _tile(dim, pref):
    t = min(dim, pref)
    while dim % t:
        t //= 2
    return t


def _pack_bf16_pair(lo, hi):
    lo_b = lax.bitcast_convert_type(lo.astype(BF16).astype(F32), jnp.uint32) >> 16
    hi_b = lax.bitcast_convert_type(hi.astype(BF16).astype(F32), jnp.uint32) & jnp.uint32(0xFFFF0000)
    return hi_b | lo_b


def _unpack_bf16_pair(w):
    lo = lax.bitcast_convert_type(w << 16, F32)
    hi = lax.bitcast_convert_type(w & jnp.uint32(0xFFFF0000), F32)
    return lo, hi


def _mm_kernel(x_ref, w_ref, o_ref, wb_ref):
    @pl.when(pl.program_id(1) == 0)
    def _():
        wb_ref[...] = w_ref[...].astype(BF16)

    o_ref[...] = jnp.dot(x_ref[...], wb_ref[...], preferred_element_type=F32).astype(o_ref.dtype)


def _mm_res_kernel(x_ref, w_ref, r_ref, o_ref, wb_ref, *, alpha):
    @pl.when(pl.program_id(1) == 0)
    def _():
        wb_ref[...] = w_ref[...].astype(BF16)

    acc = jnp.dot(x_ref[...], wb_ref[...], preferred_element_type=F32)
    o_ref[...] = (alpha * r_ref[...] + acc).astype(o_ref.dtype)


def _matmul(x, w, layer, out_dtype, *, res=None, alpha=1.0, tm=512, tn=1024, name="matmul"):
    m, k = x.shape
    n = w.shape[2]
    tm, tn = _tile(m, tm), _tile(n, tn)
    in_specs = [
        pl.BlockSpec((tm, k), lambda j, i: (i, 0)),
        pl.BlockSpec((None, k, tn), lambda j, i: (layer, 0, j)),
    ]
    args = [x, w]
    body = _mm_kernel
    if res is not None:
        in_specs.append(pl.BlockSpec((tm, tn), lambda j, i: (i, j)))
        args.append(res)
        body = functools.partial(_mm_res_kernel, alpha=alpha)
    return pl.pallas_call(
        body,
        grid=(n // tn, m // tm),
        in_specs=in_specs,
        out_specs=pl.BlockSpec((tm, tn), lambda j, i: (i, j)),
        out_shape=jax.ShapeDtypeStruct((m, n), out_dtype),
        scratch_shapes=[pltpu.VMEM((k, tn), BF16)],
        compiler_params=_params(("parallel", "arbitrary")),
        name=name,
    )(*args)


def _swap_halves(y):
    lane = lax.broadcasted_iota(jnp.int32, y.shape, 1)
    fwd = pltpu.roll(y, LANES - 32, axis=1)
    bwd = pltpu.roll(y, 32, axis=1)
    return jnp.where((lane % 64) < 32, fwd, bwd)


def _qk_prep_kernel(p_ref, g_ref, c_ref, s_ref, o_ref, *, heads):
    cos = c_ref[...]
    sin = s_ref[...]
    for h in range(heads):
        sl = slice(h * HEAD_DIM, (h + 1) * HEAD_DIM)
        xh = p_ref[:, sl].astype(F32)
        ms = jnp.mean(xh * xh, axis=-1, keepdims=True)
        y = xh * lax.rsqrt(ms + RMS_EPS) * g_ref[:, sl]
        o_ref[:, sl] = (y * cos + _swap_halves(y) * sin).astype(o_ref.dtype)


def _qk_prep(p, gains, cos_t, sin_t, seq, qk_width, *, tm=512, heads=4):
    t = p.shape[0]
    tm = _tile(seq, tm)
    n_heads = qk_width // HEAD_DIM
    heads = _tile(n_heads, heads)
    wblk = heads * HEAD_DIM
    sblocks = seq // tm
    return pl.pallas_call(
        functools.partial(_qk_prep_kernel, heads=heads),
        grid=(t // tm, n_heads // heads),
        in_specs=[
            pl.BlockSpec((tm, wblk), lambda i, j: (i, j)),
            pl.BlockSpec((1, wblk), lambda i, j: (0, j)),
            pl.BlockSpec((tm, HEAD_DIM), lambda i, j: (i % sblocks, 0)),
            pl.BlockSpec((tm, HEAD_DIM), lambda i, j: (i % sblocks, 0)),
        ],
        out_specs=pl.BlockSpec((tm, wblk), lambda i, j: (i, j)),
        out_shape=jax.ShapeDtypeStruct((t, qk_width), BF16),
        compiler_params=_params(("parallel", "parallel")),
        name="qk_prep",
    )(p, gains, cos_t, sin_t)


def _flash_kernel(q_ref, k_ref, v_ref, o_ref, qt_ref, vt_ref, *stat_refs, tq, tk, cb, seq, unroll, ahead):
    nkv = seq // tk
    ncb = GQA_RATIO * tq // cb
    acc_refs, m_refs = stat_refs[:ncb], stat_refs[ncb:]

    @pl.when(pl.program_id(2) == 0)
    def _():
        for c in range(nkv):
            vt_ref[c, :HEAD_DIM, :] = v_ref[c * tk:(c + 1) * tk, :].T
            vt_ref[c, HEAD_DIM:, :] = jnp.ones((ONES_ROWS, tk), BF16)

    for g in range(GQA_RATIO):
        qt_ref[:, g * tq:(g + 1) * tq] = q_ref[:, g * HEAD_DIM:(g + 1) * HEAD_DIM].T
    for c in range(ncb):
        m_refs[c][...] = jnp.full(m_refs[c].shape, -jnp.inf, F32)
        acc_refs[c][...] = jnp.zeros(acc_refs[c].shape, F32)

    def body(j, carry):
        tiles = [(u, c) for u in range(unroll) for c in range(ncb)]
        ks, vts = [], []
        for u in range(unroll):
            off = pl.multiple_of((j * unroll + u) * tk, tk)
            ks.append(k_ref[pl.ds(off, tk), :])
            vts.append(vt_ref[j * unroll + u])

        def scores(t):
            u, c = tiles[t]
            return jnp.dot(ks[u], qt_ref[:, c * cb:(c + 1) * cb], preferred_element_type=F32)

        pending = [scores(t) for t in range(min(ahead, len(tiles)))]
        for t, (u, c) in enumerate(tiles):
            st = pending.pop(0)
            if t + ahead < len(tiles):
                pending.append(scores(t + ahead))
            m_old = m_refs[c][...]
            m_new = jnp.maximum(m_old, jnp.max(st, axis=0, keepdims=True))
            alpha = jnp.exp2(m_old - m_new)
            pt = jnp.exp2(st - m_new).astype(BF16)
            m_refs[c][...] = m_new
            acc_refs[c][...] = alpha * acc_refs[c][...] + jnp.dot(vts[u], pt, preferred_element_type=F32)
        return carry

    lax.fori_loop(0, nkv // unroll, body, 0)
    per_g = tq // cb
    for c in range(ncb):
        g, r = divmod(c, per_g)
        acc = acc_refs[c][...]
        out_t = acc[:HEAD_DIM, :] / acc[HEAD_DIM:HEAD_DIM + 1, :]
        o_ref[r * cb:(r + 1) * cb, g * HEAD_DIM:(g + 1) * HEAD_DIM] = out_t.T.astype(o_ref.dtype)


def _attention(qk, p, batch, seq, n_kv, k_col0, v_col0, *, tq=512, tk=256, cb=256, unroll=8, ahead=5):
    tq, tk = _tile(seq, tq), _tile(seq, tk)
    rows = GQA_RATIO * tq
    cb = _tile(rows, cb)
    unroll = _tile(seq // tk, unroll)
    qk3 = qk.reshape(batch, seq, qk.shape[1])
    p3 = p.reshape(batch, seq, p.shape[1])
    gw = GQA_RATIO * HEAD_DIM
    out = pl.pallas_call(
        functools.partial(_flash_kernel, tq=tq, tk=tk, cb=cb, seq=seq, unroll=unroll, ahead=ahead),
        grid=(batch, n_kv, seq // tq),
        in_specs=[
            pl.BlockSpec((None, tq, gw), lambda b, h, i: (b, i, h)),
            pl.BlockSpec((None, seq, HEAD_DIM), lambda b, h, i: (b, 0, k_col0 + h)),
            pl.BlockSpec((None, seq, HEAD_DIM), lambda b, h, i: (b, 0, v_col0 + h)),
        ],
        out_specs=pl.BlockSpec((None, tq, gw), lambda b, h, i: (b, i, h)),
        out_shape=jax.ShapeDtypeStruct((batch, seq, n_kv * gw), BF16),
        scratch_shapes=[
            pltpu.VMEM((HEAD_DIM, rows), BF16),
            pltpu.VMEM((seq // tk, HEAD_DIM + ONES_ROWS, tk), BF16),
        ] + [pltpu.VMEM((HEAD_DIM + ONES_ROWS, cb), F32)] * (rows // cb)
        + [pltpu.VMEM((1, cb), F32)] * (rows // cb),
        compiler_params=_params(("parallel", "parallel", "arbitrary")),
        name="flash_gqa",
    )(qk3, qk3, p3)
    return out.reshape(batch * seq, n_kv * gw)


def _conv_kernel(cb_ref, cc_ref, ch_ref, cbp_ref, chp_ref, cbn_ref, chn_ref, w_ref, o_ref, *, tm, sblocks):
    i = pl.program_id(0)
    first = (i % sblocks) == 0
    last = (i % sblocks) == sblocks - 1
    bx = cb_ref[...].astype(F32) * ch_ref[...].astype(F32)
    prev_row = cbp_ref[SUBLANES - 1:SUBLANES, :].astype(F32) * chp_ref[SUBLANES - 1:SUBLANES, :].astype(F32)
    next_row = cbn_ref[0:1, :].astype(F32) * chn_ref[0:1, :].astype(F32)
    prev_row = jnp.where(first, 0.0, prev_row)
    next_row = jnp.where(last, 0.0, next_row)
    row = lax.broadcasted_iota(jnp.int32, bx.shape, 0)
    prev = jnp.where(row == 0, prev_row, pltpu.roll(bx, 1, axis=0))
    nxt = jnp.where(row == tm - 1, next_row, pltpu.roll(bx, tm - 1, axis=0))
    w = w_ref[...]
    y = w[0:1, :] * prev + w[1:2, :] * bx + w[2:3, :] * nxt
    o_ref[...] = (cc_ref[...].astype(F32) * y).astype(o_ref.dtype)


def _short_conv(p, conv_w, layer, seq, cb0, cc0, ch0, width, *, tm=512, tc=1024):
    t = p.shape[0]
    tm, tc = _tile(seq, tm), math.gcd(cb0, cc0, ch0, width, tc)
    sblocks = seq // tm
    rpb = tm // SUBLANES
    nrb = t // SUBLANES

    def main(c0):
        return pl.BlockSpec((tm, tc), lambda i, j: (i, c0 // tc + j))

    def halo_prev(c0):
        return pl.BlockSpec((SUBLANES, tc), lambda i, j: (jnp.maximum(i * rpb - 1, 0), c0 // tc + j))

    def halo_next(c0):
        return pl.BlockSpec((SUBLANES, tc), lambda i, j: (jnp.minimum((i + 1) * rpb, nrb - 1), c0 // tc + j))

    return pl.pallas_call(
        functools.partial(_conv_kernel, tm=tm, sblocks=sblocks),
        grid=(t // tm, width // tc),
        in_specs=[main(cb0), main(cc0), main(ch0), halo_prev(cb0), halo_prev(ch0),
                  halo_next(cb0), halo_next(ch0),
                  pl.BlockSpec((None, CONV_K, tc), lambda i, j: (layer, 0, j))],
        out_specs=pl.BlockSpec((tm, tc), lambda i, j: (i, j)),
        out_shape=jax.ShapeDtypeStruct((t, width), BF16),
        compiler_params=_params(("parallel", "parallel")),
        name="short_conv",
    )(p, p, p, p, p, p, p, conv_w)


def _merge_kernel(a_ref, c_ref, wa_ref, wc_ref, ga_ref, gc_ref, o_ref, wab_ref, wcb_ref):
    @pl.when(pl.program_id(1) == 0)
    def _():
        wab_ref[...] = wa_ref[...].astype(BF16)
        wcb_ref[...] = wc_ref[...].astype(BF16)

    ya = jnp.dot(a_ref[...], wab_ref[...], preferred_element_type=F32)
    yc = jnp.dot(c_ref[...], wcb_ref[...], preferred_element_type=F32)
    o_ref[...] = (jax.nn.sigmoid(ga_ref[...].astype(F32)) * ya
                  + jax.nn.sigmoid(gc_ref[...].astype(F32)) * yc).astype(o_ref.dtype)


def _merge(attn, conv, wa, wc, p, layer, ga0, gc0, *, tm=1024, tn=512):
    t, ka = attn.shape
    kc = conv.shape[1]
    d = wa.shape[2]
    tm, tn = _tile(t, tm), math.gcd(ga0, gc0, d, tn)
    return pl.pallas_call(
        _merge_kernel,
        grid=(d // tn, t // tm),
        in_specs=[
            pl.BlockSpec((tm, ka), lambda j, i: (i, 0)),
            pl.BlockSpec((tm, kc), lambda j, i: (i, 0)),
            pl.BlockSpec((None, ka, tn), lambda j, i: (layer, 0, j)),
            pl.BlockSpec((None, kc, tn), lambda j, i: (layer, 0, j)),
            pl.BlockSpec((tm, tn), lambda j, i: (i, ga0 // tn + j)),
            pl.BlockSpec((tm, tn), lambda j, i: (i, gc0 // tn + j)),
        ],
        out_specs=pl.BlockSpec((tm, tn), lambda j, i: (i, j)),
        out_shape=jax.ShapeDtypeStruct((t, d), BF16),
        scratch_shapes=[pltpu.VMEM((ka, tn), BF16), pltpu.VMEM((kc, tn), BF16)],
        compiler_params=_params(("parallel", "arbitrary")),
        name="gated_merge",
    )(attn, conv, wa, wc, p, p)


def _ln_rows(h, g, b):
    mu = jnp.mean(h, axis=-1, keepdims=True)
    hc = h - mu
    var = jnp.mean(hc * hc, axis=-1, keepdims=True)
    return hc * lax.rsqrt(var + LN_EPS) * g + b


def _ln_kernel(h_ref, g_ref, b_ref, o_ref, op_ref):
    y = _ln_rows(h_ref[...], g_ref[...], b_ref[...])
    half = y.shape[1] // 2
    o_ref[...] = y
    op_ref[...] = _pack_bf16_pair(y[:, :half], y[:, half:])


def _layer_norm(h, g, b, layer, *, tm=256):
    t, d = h.shape
    tm = _tile(t, tm)
    row = pl.BlockSpec((tm, d), lambda i: (i, 0))
    vec = pl.BlockSpec((None, 1, d), lambda i: (layer, 0, 0))
    return pl.pallas_call(
        _ln_kernel,
        grid=(t // tm,),
        in_specs=[row, vec, vec],
        out_specs=[row, pl.BlockSpec((tm, d // 2), lambda i: (i, 0))],
        out_shape=[jax.ShapeDtypeStruct((t, d), F32), jax.ShapeDtypeStruct((t, d // 2), jnp.uint32)],
        compiler_params=_params(("parallel",)),
        name="layer_norm",
    )(h, g, b)


def _router_kernel(x_ref, w_ref, b_ref, idx_ref, gate_ref, *, n_exp):
    epg = n_exp // N_GROUPS
    logits = lax.dot_general(w_ref[...], x_ref[...], (((1,), (1,)), ((), ())),
                             precision=lax.Precision.HIGHEST, preferred_element_type=F32)
    mx = jnp.max(logits, axis=0, keepdims=True)
    ex = jnp.exp(logits - mx)
    scores = ex / jnp.sum(ex, axis=0, keepdims=True)
    sel = scores + b_ref[...]
    rows_sel = [sel[e:e + 1, :] for e in range(n_exp)]
    rows_sc = [scores[e:e + 1, :] for e in range(n_exp)]
    best = None
    for g in range(N_GROUPS):
        mem = rows_sel[g * epg:(g + 1) * epg]
        gs = None
        for a in range(epg):
            for c in range(a + 1, epg):
                pair = mem[a] + mem[c]
                gs = pair if gs is None else jnp.maximum(gs, pair)
        if best is None:
            best, grp = gs, jnp.zeros(gs.shape, jnp.int32)
        else:
            upd = gs > best
            best = jnp.where(upd, gs, best)
            grp = jnp.where(upd, g, grp)
    cand_sel, cand_sc = [], []
    for j in range(epg):
        cs, cc = rows_sel[j], rows_sc[j]
        for g in range(1, N_GROUPS):
            cs = jnp.where(grp == g, rows_sel[g * epg + j], cs)
            cc = jnp.where(grp == g, rows_sc[g * epg + j], cc)
        cand_sel.append(cs)
        cand_sc.append(cc)

    def first_argmax(vals, skip):
        bv = bi = bs = None
        for j in range(epg):
            v = vals[j] if skip is None else jnp.where(skip == j, -jnp.inf, vals[j])
            if bv is None:
                bv, bi, bs = v, jnp.zeros(v.shape, jnp.int32), cand_sc[0]
            else:
                upd = v > bv
                bv = jnp.where(upd, v, bv)
                bi = jnp.where(upd, j, bi)
                bs = jnp.where(upd, cand_sc[j], bs)
        return bi, bs

    i1, s1 = first_argmax(cand_sel, None)
    i2, s2 = first_argmax(cand_sel, i1)
    tot = s1 + s2
    zi = jnp.zeros((SUBLANES - TOP_K,) + i1.shape[1:], jnp.int32)
    zf = jnp.zeros((SUBLANES - TOP_K,) + i1.shape[1:], F32)
    idx_ref[...] = jnp.concatenate([grp * epg + i1, grp * epg + i2, zi], axis=0)
    gate_ref[...] = jnp.concatenate([s1 / tot, s2 / tot, zf], axis=0)


def _router(x, w_router_t, bias_col, *, tm=512):
    t, d = x.shape
    n_exp = w_router_t.shape[0]
    tm = _tile(t, tm)
    out = pl.BlockSpec((SUBLANES, tm), lambda i: (0, i))
    return pl.pallas_call(
        functools.partial(_router_kernel, n_exp=n_exp),
        grid=(t // tm,),
        in_specs=[
            pl.BlockSpec((tm, d), lambda i: (i, 0)),
            pl.BlockSpec((n_exp, d), lambda i: (0, 0)),
            pl.BlockSpec((n_exp, 1), lambda i: (0, 0)),
        ],
        out_specs=[out, out],
        out_shape=[jax.ShapeDtypeStruct((SUBLANES, t), jnp.int32),
                   jax.ShapeDtypeStruct((SUBLANES, t), F32)],
        compiler_params=_params(("parallel",)),
        name="router",
    )(x, w_router_t, bias_col)


def _gather_kernel(nu_ref, rt_ref, x_hbm, o_ref, xg_ref, sem, *, tm):
    i = pl.program_id(0)
    n_used = nu_ref[0]

    def start_gather(blk, s):
        base = blk * tm

        def issue(r, c):
            pltpu.make_async_copy(x_hbm.at[pl.ds(rt_ref[base + r], 1), :], xg_ref.at[s, pl.ds(r, 1), :],
                                  sem.at[s]).start()
            return c

        lax.fori_loop(0, tm, issue, 0, unroll=8)

    @pl.when(jnp.logical_and(i == 0, n_used > 0))
    def _():
        start_gather(0, 0)

    @pl.when(i + 1 < n_used)
    def _():
        start_gather(i + 1, (i + 1) % 2)

    @pl.when(i < n_used)
    def _():
        s = i % 2
        pltpu.make_async_copy(x_hbm.at[pl.ds(0, tm), :], xg_ref.at[s], sem.at[s]).wait()
        lo, hi = _unpack_bf16_pair(xg_ref[s])
        half = lo.shape[1]
        o_ref[:, :half] = lo.astype(BF16)
        o_ref[:, half:] = hi.astype(BF16)

    @pl.when(i >= n_used)
    def _():
        o_ref[...] = jnp.zeros_like(o_ref)


def _gather_rows(n_used, row_token, xp, tm):
    d = 2 * xp.shape[1]
    r = row_token.shape[0]
    grid_spec = pltpu.PrefetchScalarGridSpec(
        num_scalar_prefetch=2,
        grid=(r // tm,),
        in_specs=[pl.BlockSpec(memory_space=pl.ANY)],
        out_specs=pl.BlockSpec((tm, d), lambda i, nu, rt: (i, 0)),
        scratch_shapes=[pltpu.VMEM((2, tm, d // 2), jnp.uint32), pltpu.SemaphoreType.DMA((2,))],
    )
    return pl.pallas_call(
        functools.partial(_gather_kernel, tm=tm),
        grid_spec=grid_spec,
        out_shape=jax.ShapeDtypeStruct((r, d), BF16),
        compiler_params=_params(("arbitrary",)),
        name="dispatch_gather",
    )(n_used, row_token, xp)


def _new_expert(be_ref, i):
    return jnp.logical_or(i == 0, be_ref[i] != be_ref[jnp.maximum(i - 1, 0)])


def _expert_up_kernel(be_ref, nu_ref, x_ref, wg_ref, wu_ref, o_ref, wgb_ref, wub_ref):
    i = pl.program_id(1)

    @pl.when(_new_expert(be_ref, i))
    def _():
        wgb_ref[...] = wg_ref[...].astype(BF16)
        wub_ref[...] = wu_ref[...].astype(BF16)

    @pl.when(i < nu_ref[0])
    def _():
        x = x_ref[...]
        hg = jnp.dot(x, wgb_ref[...], preferred_element_type=F32)
        hu = jnp.dot(x, wub_ref[...], preferred_element_type=F32)
        o_ref[...] = (hg * jax.nn.sigmoid(hg) * hu).astype(o_ref.dtype)

    @pl.when(i >= nu_ref[0])
    def _():
        o_ref[...] = jnp.zeros_like(o_ref)


def _expert_down_kernel(be_ref, nu_ref, h_ref, wlo_ref, whi_ref, o_ref, wlob_ref, whib_ref):
    i = pl.program_id(1)

    @pl.when(_new_expert(be_ref, i))
    def _():
        wlob_ref[...] = wlo_ref[...].astype(BF16)
        whib_ref[...] = whi_ref[...].astype(BF16)

    @pl.when(i < nu_ref[0])
    def _():
        h = h_ref[...]
        o_ref[...] = _pack_bf16_pair(jnp.dot(h, wlob_ref[...], preferred_element_type=F32),
                                     jnp.dot(h, whib_ref[...], preferred_element_type=F32))

    @pl.when(i >= nu_ref[0])
    def _():
        o_ref[...] = jnp.zeros_like(o_ref)


def _experts(block_e, n_used, xs, wg, wu, wd, layer, tm, *, tf=512, tn=1024):
    r, d = xs.shape
    ff = wg.shape[3]
    tf = _tile(ff, tf)
    nblk = r // tm
    hid = pl.pallas_call(
        _expert_up_kernel,
        grid_spec=pltpu.PrefetchScalarGridSpec(
            num_scalar_prefetch=2,
            grid=(ff // tf, nblk),
            in_specs=[
                pl.BlockSpec((tm, d), lambda f, i, be, nu: (i, 0)),
                pl.BlockSpec((None, None, d, tf), lambda f, i, be, nu: (layer, be[i], 0, f)),
                pl.BlockSpec((None, None, d, tf), lambda f, i, be, nu: (layer, be[i], 0, f)),
            ],
            out_specs=pl.BlockSpec((tm, tf), lambda f, i, be, nu: (i, f)),
            scratch_shapes=[pltpu.VMEM((d, tf), BF16), pltpu.VMEM((d, tf), BF16)],
        ),
        out_shape=jax.ShapeDtypeStruct((r, ff), BF16),
        compiler_params=_params(("arbitrary", "arbitrary")),
        name="expert_up",
    )(block_e, n_used, xs, wg, wu)
    half = d // 2
    tn = _tile(half, tn)
    hi0 = half // tn
    return pl.pallas_call(
        _expert_down_kernel,
        grid_spec=pltpu.PrefetchScalarGridSpec(
            num_scalar_prefetch=2,
            grid=(half // tn, nblk),
            in_specs=[
                pl.BlockSpec((tm, ff), lambda n, i, be, nu: (i, 0)),
                pl.BlockSpec((None, None, ff, tn), lambda n, i, be, nu: (layer, be[i], 0, n)),
                pl.BlockSpec((None, None, ff, tn), lambda n, i, be, nu: (layer, be[i], 0, hi0 + n)),
            ],
            out_specs=pl.BlockSpec((tm, tn), lambda n, i, be, nu: (i, n)),
            scratch_shapes=[pltpu.VMEM((ff, tn), BF16), pltpu.VMEM((ff, tn), BF16)],
        ),
        out_shape=jax.ShapeDtypeStruct((r, half), jnp.uint32),
        compiler_params=_params(("arbitrary", "arbitrary")),
        name="expert_down",
    )(block_e, n_used, hid, wd, wd)


def _combine_kernel(slot_ref, x_ref, y_hbm, gt_ref, g_ref, b_ref, o_ref, ob_ref, ybuf, sem, *, alpha, tm):
    i = pl.program_id(0)
    n = pl.num_programs(0)

    def row_copy(row, s, k, r):
        return pltpu.make_async_copy(y_hbm.at[pl.ds(row, 1), :], ybuf.at[s, k, pl.ds(r, 1), :], sem.at[s])

    def start_gather(step, s):
        base = step * (tm * TOP_K)

        def issue(r, c):
            for k in range(TOP_K):
                row_copy(slot_ref[base + r * TOP_K + k], s, k, r).start()
            return c

        lax.fori_loop(0, tm, issue, 0, unroll=4)

    def wait_gather(s):
        for k in range(TOP_K):
            pltpu.make_async_copy(y_hbm.at[pl.ds(0, tm), :], ybuf.at[s, k], sem.at[s]).wait()

    @pl.when(i == 0)
    def _():
        start_gather(0, 0)

    @pl.when(i + 1 < n)
    def _():
        start_gather(i + 1, (i + 1) % 2)

    s = i % 2
    wait_gather(s)
    gt = gt_ref[...]
    a_lo, a_hi = _unpack_bf16_pair(ybuf[s, 0])
    b_lo, b_hi = _unpack_bf16_pair(ybuf[s, 1])
    g0, g1 = gt[:, 0:1], gt[:, 1:2]
    ffn = jnp.concatenate([g0 * a_lo + g1 * b_lo, g0 * a_hi + g1 * b_hi], axis=1)
    y = _ln_rows(alpha * x_ref[...] + ffn, g_ref[...], b_ref[...])
    o_ref[...] = y
    ob_ref[...] = y.astype(BF16)


def _combine_ln(slot, x, y, gates_t, g, b, layer, alpha, *, tm=256):
    t, d = x.shape
    tm = _tile(t, tm)
    row = pl.BlockSpec((tm, d), lambda i, sl: (i, 0))
    vec = pl.BlockSpec((None, 1, d), lambda i, sl: (layer, 0, 0))
    grid_spec = pltpu.PrefetchScalarGridSpec(
        num_scalar_prefetch=1,
        grid=(t // tm,),
        in_specs=[row, pl.BlockSpec(memory_space=pl.ANY),
                  pl.BlockSpec((tm, SUBLANES), lambda i, sl: (i, 0)), vec, vec],
        out_specs=[row, row],
        scratch_shapes=[pltpu.VMEM((2, TOP_K, tm, d // 2), jnp.uint32), pltpu.SemaphoreType.DMA((2,))],
    )
    return pl.pallas_call(
        functools.partial(_combine_kernel, alpha=alpha, tm=tm),
        grid_spec=grid_spec,
        out_shape=[jax.ShapeDtypeStruct((t, d), F32), jax.ShapeDtypeStruct((t, d), BF16)],
        compiler_params=_params(("arbitrary",)),
        name="combine_ln",
    )(slot, x, y, gates_t, g, b)


def _rope_tables(seq):
    axis_dim = HEAD_DIM // 2
    rows = seq // GRID_W
    inv = ROPE_THETA ** (-jnp.arange(0, axis_dim, 2, dtype=F32) / axis_dim)
    ang_r = jnp.repeat(jnp.arange(rows, dtype=F32), GRID_W)[:, None] * inv[None, :]
    ang_c = jnp.tile(jnp.arange(GRID_W, dtype=F32), rows)[:, None] * inv[None, :]
    cr, sr, cc, sc = jnp.cos(ang_r), jnp.sin(ang_r), jnp.cos(ang_c), jnp.sin(ang_c)
    cos_t = jnp.concatenate([cr, cr, cc, cc], axis=1)
    sin_t = jnp.concatenate([-sr, sr, -sc, sc], axis=1)
    return cos_t, sin_t


def _dispatch(idx, n_exp, tm):
    t = idx.shape[1]
    n_asg = t * TOP_K
    e_flat = idx.T.reshape(-1)
    onehot = (e_flat[:, None] == jnp.arange(n_exp, dtype=jnp.int32)[None, :]).astype(jnp.int32)
    csum = jnp.cumsum(onehot, axis=0)
    rank = jnp.sum(csum * onehot, axis=1) - 1
    counts = csum[-1]
    pcounts = (counts + tm - 1) // tm * tm
    pends = jnp.cumsum(pcounts)
    pstarts = pends - pcounts
    slot = (pstarts[e_flat] + rank).astype(jnp.int32)
    nblk = n_asg // tm + n_exp
    row_token = jnp.zeros((nblk * tm,), jnp.int32).at[slot].set(jnp.arange(n_asg, dtype=jnp.int32) // TOP_K)
    block_start = jnp.arange(nblk, dtype=jnp.int32) * tm
    block_e = jnp.minimum(jnp.sum(pends[None, :] <= block_start[:, None], axis=-1), n_exp - 1).astype(jnp.int32)
    n_used = (pends[-1] // tm).astype(jnp.int32).reshape(1)
    return slot, row_token, block_e, n_used


def kernel(x, w_in, q_norm_g, k_norm_g, conv_w, w_attn_proj, w_conv_proj, w_out, ln1_g, ln1_b,
           w_router, router_bias, w_gate, w_up, w_down, ln2_g, ln2_b):
    batch, seq, d = x.shape
    depth = w_in.shape[0]
    t = batch * seq
    attn_w = d // 2
    n_q = attn_w // HEAD_DIM
    n_kv = n_q // GQA_RATIO
    kv_w = n_kv * HEAD_DIM
    conv_wd = d // 2
    n_exp = w_router.shape[1]
    alpha = (2 * depth) ** 0.25
    q_end = attn_w
    k_end = q_end + kv_w
    v_end = k_end + kv_w
    cb_end = v_end + conv_wd
    cc_end = cb_end + conv_wd
    ch_end = cc_end + conv_wd
    ga_end = ch_end + d
    expert_tm = min(512, t)

    cos_t, sin_t = _rope_tables(seq)
    scale = HEAD_DIM ** -0.5 * math.log2(math.e)
    w_router_t = w_router.T
    bias_col = router_bias.reshape(n_exp, 1).astype(F32)
    ln1_g3, ln1_b3 = ln1_g.reshape(depth, 1, d), ln1_b.reshape(depth, 1, d)
    ln2_g3, ln2_b3 = ln2_g.reshape(depth, 1, d), ln2_b.reshape(depth, 1, d)

    xf = x.reshape(t, d)
    xb = xf.astype(BF16)
    for l in range(depth):
        p = _matmul(xb, w_in, l, BF16, name="in_proj")
        gains = jnp.concatenate([jnp.tile(q_norm_g[l] * scale, n_q), jnp.tile(k_norm_g[l], n_kv)]).reshape(1, k_end)
        qk = _qk_prep(p, gains.astype(F32), cos_t, sin_t, seq, k_end)
        attn = _attention(qk, p, batch, seq, n_kv, q_end // HEAD_DIM, k_end // HEAD_DIM)
        conv = _short_conv(p, conv_w, l, seq, v_end, cb_end, cc_end, conv_wd)
        merged = _merge(attn, conv, w_attn_proj, w_conv_proj, p, l, ch_end, ga_end)
        h1 = _matmul(merged, w_out, l, F32, res=xf, alpha=alpha, tm=1024, tn=512, name="out_proj")
        x1, x1p = _layer_norm(h1, ln1_g3, ln1_b3, l)
        idx, gate = _router(x1, w_router_t, bias_col)
        slot, row_token, block_e, n_used = _dispatch(idx[:TOP_K], n_exp, expert_tm)
        xs = _gather_rows(n_used, row_token, x1p, expert_tm)
        y = _experts(block_e, n_used, xs, w_gate, w_up, w_down, l, expert_tm)
        xf, xb = _combine_ln(slot, x1, y, gate.T, ln2_g3, ln2_b3, l, alpha)
    return xf.reshape(batch, seq, d)
```

```python
import functools
import math

import jax
import jax.numpy as jnp
from jax import lax
from jax.experimental import pallas as pl
from jax.experimental.pallas import tpu as pltpu

HEAD_DIM = 128
GQA_RATIO = 4
CONV_K = 3
GRID_W = 64
ROPE_THETA = 10000.0
N_GROUPS = 4
TOP_K = 2
RMS_EPS = 1e-6
LN_EPS = 1e-5
LANES = 128
SUBLANES = 8
ONES_ROWS = 2 * SUBLANES
VMEM_LIMIT = 56 * 1024 * 1024

F32 = jnp.float32
BF16 = jnp.bfloat16


def _params(semantics):
    return pltpu.CompilerParams(dimension_semantics=semantics, vmem_limit_bytes=VMEM_LIMIT)


def _tile(dim, pref):
    t = min(dim, pref)
    while dim % t:
        t //= 2
    return t


def _pack_bf16_pair(lo, hi):
    lo_b = lax.bitcast_convert_type(lo.astype(BF16).astype(F32), jnp.uint32) >> 16
    hi_b = lax.bitcast_convert_type(hi.astype(BF16).astype(F32), jnp.uint32) & jnp.uint32(0xFFFF0000)
    return hi_b | lo_b


def _unpack_bf16_pair(w):
    lo = lax.bitcast_convert_type(w << 16, F32)
    hi = lax.bitcast_convert_type(w & jnp.uint32(0xFFFF0000), F32)
    return lo, hi


def _mm_kernel(x_ref, w_ref, o_ref, wb_ref):
    @pl.when(pl.program_id(1) == 0)
    def _():
        wb_ref[...] = w_ref[...].astype(BF16)

    o_ref[...] = jnp.dot(x_ref[...], wb_ref[...], preferred_element_type=F32).astype(o_ref.dtype)


def _mm_res_kernel(x_ref, w_ref, r_ref, o_ref, wb_ref, *, alpha):
    @pl.when(pl.program_id(1) == 0)
    def _():
        wb_ref[...] = w_ref[...].astype(BF16)

    acc = jnp.dot(x_ref[...], wb_ref[...], preferred_element_type=F32)
    o_ref[...] = (alpha * r_ref[...] + acc).astype(o_ref.dtype)


def _matmul(x, w, layer, out_dtype, *, res=None, alpha=1.0, tm=512, tn=1024, name="matmul"):
    m, k = x.shape
    n = w.shape[2]
    tm, tn = _tile(m, tm), _tile(n, tn)
    in_specs = [
        pl.BlockSpec((tm, k), lambda j, i: (i, 0)),
        pl.BlockSpec((None, k, tn), lambda j, i: (layer, 0, j)),
    ]
    args = [x, w]
    body = _mm_kernel
    if res is not None:
        in_specs.append(pl.BlockSpec((tm, tn), lambda j, i: (i, j)))
        args.append(res)
        body = functools.partial(_mm_res_kernel, alpha=alpha)
    return pl.pallas_call(
        body,
        grid=(n // tn, m // tm),
        in_specs=in_specs,
        out_specs=pl.BlockSpec((tm, tn), lambda j, i: (i, j)),
        out_shape=jax.ShapeDtypeStruct((m, n), out_dtype),
        scratch_shapes=[pltpu.VMEM((k, tn), BF16)],
        compiler_params=_params(("parallel", "arbitrary")),
        name=name,
    )(*args)


def _qk_prep_kernel(p_ref, g_ref, c_ref, s_ref, ones_ref, perm_ref, o_ref, *, heads):
    cos = c_ref[...]
    sin = s_ref[...]
    ones = ones_ref[...]
    perm = perm_ref[...]
    for h in range(heads):
        sl = slice(h * HEAD_DIM, (h + 1) * HEAD_DIM)
        xh = p_ref[:, sl].astype(F32)
        ssq = jnp.dot((xh * xh).astype(BF16), ones, preferred_element_type=F32)
        y = xh * lax.rsqrt(ssq * (1.0 / HEAD_DIM) + RMS_EPS) * g_ref[:, sl]
        swapped = jnp.dot(y.astype(BF16), perm, preferred_element_type=F32)
        o_ref[:, sl] = (y * cos + swapped * sin).astype(o_ref.dtype)


def _qk_prep(p, gains, cos_t, sin_t, seq, qk_width, *, tm=512, heads=4):
    t = p.shape[0]
    tm = _tile(seq, tm)
    n_heads = qk_width // HEAD_DIM
    heads = _tile(n_heads, heads)
    wblk = heads * HEAD_DIM
    sblocks = seq // tm
    ones = jnp.ones((HEAD_DIM, HEAD_DIM), BF16)
    lane = jnp.arange(HEAD_DIM)
    partner = jnp.where((lane % 64) < 32, lane + 32, lane - 32)
    perm = (lane[:, None] == partner[None, :]).astype(BF16)
    return pl.pallas_call(
        functools.partial(_qk_prep_kernel, heads=heads),
        grid=(t // tm, n_heads // heads),
        in_specs=[
            pl.BlockSpec((tm, wblk), lambda i, j: (i, j)),
            pl.BlockSpec((1, wblk), lambda i, j: (0, j)),
            pl.BlockSpec((tm, HEAD_DIM), lambda i, j: (i % sblocks, 0)),
            pl.BlockSpec((tm, HEAD_DIM), lambda i, j: (i % sblocks, 0)),
            pl.BlockSpec((HEAD_DIM, HEAD_DIM), lambda i, j: (0, 0)),
            pl.BlockSpec((HEAD_DIM, HEAD_DIM), lambda i, j: (0, 0)),
        ],
        out_specs=pl.BlockSpec((tm, wblk), lambda i, j: (i, j)),
        out_shape=jax.ShapeDtypeStruct((t, qk_width), BF16),
        compiler_params=_params(("parallel", "parallel")),
        name="qk_prep",
    )(p, gains, cos_t, sin_t, ones, perm)


def _flash_kernel(q_ref, k_ref, v_ref, o_ref, qt_ref, vt_ref, *stat_refs, tq, tk, cb, seq, unroll, ahead):
    nkv = seq // tk
    ncb = GQA_RATIO * tq // cb
    acc_refs, m_refs = stat_refs[:ncb], stat_refs[ncb:]

    @pl.when(pl.program_id(2) == 0)
    def _():
        for c in range(nkv):
            vt_ref[c, :HEAD_DIM, :] = v_ref[c * tk:(c + 1) * tk, :].T
            vt_ref[c, HEAD_DIM:, :] = jnp.ones((ONES_ROWS, tk), BF16)

    for g in range(GQA_RATIO):
        qt_ref[:, g * tq:(g + 1) * tq] = q_ref[:, g * HEAD_DIM:(g + 1) * HEAD_DIM].T
    for c in range(ncb):
        m_refs[c][...] = jnp.full(m_refs[c].shape, -jnp.inf, F32)
        acc_refs[c][...] = jnp.zeros(acc_refs[c].shape, F32)

    def body(j, carry):
        tiles = [(u, c) for u in range(unroll) for c in range(ncb)]
        ks, vts = [], []
        for u in range(unroll):
            off = pl.multiple_of((j * unroll + u) * tk, tk)
            ks.append(k_ref[pl.ds(off, tk), :])
            vts.append(vt_ref[j * unroll + u])

        def scores(t):
            u, c = tiles[t]
            return jnp.dot(ks[u], qt_ref[:, c * cb:(c + 1) * cb], preferred_element_type=F32)

        pending = [scores(t) for t in range(min(ahead, len(tiles)))]
        for t, (u, c) in enumerate(tiles):
            st = pending.pop(0)
            if t + ahead < len(tiles):
                pending.append(scores(t + ahead))
            m_old = m_refs[c][...]
            m_new = jnp.maximum(m_old, jnp.max(st, axis=0, keepdims=True))
            alpha = jnp.exp2(m_old - m_new)
            pt = jnp.exp2(st - m_new).astype(BF16)
            m_refs[c][...] = m_new
            acc_refs[c][...] = alpha * acc_refs[c][...] + jnp.dot(vts[u], pt, preferred_element_type=F32)
        return carry

    lax.fori_loop(0, nkv // unroll, body, 0)
    per_g = tq // cb
    for c in range(ncb):
        g, r = divmod(c, per_g)
        acc = acc_refs[c][...]
        out_t = acc[:HEAD_DIM, :] / acc[HEAD_DIM:HEAD_DIM + 1, :]
        o_ref[r * cb:(r + 1) * cb, g * HEAD_DIM:(g + 1) * HEAD_DIM] = out_t.T.astype(o_ref.dtype)


def _attention(qk, p, batch, seq, n_kv, k_col0, v_col0, *, tq=512, tk=256, cb=256, unroll=8, ahead=5):
    tq, tk = _tile(seq, tq), _tile(seq, tk)
    rows = GQA_RATIO * tq
    cb = _tile(rows, cb)
    unroll = _tile(seq // tk, unroll)
    qk3 = qk.reshape(batch, seq, qk.shape[1])
    p3 = p.reshape(batch, seq, p.shape[1])
    gw = GQA_RATIO * HEAD_DIM
    out = pl.pallas_call(
        functools.partial(_flash_kernel, tq=tq, tk=tk, cb=cb, seq=seq, unroll=unroll, ahead=ahead),
        grid=(batch, n_kv, seq // tq),
        in_specs=[
            pl.BlockSpec((None, tq, gw), lambda b, h, i: (b, i, h)),
            pl.BlockSpec((None, seq, HEAD_DIM), lambda b, h, i: (b, 0, k_col0 + h)),
            pl.BlockSpec((None, seq, HEAD_DIM), lambda b, h, i: (b, 0, v_col0 + h)),
        ],
        out_specs=pl.BlockSpec((None, tq, gw), lambda b, h, i: (b, i, h)),
        out_shape=jax.ShapeDtypeStruct((batch, seq, n_kv * gw), BF16),
        scratch_shapes=[
            pltpu.VMEM((HEAD_DIM, rows), BF16),
            pltpu.VMEM((seq // tk, HEAD_DIM + ONES_ROWS, tk), BF16),
        ] + [pltpu.VMEM((HEAD_DIM + ONES_ROWS, cb), F32)] * (rows // cb)
        + [pltpu.VMEM((1, cb), F32)] * (rows // cb),
        compiler_params=_params(("parallel", "parallel", "arbitrary")),
        name="flash_gqa",
    )(qk3, qk3, p3)
    return out.reshape(batch * seq, n_kv * gw)


def _conv_kernel(cb_ref, cc_ref, ch_ref, cbp_ref, chp_ref, cbn_ref, chn_ref, w_ref, o_ref, *, tm, sblocks):
    i = pl.program_id(0)
    first = (i % sblocks) == 0
    last = (i % sblocks) == sblocks - 1
    bx = cb_ref[...].astype(F32) * ch_ref[...].astype(F32)
    prev_row = cbp_ref[SUBLANES - 1:SUBLANES, :].astype(F32) * chp_ref[SUBLANES - 1:SUBLANES, :].astype(F32)
    next_row = cbn_ref[0:1, :].astype(F32) * chn_ref[0:1, :].astype(F32)
    prev_row = jnp.where(first, 0.0, prev_row)
    next_row = jnp.where(last, 0.0, next_row)
    row = lax.broadcasted_iota(jnp.int32, bx.shape, 0)
    prev = jnp.where(row == 0, prev_row, pltpu.roll(bx, 1, axis=0))
    nxt = jnp.where(row == tm - 1, next_row, pltpu.roll(bx, tm - 1, axis=0))
    w = w_ref[...]
    y = w[0:1, :] * prev + w[1:2, :] * bx + w[2:3, :] * nxt
    o_ref[...] = (cc_ref[...].astype(F32) * y).astype(o_ref.dtype)


def _short_conv(p, conv_w, layer, seq, cb0, cc0, ch0, width, *, tm=512, tc=1024):
    t = p.shape[0]
    tm, tc = _tile(seq, tm), math.gcd(cb0, cc0, ch0, width, tc)
    sblocks = seq // tm
    rpb = tm // SUBLANES
    nrb = t // SUBLANES

    def main(c0):
        return pl.BlockSpec((tm, tc), lambda i, j: (i, c0 // tc + j))

    def halo_prev(c0):
        return pl.BlockSpec((SUBLANES, tc), lambda i, j: (jnp.maximum(i * rpb - 1, 0), c0 // tc + j))

    def halo_next(c0):
        return pl.BlockSpec((SUBLANES, tc), lambda i, j: (jnp.minimum((i + 1) * rpb, nrb - 1), c0 // tc + j))

    return pl.pallas_call(
        functools.partial(_conv_kernel, tm=tm, sblocks=sblocks),
        grid=(t // tm, width // tc),
        in_specs=[main(cb0), main(cc0), main(ch0), halo_prev(cb0), halo_prev(ch0),
                  halo_next(cb0), halo_next(ch0),
                  pl.BlockSpec((None, CONV_K, tc), lambda i, j: (layer, 0, j))],
        out_specs=pl.BlockSpec((tm, tc), lambda i, j: (i, j)),
        out_shape=jax.ShapeDtypeStruct((t, width), BF16),
        compiler_params=_params(("parallel", "parallel")),
        name="short_conv",
    )(p, p, p, p, p, p, p, conv_w)


def _merge_kernel(a_ref, c_ref, wa_ref, wc_ref, ga_ref, gc_ref, o_ref, wab_ref, wcb_ref):
    @pl.when(pl.program_id(1) == 0)
    def _():
        wab_ref[...] = wa_ref[...].astype(BF16)
        wcb_ref[...] = wc_ref[...].astype(BF16)

    ya = jnp.dot(a_ref[...], wab_ref[...], preferred_element_type=F32)
    yc = jnp.dot(c_ref[...], wcb_ref[...], preferred_element_type=F32)
    o_ref[...] = (jax.nn.sigmoid(ga_ref[...].astype(F32)) * ya
                  + jax.nn.sigmoid(gc_ref[...].astype(F32)) * yc).astype(o_ref.dtype)


def _merge(attn, conv, wa, wc, p, layer, ga0, gc0, *, tm=1024, tn=512):
    t, ka = attn.shape
    kc = conv.shape[1]
    d = wa.shape[2]
    tm, tn = _tile(t, tm), math.gcd(ga0, gc0, d, tn)
    return pl.pallas_call(
        _merge_kernel,
        grid=(d // tn, t // tm),
        in_specs=[
            pl.BlockSpec((tm, ka), lambda j, i: (i, 0)),
            pl.BlockSpec((tm, kc), lambda j, i: (i, 0)),
            pl.BlockSpec((None, ka, tn), lambda j, i: (layer, 0, j)),
            pl.BlockSpec((None, kc, tn), lambda j, i: (layer, 0, j)),
            pl.BlockSpec((tm, tn), lambda j, i: (i, ga0 // tn + j)),
            pl.BlockSpec((tm, tn), lambda j, i: (i, gc0 // tn + j)),
        ],
        out_specs=pl.BlockSpec((tm, tn), lambda j, i: (i, j)),
        out_shape=jax.ShapeDtypeStruct((t, d), BF16),
        scratch_shapes=[pltpu.VMEM((ka, tn), BF16), pltpu.VMEM((kc, tn), BF16)],
        compiler_params=_params(("parallel", "arbitrary")),
        name="gated_merge",
    )(attn, conv, wa, wc, p, p)


def _ln_rows(h, g, b):
    mu = jnp.mean(h, axis=-1, keepdims=True)
    hc = h - mu
    var = jnp.mean(hc * hc, axis=-1, keepdims=True)
    return hc * lax.rsqrt(var + LN_EPS) * g + b


def _route(logits, bias, n_exp):
    epg = n_exp // N_GROUPS
    mx = jnp.max(logits, axis=0, keepdims=True)
    ex = jnp.exp(logits - mx)
    scores = ex / jnp.sum(ex, axis=0, keepdims=True)
    sel = scores + bias
    rows_sel = [sel[e:e + 1, :] for e in range(n_exp)]
    rows_sc = [scores[e:e + 1, :] for e in range(n_exp)]
    best = None
    for g in range(N_GROUPS):
        mem = rows_sel[g * epg:(g + 1) * epg]
        gs = None
        for a in range(epg):
            for c in range(a + 1, epg):
                pair = mem[a] + mem[c]
                gs = pair if gs is None else jnp.maximum(gs, pair)
        if best is None:
            best, grp = gs, jnp.zeros(gs.shape, jnp.int32)
        else:
            upd = gs > best
            best = jnp.where(upd, gs, best)
            grp = jnp.where(upd, g, grp)
    cand_sel, cand_sc = [], []
    for j in range(epg):
        cs, cc = rows_sel[j], rows_sc[j]
        for g in range(1, N_GROUPS):
            cs = jnp.where(grp == g, rows_sel[g * epg + j], cs)
            cc = jnp.where(grp == g, rows_sc[g * epg + j], cc)
        cand_sel.append(cs)
        cand_sc.append(cc)

    def first_argmax(vals, skip):
        bv = bi = bs = None
        for j in range(epg):
            v = vals[j] if skip is None else jnp.where(skip == j, -jnp.inf, vals[j])
            if bv is None:
                bv, bi, bs = v, jnp.zeros(v.shape, jnp.int32), cand_sc[0]
            else:
                upd = v > bv
                bv = jnp.where(upd, v, bv)
                bi = jnp.where(upd, j, bi)
                bs = jnp.where(upd, cand_sc[j], bs)
        return bi, bs

    i1, s1 = first_argmax(cand_sel, None)
    i2, s2 = first_argmax(cand_sel, i1)
    tot = s1 + s2
    zi = jnp.zeros((SUBLANES - TOP_K,) + i1.shape[1:], jnp.int32)
    zf = jnp.zeros((SUBLANES - TOP_K,) + i1.shape[1:], F32)
    idx = jnp.concatenate([grp * epg + i1, grp * epg + i2, zi], axis=0)
    gate = jnp.concatenate([s1 / tot, s2 / tot, zf], axis=0)
    return idx, gate


def _ln_route_kernel(h_ref, g_ref, b_ref, wh_ref, wl_ref, rb_ref, o_ref, op_ref, idx_ref, gate_ref, *, n_exp):
    y = _ln_rows(h_ref[...], g_ref[...], b_ref[...])
    half = y.shape[1] // 2
    o_ref[...] = y
    op_ref[...] = _pack_bf16_pair(y[:, :half], y[:, half:])
    yh = y.astype(BF16)
    yl = (y - yh.astype(F32)).astype(BF16)
    nt = (((1,), (1,)), ((), ()))
    logits = (lax.dot_general(wh_ref[...], yh, nt, preferred_element_type=F32)
              + lax.dot_general(wh_ref[...], yl, nt, preferred_element_type=F32)
              + lax.dot_general(wl_ref[...], yh, nt, preferred_element_type=F32))
    idx_ref[...], gate_ref[...] = _route(logits, rb_ref[...], n_exp)


def _ln_route(h, g, b, layer, w_router_t, bias_col, *, tm=256):
    t, d = h.shape
    n_exp = w_router_t.shape[0]
    tm = _tile(t, tm)
    wh = w_router_t.astype(BF16)
    wl = (w_router_t - wh.astype(F32)).astype(BF16)
    row = pl.BlockSpec((tm, d), lambda i: (i, 0))
    vec = pl.BlockSpec((None, 1, d), lambda i: (layer, 0, 0))
    wsp = pl.BlockSpec((n_exp, d), lambda i: (0, 0))
    rsp = pl.BlockSpec((SUBLANES, tm), lambda i: (0, i))
    return pl.pallas_call(
        functools.partial(_ln_route_kernel, n_exp=n_exp),
        grid=(t // tm,),
        in_specs=[row, vec, vec, wsp, wsp, pl.BlockSpec((n_exp, 1), lambda i: (0, 0))],
        out_specs=[row, pl.BlockSpec((tm, d // 2), lambda i: (i, 0)), rsp, rsp],
        out_shape=[jax.ShapeDtypeStruct((t, d), F32), jax.ShapeDtypeStruct((t, d // 2), jnp.uint32),
                   jax.ShapeDtypeStruct((SUBLANES, t), jnp.int32), jax.ShapeDtypeStruct((SUBLANES, t), F32)],
        compiler_params=_params(("parallel",)),
        name="ln_route",
    )(h, g, b, wh, wl, bias_col)


def _gather_kernel(nu_ref, rt_ref, x_hbm, o_ref, xg_ref, sem, *, tm):
    i = pl.program_id(0)
    n_used = nu_ref[0]

    def start_gather(blk, s):
        base = blk * tm

        def issue(r, c):
            pltpu.make_async_copy(x_hbm.at[pl.ds(rt_ref[base + r], 1), :], xg_ref.at[s, pl.ds(r, 1), :],
                                  sem.at[s]).start()
            return c

        lax.fori_loop(0, tm, issue, 0, unroll=8)

    @pl.when(jnp.logical_and(i == 0, n_used > 0))
    def _():
        start_gather(0, 0)

    @pl.when(i + 1 < n_used)
    def _():
        start_gather(i + 1, (i + 1) % 2)

    @pl.when(i < n_used)
    def _():
        s = i % 2
        pltpu.make_async_copy(x_hbm.at[pl.ds(0, tm), :], xg_ref.at[s], sem.at[s]).wait()
        lo, hi = _unpack_bf16_pair(xg_ref[s])
        half = lo.shape[1]
        o_ref[:, :half] = lo.astype(BF16)
        o_ref[:, half:] = hi.astype(BF16)

    @pl.when(i >= n_used)
    def _():
        o_ref[...] = jnp.zeros_like(o_ref)


def _gather_rows(n_used, row_token, xp, tm):
    d = 2 * xp.shape[1]
    r = row_token.shape[0]
    grid_spec = pltpu.PrefetchScalarGridSpec(
        num_scalar_prefetch=2,
        grid=(r // tm,),
        in_specs=[pl.BlockSpec(memory_space=pl.ANY)],
        out_specs=pl.BlockSpec((tm, d), lambda i, nu, rt: (i, 0)),
        scratch_shapes=[pltpu.VMEM((2, tm, d // 2), jnp.uint32), pltpu.SemaphoreType.DMA((2,))],
    )
    return pl.pallas_call(
        functools.partial(_gather_kernel, tm=tm),
        grid_spec=grid_spec,
        out_shape=jax.ShapeDtypeStruct((r, d), BF16),
        compiler_params=_params(("arbitrary",)),
        name="dispatch_gather",
    )(n_used, row_token, xp)


def _new_expert(be_ref, i):
    return jnp.logical_or(i == 0, be_ref[i] != be_ref[jnp.maximum(i - 1, 0)])


def _expert_up_kernel(be_ref, nu_ref, x_ref, wg_ref, wu_ref, o_ref, wgb_ref, wub_ref):
    i = pl.program_id(1)

    @pl.when(_new_expert(be_ref, i))
    def _():
        wgb_ref[...] = wg_ref[...].astype(BF16)
        wub_ref[...] = wu_ref[...].astype(BF16)

    @pl.when(i < nu_ref[0])
    def _():
        x = x_ref[...]
        hg = jnp.dot(x, wgb_ref[...], preferred_element_type=F32)
        hu = jnp.dot(x, wub_ref[...], preferred_element_type=F32)
        o_ref[...] = (hg * jax.nn.sigmoid(hg) * hu).astype(o_ref.dtype)

    @pl.when(i >= nu_ref[0])
    def _():
        o_ref[...] = jnp.zeros_like(o_ref)


def _expert_down_kernel(be_ref, nu_ref, h_ref, wlo_ref, whi_ref, o_ref, wlob_ref, whib_ref):
    i = pl.program_id(1)

    @pl.when(_new_expert(be_ref, i))
    def _():
        wlob_ref[...] = wlo_ref[...].astype(BF16)
        whib_ref[...] = whi_ref[...].astype(BF16)

    @pl.when(i < nu_ref[0])
    def _():
        h = h_ref[...]
        o_ref[...] = _pack_bf16_pair(jnp.dot(h, wlob_ref[...], preferred_element_type=F32),
                                     jnp.dot(h, whib_ref[...], preferred_element_type=F32))

    @pl.when(i >= nu_ref[0])
    def _():
        o_ref[...] = jnp.zeros_like(o_ref)


def _experts(block_e, n_used, xs, wg, wu, wd, layer, tm, *, tf=512, tn=1024):
    r, d = xs.shape
    ff = wg.shape[3]
    tf = _tile(ff, tf)
    nblk = r // tm
    hid = pl.pallas_call(
        _expert_up_kernel,
        grid_spec=pltpu.PrefetchScalarGridSpec(
            num_scalar_prefetch=2,
            grid=(ff // tf, nblk),
            in_specs=[
                pl.BlockSpec((tm, d), lambda f, i, be, nu: (i, 0)),
                pl.BlockSpec((None, None, d, tf), lambda f, i, be, nu: (layer, be[i], 0, f)),
                pl.BlockSpec((None, None, d, tf), lambda f, i, be, nu: (layer, be[i], 0, f)),
            ],
            out_specs=pl.BlockSpec((tm, tf), lambda f, i, be, nu: (i, f)),
            scratch_shapes=[pltpu.VMEM((d, tf), BF16), pltpu.VMEM((d, tf), BF16)],
        ),
        out_shape=jax.ShapeDtypeStruct((r, ff), BF16),
        compiler_params=_params(("arbitrary", "arbitrary")),
        name="expert_up",
    )(block_e, n_used, xs, wg, wu)
    half = d // 2
    tn = _tile(half, tn)
    hi0 = half // tn
    return pl.pallas_call(
        _expert_down_kernel,
        grid_spec=pltpu.PrefetchScalarGridSpec(
            num_scalar_prefetch=2,
            grid=(half // tn, nblk),
            in_specs=[
                pl.BlockSpec((tm, ff), lambda n, i, be, nu: (i, 0)),
                pl.BlockSpec((None, None, ff, tn), lambda n, i, be, nu: (layer, be[i], 0, n)),
                pl.BlockSpec((None, None, ff, tn), lambda n, i, be, nu: (layer, be[i], 0, hi0 + n)),
            ],
            out_specs=pl.BlockSpec((tm, tn), lambda n, i, be, nu: (i, n)),
            scratch_shapes=[pltpu.VMEM((ff, tn), BF16), pltpu.VMEM((ff, tn), BF16)],
        ),
        out_shape=jax.ShapeDtypeStruct((r, half), jnp.uint32),
        compiler_params=_params(("arbitrary", "arbitrary")),
        name="expert_down",
    )(block_e, n_used, hid, wd, wd)


def _combine_kernel(slot_ref, x_ref, y_hbm, gt_ref, g_ref, b_ref, o_ref, ob_ref, ybuf, sem, *, alpha, tm):
    i = pl.program_id(0)
    n = pl.num_programs(0)

    def row_copy(row, s, k, r):
        return pltpu.make_async_copy(y_hbm.at[pl.ds(row, 1), :], ybuf.at[s, k, pl.ds(r, 1), :], sem.at[s])

    def start_gather(step, s):
        base = step * (tm * TOP_K)

        def issue(r, c):
            for k in range(TOP_K):
                row_copy(slot_ref[base + r * TOP_K + k], s, k, r).start()
            return c

        lax.fori_loop(0, tm, issue, 0, unroll=4)

    def wait_gather(s):
        for k in range(TOP_K):
            pltpu.make_async_copy(y_hbm.at[pl.ds(0, tm), :], ybuf.at[s, k], sem.at[s]).wait()

    @pl.when(i == 0)
    def _():
        start_gather(0, 0)

    @pl.when(i + 1 < n)
    def _():
        start_gather(i + 1, (i + 1) % 2)

    s = i % 2
    wait_gather(s)
    gt = gt_ref[...]
    a_lo, a_hi = _unpack_bf16_pair(ybuf[s, 0])
    b_lo, b_hi = _unpack_bf16_pair(ybuf[s, 1])
    g0, g1 = gt[:, 0:1], gt[:, 1:2]
    ffn = jnp.concatenate([g0 * a_lo + g1 * b_lo, g0 * a_hi + g1 * b_hi], axis=1)
    y = _ln_rows(alpha * x_ref[...] + ffn, g_ref[...], b_ref[...])
    o_ref[...] = y
    ob_ref[...] = y.astype(BF16)


def _combine_ln(slot, x, y, gates_t, g, b, layer, alpha, *, tm=256):
    t, d = x.shape
    tm = _tile(t, tm)
    row = pl.BlockSpec((tm, d), lambda i, sl: (i, 0))
    vec = pl.BlockSpec((None, 1, d), lambda i, sl: (layer, 0, 0))
    grid_spec = pltpu.PrefetchScalarGridSpec(
        num_scalar_prefetch=1,
        grid=(t // tm,),
        in_specs=[row, pl.BlockSpec(memory_space=pl.ANY),
                  pl.BlockSpec((tm, SUBLANES), lambda i, sl: (i, 0)), vec, vec],
        out_specs=[row, row],
        scratch_shapes=[pltpu.VMEM((2, TOP_K, tm, d // 2), jnp.uint32), pltpu.SemaphoreType.DMA((2,))],
    )
    return pl.pallas_call(
        functools.partial(_combine_kernel, alpha=alpha, tm=tm),
        grid_spec=grid_spec,
        out_shape=[jax.ShapeDtypeStruct((t, d), F32), jax.ShapeDtypeStruct((t, d), BF16)],
        compiler_params=_params(("arbitrary",)),
        name="combine_ln",
    )(slot, x, y, gates_t, g, b)


def _rope_tables(seq):
    axis_dim = HEAD_DIM // 2
    rows = seq // GRID_W
    inv = ROPE_THETA ** (-jnp.arange(0, axis_dim, 2, dtype=F32) / axis_dim)
    ang_r = jnp.repeat(jnp.arange(rows, dtype=F32), GRID_W)[:, None] * inv[None, :]
    ang_c = jnp.tile(jnp.arange(GRID_W, dtype=F32), rows)[:, None] * inv[None, :]
    cr, sr, cc, sc = jnp.cos(ang_r), jnp.sin(ang_r), jnp.cos(ang_c), jnp.sin(ang_c)
    cos_t = jnp.concatenate([cr, cr, cc, cc], axis=1)
    sin_t = jnp.concatenate([-sr, sr, -sc, sc], axis=1)
    return cos_t, sin_t


def _dispatch(idx, n_exp, tm):
    t = idx.shape[1]
    n_asg = t * TOP_K
    e_flat = idx.T.reshape(-1)
    onehot = (e_flat[:, None] == jnp.arange(n_exp, dtype=jnp.int32)[None, :]).astype(jnp.int32)
    csum = jnp.cumsum(onehot, axis=0)
    rank = jnp.sum(csum * onehot, axis=1) - 1
    counts = csum[-1]
    pcounts = (counts + tm - 1) // tm * tm
    pends = jnp.cumsum(pcounts)
    pstarts = pends - pcounts
    slot = (pstarts[e_flat] + rank).astype(jnp.int32)
    nblk = n_asg // tm + n_exp
    row_token = jnp.zeros((nblk * tm,), jnp.int32).at[slot].set(jnp.arange(n_asg, dtype=jnp.int32) // TOP_K)
    block_start = jnp.arange(nblk, dtype=jnp.int32) * tm
    block_e = jnp.minimum(jnp.sum(pends[None, :] <= block_start[:, None], axis=-1), n_exp - 1).astype(jnp.int32)
    n_used = (pends[-1] // tm).astype(jnp.int32).reshape(1)
    return slot, row_token, block_e, n_used


def kernel(x, w_in, q_norm_g, k_norm_g, conv_w, w_attn_proj, w_conv_proj, w_out, ln1_g, ln1_b,
           w_router, router_bias, w_gate, w_up, w_down, ln2_g, ln2_b):
    batch, seq, d = x.shape
    depth = w_in.shape[0]
    t = batch * seq
    attn_w = d // 2
    n_q = attn_w // HEAD_DIM
    n_kv = n_q // GQA_RATIO
    kv_w = n_kv * HEAD_DIM
    conv_wd = d // 2
    n_exp = w_router.shape[1]
    alpha = (2 * depth) ** 0.25
    q_end = attn_w
    k_end = q_end + kv_w
    v_end = k_end + kv_w
    cb_end = v_end + conv_wd
    cc_end = cb_end + conv_wd
    ch_end = cc_end + conv_wd
    ga_end = ch_end + d
    expert_tm = min(512, t)

    cos_t, sin_t = _rope_tables(seq)
    scale = HEAD_DIM ** -0.5 * math.log2(math.e)
    w_router_t = w_router.T
    bias_col = router_bias.reshape(n_exp, 1).astype(F32)
    ln1_g3, ln1_b3 = ln1_g.reshape(depth, 1, d), ln1_b.reshape(depth, 1, d)
    ln2_g3, ln2_b3 = ln2_g.reshape(depth, 1, d), ln2_b.reshape(depth, 1, d)

    xf = x.reshape(t, d)
    xb = xf.astype(BF16)
    for l in range(depth):
        p = _matmul(xb, w_in, l, BF16, name="in_proj")
        gains = jnp.concatenate([jnp.tile(q_norm_g[l] * scale, n_q), jnp.tile(k_norm_g[l], n_kv)]).reshape(1, k_end)
        qk = _qk_prep(p, gains.astype(F32), cos_t, sin_t, seq, k_end)
        attn = _attention(qk, p, batch, seq, n_kv, q_end // HEAD_DIM, k_end // HEAD_DIM)
        conv = _short_conv(p, conv_w, l, seq, v_end, cb_end, cc_end, conv_wd)
        merged = _merge(attn, conv, w_attn_proj, w_conv_proj, p, l, ch_end, ga_end)
        h1 = _matmul(merged, w_out, l, F32, res=xf, alpha=alpha, tm=1024, tn=512, name="out_proj")
        x1, x1p, idx, gate = _ln_route(h1, ln1_g3, ln1_b3, l, w_router_t, bias_col)
        slot, row_token, block_e, n_used = _dispatch(idx[:TOP_K], n_exp, expert_tm)
        xs = _gather_rows(n_used, row_token, x1p, expert_tm)
        y = _experts(block_e, n_used, xs, w_gate, w_up, w_down, l, expert_tm)
        xf, xb = _combine_ln(slot, x1, y, gate.T, ln2_g3, ln2_b3, l, alpha)
    return xf.reshape(batch, seq, d)
```

```python
import functools
import math

import jax
import jax.numpy as jnp
from jax import lax
from jax.experimental import pallas as pl
from jax.experimental.pallas import tpu as pltpu

HEAD_DIM = 128
GQA_RATIO = 4
CONV_K = 3
GRID_W = 64
ROPE_THETA = 10000.0
N_GROUPS = 4
TOP_K = 2
RMS_EPS = 1e-6
LN_EPS = 1e-5
LANES = 128
SUBLANES = 8
ONES_ROWS = 2 * SUBLANES
ISSUE_UNROLL = 8
VMEM_LIMIT = 56 * 1024 * 1024

F32 = jnp.float32
BF16 = jnp.bfloat16


def _params(semantics):
    return pltpu.CompilerParams(dimension_semantics=semantics, vmem_limit_bytes=VMEM_LIMIT)


def _tile(dim, pref):
    t = min(dim, pref)
    while dim % t:
        t //= 2
    return t


def _pack_bf16_pair(lo, hi):
    lo_b = lax.bitcast_convert_type(lo.astype(BF16).astype(F32), jnp.uint32) >> 16
    hi_b = lax.bitcast_convert_type(hi.astype(BF16).astype(F32), jnp.uint32) & jnp.uint32(0xFFFF0000)
    return hi_b | lo_b


def _unpack_bf16_pair(w):
    lo = lax.bitcast_convert_type(w << 16, F32)
    hi = lax.bitcast_convert_type(w & jnp.uint32(0xFFFF0000), F32)
    return lo, hi


def _mm_kernel(x_ref, w_ref, o_ref, wb_ref):
    @pl.when(pl.program_id(1) == 0)
    def _():
        wb_ref[...] = w_ref[...].astype(BF16)

    o_ref[...] = jnp.dot(x_ref[...], wb_ref[...], preferred_element_type=F32).astype(o_ref.dtype)


def _mm_res_kernel(x_ref, w_ref, r_ref, o_ref, wb_ref, *, alpha):
    @pl.when(pl.program_id(1) == 0)
    def _():
        wb_ref[...] = w_ref[...].astype(BF16)

    acc = jnp.dot(x_ref[...], wb_ref[...], preferred_element_type=F32)
    o_ref[...] = (alpha * r_ref[...] + acc).astype(o_ref.dtype)


def _matmul(x, w, layer, out_dtype, *, res=None, alpha=1.0, tm=512, tn=1024, name="matmul"):
    m, k = x.shape
    n = w.shape[2]
    tm, tn = _tile(m, tm), _tile(n, tn)
    in_specs = [
        pl.BlockSpec((tm, k), lambda j, i: (i, 0)),
        pl.BlockSpec((None, k, tn), lambda j, i: (layer, 0, j)),
    ]
    args = [x, w]
    body = _mm_kernel
    if res is not None:
        in_specs.append(pl.BlockSpec((tm, tn), lambda j, i: (i, j)))
        args.append(res)
        body = functools.partial(_mm_res_kernel, alpha=alpha)
    return pl.pallas_call(
        body,
        grid=(n // tn, m // tm),
        in_specs=in_specs,
        out_specs=pl.BlockSpec((tm, tn), lambda j, i: (i, j)),
        out_shape=jax.ShapeDtypeStruct((m, n), out_dtype),
        scratch_shapes=[pltpu.VMEM((k, tn), BF16)],
        compiler_params=_params(("parallel", "arbitrary")),
        name=name,
    )(*args)


def _qk_prep_kernel(p_ref, g_ref, c_ref, s_ref, ones_ref, perm_ref, o_ref, *, heads):
    cos = c_ref[...]
    sin = s_ref[...]
    ones = ones_ref[...]
    perm = perm_ref[...]
    for h in range(heads):
        sl = slice(h * HEAD_DIM, (h + 1) * HEAD_DIM)
        xh = p_ref[:, sl].astype(F32)
        ssq = jnp.dot((xh * xh).astype(BF16), ones, preferred_element_type=F32)
        y = xh * lax.rsqrt(ssq * (1.0 / HEAD_DIM) + RMS_EPS) * g_ref[:, sl]
        swapped = jnp.dot(y.astype(BF16), perm, preferred_element_type=F32)
        o_ref[:, sl] = (y * cos + swapped * sin).astype(o_ref.dtype)


def _qk_prep(p, gains, cos_t, sin_t, seq, qk_width, *, tm=512, heads=32):
    t = p.shape[0]
    tm = _tile(seq, tm)
    n_heads = qk_width // HEAD_DIM
    heads = _tile(n_heads, heads)
    wblk = heads * HEAD_DIM
    sblocks = seq // tm
    ones = jnp.ones((HEAD_DIM, HEAD_DIM), BF16)
    lane = jnp.arange(HEAD_DIM)
    partner = jnp.where((lane % 64) < 32, lane + 32, lane - 32)
    perm = (lane[:, None] == partner[None, :]).astype(BF16)
    return pl.pallas_call(
        functools.partial(_qk_prep_kernel, heads=heads),
        grid=(t // tm, n_heads // heads),
        in_specs=[
            pl.BlockSpec((tm, wblk), lambda i, j: (i, j)),
            pl.BlockSpec((1, wblk), lambda i, j: (0, j)),
            pl.BlockSpec((tm, HEAD_DIM), lambda i, j: (i % sblocks, 0)),
            pl.BlockSpec((tm, HEAD_DIM), lambda i, j: (i % sblocks, 0)),
            pl.BlockSpec((HEAD_DIM, HEAD_DIM), lambda i, j: (0, 0)),
            pl.BlockSpec((HEAD_DIM, HEAD_DIM), lambda i, j: (0, 0)),
        ],
        out_specs=pl.BlockSpec((tm, wblk), lambda i, j: (i, j)),
        out_shape=jax.ShapeDtypeStruct((t, qk_width), BF16),
        compiler_params=_params(("parallel", "parallel")),
        name="qk_prep",
    )(p, gains, cos_t, sin_t, ones, perm)


def _flash_kernel(q_ref, k_ref, v_ref, o_ref, qt_ref, vt_ref, *stat_refs, tq, tk, cb, seq, unroll, ahead):
    nkv = seq // tk
    ncb = GQA_RATIO * tq // cb
    acc_refs, m_refs = stat_refs[:ncb], stat_refs[ncb:]

    @pl.when(pl.program_id(2) == 0)
    def _():
        for c in range(nkv):
            vt_ref[c, :HEAD_DIM, :] = v_ref[c * tk:(c + 1) * tk, :].T
            vt_ref[c, HEAD_DIM:, :] = jnp.ones((ONES_ROWS, tk), BF16)

    for g in range(GQA_RATIO):
        qt_ref[:, g * tq:(g + 1) * tq] = q_ref[:, g * HEAD_DIM:(g + 1) * HEAD_DIM].T
    for c in range(ncb):
        m_refs[c][...] = jnp.full(m_refs[c].shape, -jnp.inf, F32)
        acc_refs[c][...] = jnp.zeros(acc_refs[c].shape, F32)

    def body(j, carry):
        tiles = [(u, c) for u in range(unroll) for c in range(ncb)]
        ks, vts = [], []
        for u in range(unroll):
            off = pl.multiple_of((j * unroll + u) * tk, tk)
            ks.append(k_ref[pl.ds(off, tk), :])
            vts.append(vt_ref[j * unroll + u])

        def scores(t):
            u, c = tiles[t]
            return jnp.dot(ks[u], qt_ref[:, c * cb:(c + 1) * cb], preferred_element_type=F32)

        pending = [scores(t) for t in range(min(ahead, len(tiles)))]
        for t, (u, c) in enumerate(tiles):
            st = pending.pop(0)
            if t + ahead < len(tiles):
                pending.append(scores(t + ahead))
            m_old = m_refs[c][...]
            m_new = jnp.maximum(m_old, jnp.max(st, axis=0, keepdims=True))
            alpha = jnp.exp2(m_old - m_new)
            pt = jnp.exp2(st - m_new).astype(BF16)
            m_refs[c][...] = m_new
            acc_refs[c][...] = alpha * acc_refs[c][...] + jnp.dot(vts[u], pt, preferred_element_type=F32)
        return carry

    lax.fori_loop(0, nkv // unroll, body, 0)
    per_g = tq // cb
    for c in range(ncb):
        g, r = divmod(c, per_g)
        acc = acc_refs[c][...]
        out_t = acc[:HEAD_DIM, :] / acc[HEAD_DIM:HEAD_DIM + 1, :]
        o_ref[r * cb:(r + 1) * cb, g * HEAD_DIM:(g + 1) * HEAD_DIM] = out_t.T.astype(o_ref.dtype)


def _attention(qk, p, batch, seq, n_kv, k_col0, v_col0, *, tq=512, tk=256, cb=256, unroll=8, ahead=5):
    tq, tk = _tile(seq, tq), _tile(seq, tk)
    rows = GQA_RATIO * tq
    cb = _tile(rows, cb)
    unroll = _tile(seq // tk, unroll)
    qk3 = qk.reshape(batch, seq, qk.shape[1])
    p3 = p.reshape(batch, seq, p.shape[1])
    gw = GQA_RATIO * HEAD_DIM
    out = pl.pallas_call(
        functools.partial(_flash_kernel, tq=tq, tk=tk, cb=cb, seq=seq, unroll=unroll, ahead=ahead),
        grid=(batch, n_kv, seq // tq),
        in_specs=[
            pl.BlockSpec((None, tq, gw), lambda b, h, i: (b, i, h)),
            pl.BlockSpec((None, seq, HEAD_DIM), lambda b, h, i: (b, 0, k_col0 + h)),
            pl.BlockSpec((None, seq, HEAD_DIM), lambda b, h, i: (b, 0, v_col0 + h)),
        ],
        out_specs=pl.BlockSpec((None, tq, gw), lambda b, h, i: (b, i, h)),
        out_shape=jax.ShapeDtypeStruct((batch, seq, n_kv * gw), BF16),
        scratch_shapes=[
            pltpu.VMEM((HEAD_DIM, rows), BF16),
            pltpu.VMEM((seq // tk, HEAD_DIM + ONES_ROWS, tk), BF16),
        ] + [pltpu.VMEM((HEAD_DIM + ONES_ROWS, cb), F32)] * (rows // cb)
        + [pltpu.VMEM((1, cb), F32)] * (rows // cb),
        compiler_params=_params(("parallel", "parallel", "arbitrary")),
        name="flash_gqa",
    )(qk3, qk3, p3)
    return out.reshape(batch * seq, n_kv * gw)


def _conv_kernel(cb_ref, cc_ref, ch_ref, cbp_ref, chp_ref, cbn_ref, chn_ref, w_ref, o_ref, *, tm, sblocks):
    i = pl.program_id(0)
    first = (i % sblocks) == 0
    last = (i % sblocks) == sblocks - 1
    bx = cb_ref[...].astype(F32) * ch_ref[...].astype(F32)
    prev_row = cbp_ref[SUBLANES - 1:SUBLANES, :].astype(F32) * chp_ref[SUBLANES - 1:SUBLANES, :].astype(F32)
    next_row = cbn_ref[0:1, :].astype(F32) * chn_ref[0:1, :].astype(F32)
    prev_row = jnp.where(first, 0.0, prev_row)
    next_row = jnp.where(last, 0.0, next_row)
    row = lax.broadcasted_iota(jnp.int32, bx.shape, 0)
    prev = jnp.where(row == 0, prev_row, pltpu.roll(bx, 1, axis=0))
    nxt = jnp.where(row == tm - 1, next_row, pltpu.roll(bx, tm - 1, axis=0))
    w = w_ref[...]
    y = w[0:1, :] * prev + w[1:2, :] * bx + w[2:3, :] * nxt
    o_ref[...] = (cc_ref[...].astype(F32) * y).astype(o_ref.dtype)


def _short_conv(p, conv_w, layer, seq, cb0, cc0, ch0, width, *, tm=512, tc=1024):
    t = p.shape[0]
    tm, tc = _tile(seq, tm), math.gcd(cb0, cc0, ch0, width, tc)
    sblocks = seq // tm
    rpb = tm // SUBLANES
    nrb = t // SUBLANES

    def main(c0):
        return pl.BlockSpec((tm, tc), lambda i, j: (i, c0 // tc + j))

    def halo_prev(c0):
        return pl.BlockSpec((SUBLANES, tc), lambda i, j: (jnp.maximum(i * rpb - 1, 0), c0 // tc + j))

    def halo_next(c0):
        return pl.BlockSpec((SUBLANES, tc), lambda i, j: (jnp.minimum((i + 1) * rpb, nrb - 1), c0 // tc + j))

    return pl.pallas_call(
        functools.partial(_conv_kernel, tm=tm, sblocks=sblocks),
        grid=(t // tm, width // tc),
        in_specs=[main(cb0), main(cc0), main(ch0), halo_prev(cb0), halo_prev(ch0),
                  halo_next(cb0), halo_next(ch0),
                  pl.BlockSpec((None, CONV_K, tc), lambda i, j: (layer, 0, j))],
        out_specs=pl.BlockSpec((tm, tc), lambda i, j: (i, j)),
        out_shape=jax.ShapeDtypeStruct((t, width), BF16),
        compiler_params=_params(("parallel", "parallel")),
        name="short_conv",
    )(p, p, p, p, p, p, p, conv_w)


def _merge_kernel(a_ref, c_ref, wa_ref, wc_ref, ga_ref, gc_ref, o_ref, wab_ref, wcb_ref):
    @pl.when(pl.program_id(1) == 0)
    def _():
        wab_ref[...] = wa_ref[...].astype(BF16)
        wcb_ref[...] = wc_ref[...].astype(BF16)

    ya = jnp.dot(a_ref[...], wab_ref[...], preferred_element_type=F32)
    yc = jnp.dot(c_ref[...], wcb_ref[...], preferred_element_type=F32)
    o_ref[...] = (jax.nn.sigmoid(ga_ref[...].astype(F32)) * ya
                  + jax.nn.sigmoid(gc_ref[...].astype(F32)) * yc).astype(o_ref.dtype)


def _merge(attn, conv, wa, wc, p, layer, ga0, gc0, *, tm=1024, tn=512):
    t, ka = attn.shape
    kc = conv.shape[1]
    d = wa.shape[2]
    tm, tn = _tile(t, tm), math.gcd(ga0, gc0, d, tn)
    return pl.pallas_call(
        _merge_kernel,
        grid=(d // tn, t // tm),
        in_specs=[
            pl.BlockSpec((tm, ka), lambda j, i: (i, 0)),
            pl.BlockSpec((tm, kc), lambda j, i: (i, 0)),
            pl.BlockSpec((None, ka, tn), lambda j, i: (layer, 0, j)),
            pl.BlockSpec((None, kc, tn), lambda j, i: (layer, 0, j)),
            pl.BlockSpec((tm, tn), lambda j, i: (i, ga0 // tn + j)),
            pl.BlockSpec((tm, tn), lambda j, i: (i, gc0 // tn + j)),
        ],
        out_specs=pl.BlockSpec((tm, tn), lambda j, i: (i, j)),
        out_shape=jax.ShapeDtypeStruct((t, d), BF16),
        scratch_shapes=[pltpu.VMEM((ka, tn), BF16), pltpu.VMEM((kc, tn), BF16)],
        compiler_params=_params(("parallel", "arbitrary")),
        name="gated_merge",
    )(attn, conv, wa, wc, p, p)


def _ln_rows(h, g, b):
    mu = jnp.mean(h, axis=-1, keepdims=True)
    hc = h - mu
    var = jnp.mean(hc * hc, axis=-1, keepdims=True)
    return hc * lax.rsqrt(var + LN_EPS) * g + b


def _route(logits, bias, n_exp):
    epg = n_exp // N_GROUPS
    mx = jnp.max(logits, axis=0, keepdims=True)
    ex = jnp.exp(logits - mx)
    scores = ex / jnp.sum(ex, axis=0, keepdims=True)
    sel = scores + bias
    rows_sel = [sel[e:e + 1, :] for e in range(n_exp)]
    rows_sc = [scores[e:e + 1, :] for e in range(n_exp)]
    best = None
    for g in range(N_GROUPS):
        mem = rows_sel[g * epg:(g + 1) * epg]
        gs = None
        for a in range(epg):
            for c in range(a + 1, epg):
                pair = mem[a] + mem[c]
                gs = pair if gs is None else jnp.maximum(gs, pair)
        if best is None:
            best, grp = gs, jnp.zeros(gs.shape, jnp.int32)
        else:
            upd = gs > best
            best = jnp.where(upd, gs, best)
            grp = jnp.where(upd, g, grp)
    cand_sel, cand_sc = [], []
    for j in range(epg):
        cs, cc = rows_sel[j], rows_sc[j]
        for g in range(1, N_GROUPS):
            cs = jnp.where(grp == g, rows_sel[g * epg + j], cs)
            cc = jnp.where(grp == g, rows_sc[g * epg + j], cc)
        cand_sel.append(cs)
        cand_sc.append(cc)

    def first_argmax(vals, skip):
        bv = bi = bs = None
        for j in range(epg):
            v = vals[j] if skip is None else jnp.where(skip == j, -jnp.inf, vals[j])
            if bv is None:
                bv, bi, bs = v, jnp.zeros(v.shape, jnp.int32), cand_sc[0]
            else:
                upd = v > bv
                bv = jnp.where(upd, v, bv)
                bi = jnp.where(upd, j, bi)
                bs = jnp.where(upd, cand_sc[j], bs)
        return bi, bs

    i1, s1 = first_argmax(cand_sel, None)
    i2, s2 = first_argmax(cand_sel, i1)
    tot = s1 + s2
    zi = jnp.zeros((SUBLANES - TOP_K,) + i1.shape[1:], jnp.int32)
    zf = jnp.zeros((SUBLANES - TOP_K,) + i1.shape[1:], F32)
    idx = jnp.concatenate([grp * epg + i1, grp * epg + i2, zi], axis=0)
    gate = jnp.concatenate([s1 / tot, s2 / tot, zf], axis=0)
    return idx, gate


def _ln_route_kernel(h_ref, g_ref, b_ref, wh_ref, wl_ref, rb_ref, o_ref, op_ref, idx_ref, gate_ref, *, n_exp):
    y = _ln_rows(h_ref[...], g_ref[...], b_ref[...])
    half = y.shape[1] // 2
    o_ref[...] = y
    op_ref[...] = _pack_bf16_pair(y[:, :half], y[:, half:])
    yh = y.astype(BF16)
    yl = (y - yh.astype(F32)).astype(BF16)
    nt = (((1,), (1,)), ((), ()))
    logits = (lax.dot_general(wh_ref[...], yh, nt, preferred_element_type=F32)
              + lax.dot_general(wh_ref[...], yl, nt, preferred_element_type=F32)
              + lax.dot_general(wl_ref[...], yh, nt, preferred_element_type=F32))
    idx_ref[...], gate_ref[...] = _route(logits, rb_ref[...], n_exp)


def _ln_route(h, g, b, layer, w_router_t, bias_col, *, tm=256):
    t, d = h.shape
    n_exp = w_router_t.shape[0]
    tm = _tile(t, tm)
    wh = w_router_t.astype(BF16)
    wl = (w_router_t - wh.astype(F32)).astype(BF16)
    row = pl.BlockSpec((tm, d), lambda i: (i, 0))
    vec = pl.BlockSpec((None, 1, d), lambda i: (layer, 0, 0))
    wsp = pl.BlockSpec((n_exp, d), lambda i: (0, 0))
    rsp = pl.BlockSpec((SUBLANES, tm), lambda i: (0, i))
    return pl.pallas_call(
        functools.partial(_ln_route_kernel, n_exp=n_exp),
        grid=(t // tm,),
        in_specs=[row, vec, vec, wsp, wsp, pl.BlockSpec((n_exp, 1), lambda i: (0, 0))],
        out_specs=[row, pl.BlockSpec((tm, d // 2), lambda i: (i, 0)), rsp, rsp],
        out_shape=[jax.ShapeDtypeStruct((t, d), F32), jax.ShapeDtypeStruct((t, d // 2), jnp.uint32),
                   jax.ShapeDtypeStruct((SUBLANES, t), jnp.int32), jax.ShapeDtypeStruct((SUBLANES, t), F32)],
        compiler_params=_params(("parallel",)),
        name="ln_route",
    )(h, g, b, wh, wl, bias_col)


def _gather_kernel(nu_ref, rt_ref, x_hbm, o_ref, xg_ref, sem, *, tm):
    i = pl.program_id(0)
    n_used = nu_ref[0]

    def start_gather(blk, s):
        base = blk * tm

        def issue(g, c):
            for u in range(ISSUE_UNROLL):
                r = g * ISSUE_UNROLL + u
                pltpu.make_async_copy(x_hbm.at[pl.ds(rt_ref[base + r], 1), :], xg_ref.at[s, pl.ds(r, 1), :],
                                      sem.at[s]).start(priority=u % 2)
            return c

        lax.fori_loop(0, tm // ISSUE_UNROLL, issue, 0)

    @pl.when(jnp.logical_and(i == 0, n_used > 0))
    def _():
        start_gather(0, 0)

    @pl.when(i + 1 < n_used)
    def _():
        start_gather(i + 1, (i + 1) % 2)

    @pl.when(i < n_used)
    def _():
        s = i % 2
        pltpu.make_async_copy(x_hbm.at[pl.ds(0, tm), :], xg_ref.at[s], sem.at[s]).wait()
        lo, hi = _unpack_bf16_pair(xg_ref[s])
        half = lo.shape[1]
        o_ref[:, :half] = lo.astype(BF16)
        o_ref[:, half:] = hi.astype(BF16)

    @pl.when(i >= n_used)
    def _():
        o_ref[...] = jnp.zeros_like(o_ref)


def _gather_rows(n_used, row_token, xp, tm):
    d = 2 * xp.shape[1]
    r = row_token.shape[0]
    grid_spec = pltpu.PrefetchScalarGridSpec(
        num_scalar_prefetch=2,
        grid=(r // tm,),
        in_specs=[pl.BlockSpec(memory_space=pl.ANY)],
        out_specs=pl.BlockSpec((tm, d), lambda i, nu, rt: (i, 0)),
        scratch_shapes=[pltpu.VMEM((2, tm, d // 2), jnp.uint32), pltpu.SemaphoreType.DMA((2,))],
    )
    return pl.pallas_call(
        functools.partial(_gather_kernel, tm=tm),
        grid_spec=grid_spec,
        out_shape=jax.ShapeDtypeStruct((r, d), BF16),
        compiler_params=_params(("arbitrary",)),
        name="dispatch_gather",
    )(n_used, row_token, xp)


def _new_expert(be_ref, i):
    return jnp.logical_or(i == 0, be_ref[i] != be_ref[jnp.maximum(i - 1, 0)])


def _expert_up_kernel(be_ref, nu_ref, x_ref, wg_ref, wu_ref, o_ref, wgb_ref, wub_ref):
    i = pl.program_id(1)

    @pl.when(_new_expert(be_ref, i))
    def _():
        wgb_ref[...] = wg_ref[...].astype(BF16)
        wub_ref[...] = wu_ref[...].astype(BF16)

    @pl.when(i < nu_ref[0])
    def _():
        x = x_ref[...]
        hg = jnp.dot(x, wgb_ref[...], preferred_element_type=F32)
        hu = jnp.dot(x, wub_ref[...], preferred_element_type=F32)
        o_ref[...] = (hg * jax.nn.sigmoid(hg) * hu).astype(o_ref.dtype)

    @pl.when(i >= nu_ref[0])
    def _():
        o_ref[...] = jnp.zeros_like(o_ref)


def _expert_down_kernel(be_ref, nu_ref, h_ref, wlo_ref, whi_ref, o_ref, wlob_ref, whib_ref):
    i = pl.program_id(1)

    @pl.when(_new_expert(be_ref, i))
    def _():
        wlob_ref[...] = wlo_ref[...].astype(BF16)
        whib_ref[...] = whi_ref[...].astype(BF16)

    @pl.when(i < nu_ref[0])
    def _():
        h = h_ref[...]
        o_ref[...] = _pack_bf16_pair(jnp.dot(h, wlob_ref[...], preferred_element_type=F32),
                                     jnp.dot(h, whib_ref[...], preferred_element_type=F32))

    @pl.when(i >= nu_ref[0])
    def _():
        o_ref[...] = jnp.zeros_like(o_ref)


def _experts(block_e, n_used, xs, wg, wu, wd, layer, tm, *, tf=512, tn=1024):
    r, d = xs.shape
    ff = wg.shape[3]
    tf = _tile(ff, tf)
    nblk = r // tm
    hid = pl.pallas_call(
        _expert_up_kernel,
        grid_spec=pltpu.PrefetchScalarGridSpec(
            num_scalar_prefetch=2,
            grid=(ff // tf, nblk),
            in_specs=[
                pl.BlockSpec((tm, d), lambda f, i, be, nu: (i, 0)),
                pl.BlockSpec((None, None, d, tf), lambda f, i, be, nu: (layer, be[i], 0, f)),
                pl.BlockSpec((None, None, d, tf), lambda f, i, be, nu: (layer, be[i], 0, f)),
            ],
            out_specs=pl.BlockSpec((tm, tf), lambda f, i, be, nu: (i, f)),
            scratch_shapes=[pltpu.VMEM((d, tf), BF16), pltpu.VMEM((d, tf), BF16)],
        ),
        out_shape=jax.ShapeDtypeStruct((r, ff), BF16),
        compiler_params=_params(("arbitrary", "arbitrary")),
        name="expert_up",
    )(block_e, n_used, xs, wg, wu)
    half = d // 2
    tn = _tile(half, tn)
    hi0 = half // tn
    return pl.pallas_call(
        _expert_down_kernel,
        grid_spec=pltpu.PrefetchScalarGridSpec(
            num_scalar_prefetch=2,
            grid=(half // tn, nblk),
            in_specs=[
                pl.BlockSpec((tm, ff), lambda n, i, be, nu: (i, 0)),
                pl.BlockSpec((None, None, ff, tn), lambda n, i, be, nu: (layer, be[i], 0, n)),
                pl.BlockSpec((None, None, ff, tn), lambda n, i, be, nu: (layer, be[i], 0, hi0 + n)),
            ],
            out_specs=pl.BlockSpec((tm, tn), lambda n, i, be, nu: (i, n)),
            scratch_shapes=[pltpu.VMEM((ff, tn), BF16), pltpu.VMEM((ff, tn), BF16)],
        ),
        out_shape=jax.ShapeDtypeStruct((r, half), jnp.uint32),
        compiler_params=_params(("arbitrary", "arbitrary")),
        name="expert_down",
    )(block_e, n_used, hid, wd, wd)


def _combine_kernel(slot_ref, x_ref, y_hbm, gt_ref, g_ref, b_ref, o_ref, ob_ref, ybuf, sem, *, alpha, tm):
    i = pl.program_id(0)
    n = pl.num_programs(0)

    def row_copy(row, s, k, r):
        return pltpu.make_async_copy(y_hbm.at[pl.ds(row, 1), :], ybuf.at[s, k, pl.ds(r, 1), :], sem.at[s])

    def start_gather(step, s):
        base = step * (tm * TOP_K)

        def issue(r, c):
            for k in range(TOP_K):
                row_copy(slot_ref[base + r * TOP_K + k], s, k, r).start(priority=k % 2)
            return c

        lax.fori_loop(0, tm, issue, 0, unroll=4)

    def wait_gather(s):
        for k in range(TOP_K):
            pltpu.make_async_copy(y_hbm.at[pl.ds(0, tm), :], ybuf.at[s, k], sem.at[s]).wait()

    @pl.when(i == 0)
    def _():
        start_gather(0, 0)

    @pl.when(i + 1 < n)
    def _():
        start_gather(i + 1, (i + 1) % 2)

    s = i % 2
    wait_gather(s)
    gt = gt_ref[...]
    a_lo, a_hi = _unpack_bf16_pair(ybuf[s, 0])
    b_lo, b_hi = _unpack_bf16_pair(ybuf[s, 1])
    g0, g1 = gt[:, 0:1], gt[:, 1:2]
    ffn = jnp.concatenate([g0 * a_lo + g1 * b_lo, g0 * a_hi + g1 * b_hi], axis=1)
    y = _ln_rows(alpha * x_ref[...] + ffn, g_ref[...], b_ref[...])
    o_ref[...] = y
    ob_ref[...] = y.astype(BF16)


def _combine_ln(slot, x, y, gates_t, g, b, layer, alpha, *, tm=256):
    t, d = x.shape
    tm = _tile(t, tm)
    row = pl.BlockSpec((tm, d), lambda i, sl: (i, 0))
    vec = pl.BlockSpec((None, 1, d), lambda i, sl: (layer, 0, 0))
    grid_spec = pltpu.PrefetchScalarGridSpec(
        num_scalar_prefetch=1,
        grid=(t // tm,),
        in_specs=[row, pl.BlockSpec(memory_space=pl.ANY),
                  pl.BlockSpec((tm, SUBLANES), lambda i, sl: (i, 0)), vec, vec],
        out_specs=[row, row],
        scratch_shapes=[pltpu.VMEM((2, TOP_K, tm, d // 2), jnp.uint32), pltpu.SemaphoreType.DMA((2,))],
    )
    return pl.pallas_call(
        functools.partial(_combine_kernel, alpha=alpha, tm=tm),
        grid_spec=grid_spec,
        out_shape=[jax.ShapeDtypeStruct((t, d), F32), jax.ShapeDtypeStruct((t, d), BF16)],
        compiler_params=_params(("arbitrary",)),
        name="combine_ln",
    )(slot, x, y, gates_t, g, b)


def _rope_tables(seq):
    axis_dim = HEAD_DIM // 2
    rows = seq // GRID_W
    inv = ROPE_THETA ** (-jnp.arange(0, axis_dim, 2, dtype=F32) / axis_dim)
    ang_r = jnp.repeat(jnp.arange(rows, dtype=F32), GRID_W)[:, None] * inv[None, :]
    ang_c = jnp.tile(jnp.arange(GRID_W, dtype=F32), rows)[:, None] * inv[None, :]
    cr, sr, cc, sc = jnp.cos(ang_r), jnp.sin(ang_r), jnp.cos(ang_c), jnp.sin(ang_c)
    cos_t = jnp.concatenate([cr, cr, cc, cc], axis=1)
    sin_t = jnp.concatenate([-sr, sr, -sc, sc], axis=1)
    return cos_t, sin_t


def _dispatch(idx, n_exp, tm):
    t = idx.shape[1]
    n_asg = t * TOP_K
    e_flat = idx.T.reshape(-1)
    onehot = (e_flat[:, None] == jnp.arange(n_exp, dtype=jnp.int32)[None, :]).astype(jnp.int32)
    csum = jnp.cumsum(onehot, axis=0)
    rank = jnp.sum(csum * onehot, axis=1) - 1
    counts = csum[-1]
    pcounts = (counts + tm - 1) // tm * tm
    pends = jnp.cumsum(pcounts)
    pstarts = pends - pcounts
    slot = (pstarts[e_flat] + rank).astype(jnp.int32)
    nblk = n_asg // tm + n_exp
    row_token = jnp.zeros((nblk * tm,), jnp.int32).at[slot].set(jnp.arange(n_asg, dtype=jnp.int32) // TOP_K)
    block_start = jnp.arange(nblk, dtype=jnp.int32) * tm
    block_e = jnp.minimum(jnp.sum(pends[None, :] <= block_start[:, None], axis=-1), n_exp - 1).astype(jnp.int32)
    n_used = (pends[-1] // tm).astype(jnp.int32).reshape(1)
    return slot, row_token, block_e, n_used


def kernel(x, w_in, q_norm_g, k_norm_g, conv_w, w_attn_proj, w_conv_proj, w_out, ln1_g, ln1_b,
           w_router, router_bias, w_gate, w_up, w_down, ln2_g, ln2_b):
    batch, seq, d = x.shape
    depth = w_in.shape[0]
    t = batch * seq
    attn_w = d // 2
    n_q = attn_w // HEAD_DIM
    n_kv = n_q // GQA_RATIO
    kv_w = n_kv * HEAD_DIM
    conv_wd = d // 2
    n_exp = w_router.shape[1]
    alpha = (2 * depth) ** 0.25
    q_end = attn_w
    k_end = q_end + kv_w
    v_end = k_end + kv_w
    cb_end = v_end + conv_wd
    cc_end = cb_end + conv_wd
    ch_end = cc_end + conv_wd
    ga_end = ch_end + d
    expert_tm = min(512, t)

    cos_t, sin_t = _rope_tables(seq)
    scale = HEAD_DIM ** -0.5 * math.log2(math.e)
    w_router_t = w_router.T
    bias_col = router_bias.reshape(n_exp, 1).astype(F32)
    ln1_g3, ln1_b3 = ln1_g.reshape(depth, 1, d), ln1_b.reshape(depth, 1, d)
    ln2_g3, ln2_b3 = ln2_g.reshape(depth, 1, d), ln2_b.reshape(depth, 1, d)

    xf = x.reshape(t, d)
    xb = xf.astype(BF16)
    for l in range(depth):
        p = _matmul(xb, w_in, l, BF16, name="in_proj")
        gains = jnp.concatenate([jnp.tile(q_norm_g[l] * scale, n_q), jnp.tile(k_norm_g[l], n_kv)]).reshape(1, k_end)
        qk = _qk_prep(p, gains.astype(F32), cos_t, sin_t, seq, k_end)
        attn = _attention(qk, p, batch, seq, n_kv, q_end // HEAD_DIM, k_end // HEAD_DIM)
        conv = _short_conv(p, conv_w, l, seq, v_end, cb_end, cc_end, conv_wd)
        merged = _merge(attn, conv, w_attn_proj, w_conv_proj, p, l, ch_end, ga_end)
        h1 = _matmul(merged, w_out, l, F32, res=xf, alpha=alpha, tm=1024, tn=512, name="out_proj")
        x1, x1p, idx, gate = _ln_route(h1, ln1_g3, ln1_b3, l, w_router_t, bias_col)
        slot, row_token, block_e, n_used = _dispatch(idx[:TOP_K], n_exp, expert_tm)
        xs = _gather_rows(n_used, row_token, x1p, expert_tm)
        y = _experts(block_e, n_used, xs, w_gate, w_up, w_down, l, expert_tm)
        xf, xb = _combine_ln(slot, x1, y, gate.T, ln2_g3, ln2_b3, l, alpha)
    return xf.reshape(batch, seq, d)
```

```python
import functools
import math

import jax
import jax.numpy as jnp
from jax import lax
from jax.experimental import pallas as pl
from jax.experimental.pallas import tpu as pltpu

HEAD_DIM = 128
GQA_RATIO = 4
CONV_K = 3
GRID_W = 64
ROPE_THETA = 10000.0
N_GROUPS = 4
TOP_K = 2
RMS_EPS = 1e-6
LN_EPS = 1e-5
LANES = 128
SUBLANES = 8
ONES_ROWS = 2 * SUBLANES
ISSUE_UNROLL = 8
VMEM_LIMIT = 56 * 1024 * 1024

F32 = jnp.float32
BF16 = jnp.bfloat16


def _params(semantics):
    return pltpu.CompilerParams(dimension_semantics=semantics, vmem_limit_bytes=VMEM_LIMIT)


def _tile(dim, pref):
    t = min(dim, pref)
    while dim % t:
        t //= 2
    return t


def _pack_bf16_pair(lo, hi):
    lo_b = lax.bitcast_convert_type(lo.astype(BF16).astype(F32), jnp.uint32) >> 16
    hi_b = lax.bitcast_convert_type(hi.astype(BF16).astype(F32), jnp.uint32) & jnp.uint32(0xFFFF0000)
    return hi_b | lo_b


def _unpack_bf16_pair(w):
    lo = lax.bitcast_convert_type(w << 16, F32)
    hi = lax.bitcast_convert_type(w & jnp.uint32(0xFFFF0000), F32)
    return lo, hi


def _mm_kernel(x_ref, w_ref, o_ref, wb_ref):
    @pl.when(pl.program_id(1) == 0)
    def _():
        wb_ref[...] = w_ref[...].astype(BF16)

    o_ref[...] = jnp.dot(x_ref[...], wb_ref[...], preferred_element_type=F32).astype(o_ref.dtype)


def _mm_res_kernel(x_ref, w_ref, r_ref, o_ref, wb_ref, *, alpha):
    @pl.when(pl.program_id(1) == 0)
    def _():
        wb_ref[...] = w_ref[...].astype(BF16)

    acc = jnp.dot(x_ref[...], wb_ref[...], preferred_element_type=F32)
    o_ref[...] = (alpha * r_ref[...] + acc).astype(o_ref.dtype)


def _matmul(x, w, layer, out_dtype, *, res=None, alpha=1.0, tm=512, tn=1024, name="matmul"):
    m, k = x.shape
    n = w.shape[2]
    tm, tn = _tile(m, tm), _tile(n, tn)
    in_specs = [
        pl.BlockSpec((tm, k), lambda j, i: (i, 0)),
        pl.BlockSpec((None, k, tn), lambda j, i: (layer, 0, j)),
    ]
    args = [x, w]
    body = _mm_kernel
    if res is not None:
        in_specs.append(pl.BlockSpec((tm, tn), lambda j, i: (i, j)))
        args.append(res)
        body = functools.partial(_mm_res_kernel, alpha=alpha)
    return pl.pallas_call(
        body,
        grid=(n // tn, m // tm),
        in_specs=in_specs,
        out_specs=pl.BlockSpec((tm, tn), lambda j, i: (i, j)),
        out_shape=jax.ShapeDtypeStruct((m, n), out_dtype),
        scratch_shapes=[pltpu.VMEM((k, tn), BF16)],
        compiler_params=_params(("parallel", "arbitrary")),
        name=name,
    )(*args)


def _qk_prep_kernel(p_ref, g_ref, c_ref, s_ref, ones_ref, perm_ref, o_ref, *, heads):
    cos = c_ref[...]
    sin = s_ref[...]
    ones = ones_ref[...]
    perm = perm_ref[...]
    for h in range(heads):
        sl = slice(h * HEAD_DIM, (h + 1) * HEAD_DIM)
        xh = p_ref[:, sl].astype(F32)
        ssq = jnp.dot((xh * xh).astype(BF16), ones, preferred_element_type=F32)
        y = xh * lax.rsqrt(ssq * (1.0 / HEAD_DIM) + RMS_EPS) * g_ref[:, sl]
        swapped = jnp.dot(y.astype(BF16), perm, preferred_element_type=F32)
        o_ref[:, sl] = (y * cos + swapped * sin).astype(o_ref.dtype)


def _qk_prep(p, gains, cos_t, sin_t, seq, qk_width, *, tm=512, heads=32):
    t = p.shape[0]
    tm = _tile(seq, tm)
    n_heads = qk_width // HEAD_DIM
    heads = _tile(n_heads, heads)
    wblk = heads * HEAD_DIM
    sblocks = seq // tm
    ones = jnp.ones((HEAD_DIM, HEAD_DIM), BF16)
    lane = jnp.arange(HEAD_DIM)
    partner = jnp.where((lane % 64) < 32, lane + 32, lane - 32)
    perm = (lane[:, None] == partner[None, :]).astype(BF16)
    return pl.pallas_call(
        functools.partial(_qk_prep_kernel, heads=heads),
        grid=(t // tm, n_heads // heads),
        in_specs=[
            pl.BlockSpec((tm, wblk), lambda i, j: (i, j)),
            pl.BlockSpec((1, wblk), lambda i, j: (0, j)),
            pl.BlockSpec((tm, HEAD_DIM), lambda i, j: (i % sblocks, 0)),
            pl.BlockSpec((tm, HEAD_DIM), lambda i, j: (i % sblocks, 0)),
            pl.BlockSpec((HEAD_DIM, HEAD_DIM), lambda i, j: (0, 0)),
            pl.BlockSpec((HEAD_DIM, HEAD_DIM), lambda i, j: (0, 0)),
        ],
        out_specs=pl.BlockSpec((tm, wblk), lambda i, j: (i, j)),
        out_shape=jax.ShapeDtypeStruct((t, qk_width), BF16),
        compiler_params=_params(("parallel", "parallel")),
        name="qk_prep",
    )(p, gains, cos_t, sin_t, ones, perm)


def _flash_kernel(q_ref, k_ref, v_ref, o_ref, qt_ref, vt_ref, *stat_refs, tq, tk, cb, seq, unroll, ahead):
    nkv = seq // tk
    ncb = GQA_RATIO * tq // cb
    acc_refs, m_refs = stat_refs[:ncb], stat_refs[ncb:]

    @pl.when(pl.program_id(2) == 0)
    def _():
        for c in range(nkv):
            vt_ref[c, :HEAD_DIM, :] = v_ref[c * tk:(c + 1) * tk, :].T
            vt_ref[c, HEAD_DIM:, :] = jnp.ones((ONES_ROWS, tk), BF16)

    for g in range(GQA_RATIO):
        qt_ref[:, g * tq:(g + 1) * tq] = q_ref[:, g * HEAD_DIM:(g + 1) * HEAD_DIM].T
    for c in range(ncb):
        m_refs[c][...] = jnp.full(m_refs[c].shape, -jnp.inf, F32)
        acc_refs[c][...] = jnp.zeros(acc_refs[c].shape, F32)

    def body(j, carry):
        tiles = [(u, c) for u in range(unroll) for c in range(ncb)]
        ks, vts = [], []
        for u in range(unroll):
            off = pl.multiple_of((j * unroll + u) * tk, tk)
            ks.append(k_ref[pl.ds(off, tk), :])
            vts.append(vt_ref[j * unroll + u])

        def scores(t):
            u, c = tiles[t]
            return jnp.dot(ks[u], qt_ref[:, c * cb:(c + 1) * cb], preferred_element_type=F32)

        pending = [scores(t) for t in range(min(ahead, len(tiles)))]
        for t, (u, c) in enumerate(tiles):
            st = pending.pop(0)
            if t + ahead < len(tiles):
                pending.append(scores(t + ahead))
            m_old = m_refs[c][...]
            m_new = jnp.maximum(m_old, jnp.max(st, axis=0, keepdims=True))
            alpha = jnp.exp2(m_old - m_new)
            pt = jnp.exp2(st - m_new).astype(BF16)
            m_refs[c][...] = m_new
            acc_refs[c][...] = alpha * acc_refs[c][...] + jnp.dot(vts[u], pt, preferred_element_type=F32)
        return carry

    lax.fori_loop(0, nkv // unroll, body, 0)
    per_g = tq // cb
    for c in range(ncb):
        g, r = divmod(c, per_g)
        acc = acc_refs[c][...]
        out_t = acc[:HEAD_DIM, :] / acc[HEAD_DIM:HEAD_DIM + 1, :]
        o_ref[r * cb:(r + 1) * cb, g * HEAD_DIM:(g + 1) * HEAD_DIM] = out_t.T.astype(o_ref.dtype)


def _attention(qk, p, batch, seq, n_kv, k_col0, v_col0, *, tq=512, tk=256, cb=256, unroll=8, ahead=5):
    tq, tk = _tile(seq, tq), _tile(seq, tk)
    rows = GQA_RATIO * tq
    cb = _tile(rows, cb)
    unroll = _tile(seq // tk, unroll)
    qk3 = qk.reshape(batch, seq, qk.shape[1])
    p3 = p.reshape(batch, seq, p.shape[1])
    gw = GQA_RATIO * HEAD_DIM
    out = pl.pallas_call(
        functools.partial(_flash_kernel, tq=tq, tk=tk, cb=cb, seq=seq, unroll=unroll, ahead=ahead),
        grid=(batch, n_kv, seq // tq),
        in_specs=[
            pl.BlockSpec((None, tq, gw), lambda b, h, i: (b, i, h)),
            pl.BlockSpec((None, seq, HEAD_DIM), lambda b, h, i: (b, 0, k_col0 + h)),
            pl.BlockSpec((None, seq, HEAD_DIM), lambda b, h, i: (b, 0, v_col0 + h)),
        ],
        out_specs=pl.BlockSpec((None, tq, gw), lambda b, h, i: (b, i, h)),
        out_shape=jax.ShapeDtypeStruct((batch, seq, n_kv * gw), BF16),
        scratch_shapes=[
            pltpu.VMEM((HEAD_DIM, rows), BF16),
            pltpu.VMEM((seq // tk, HEAD_DIM + ONES_ROWS, tk), BF16),
        ] + [pltpu.VMEM((HEAD_DIM + ONES_ROWS, cb), F32)] * (rows // cb)
        + [pltpu.VMEM((1, cb), F32)] * (rows // cb),
        compiler_params=_params(("parallel", "parallel", "arbitrary")),
        name="flash_gqa",
    )(qk3, qk3, p3)
    return out.reshape(batch * seq, n_kv * gw)


def _conv_kernel(cb_ref, cc_ref, ch_ref, cbp_ref, chp_ref, cbn_ref, chn_ref, w_ref, o_ref, *, tm, sblocks):
    i = pl.program_id(0)
    first = (i % sblocks) == 0
    last = (i % sblocks) == sblocks - 1
    bx = cb_ref[...].astype(F32) * ch_ref[...].astype(F32)
    prev_row = cbp_ref[SUBLANES - 1:SUBLANES, :].astype(F32) * chp_ref[SUBLANES - 1:SUBLANES, :].astype(F32)
    next_row = cbn_ref[0:1, :].astype(F32) * chn_ref[0:1, :].astype(F32)
    prev_row = jnp.where(first, 0.0, prev_row)
    next_row = jnp.where(last, 0.0, next_row)
    row = lax.broadcasted_iota(jnp.int32, bx.shape, 0)
    prev = jnp.where(row == 0, prev_row, pltpu.roll(bx, 1, axis=0))
    nxt = jnp.where(row == tm - 1, next_row, pltpu.roll(bx, tm - 1, axis=0))
    w = w_ref[...]
    y = w[0:1, :] * prev + w[1:2, :] * bx + w[2:3, :] * nxt
    o_ref[...] = (cc_ref[...].astype(F32) * y).astype(o_ref.dtype)


def _short_conv(p, conv_w, layer, seq, cb0, cc0, ch0, width, *, tm=512, tc=1024):
    t = p.shape[0]
    tm, tc = _tile(seq, tm), math.gcd(cb0, cc0, ch0, width, tc)
    sblocks = seq // tm
    rpb = tm // SUBLANES
    nrb = t // SUBLANES

    def main(c0):
        return pl.BlockSpec((tm, tc), lambda i, j: (i, c0 // tc + j))

    def halo_prev(c0):
        return pl.BlockSpec((SUBLANES, tc), lambda i, j: (jnp.maximum(i * rpb - 1, 0), c0 // tc + j))

    def halo_next(c0):
        return pl.BlockSpec((SUBLANES, tc), lambda i, j: (jnp.minimum((i + 1) * rpb, nrb - 1), c0 // tc + j))

    return pl.pallas_call(
        functools.partial(_conv_kernel, tm=tm, sblocks=sblocks),
        grid=(t // tm, width // tc),
        in_specs=[main(cb0), main(cc0), main(ch0), halo_prev(cb0), halo_prev(ch0),
                  halo_next(cb0), halo_next(ch0),
                  pl.BlockSpec((None, CONV_K, tc), lambda i, j: (layer, 0, j))],
        out_specs=pl.BlockSpec((tm, tc), lambda i, j: (i, j)),
        out_shape=jax.ShapeDtypeStruct((t, width), BF16),
        compiler_params=_params(("parallel", "parallel")),
        name="short_conv",
    )(p, p, p, p, p, p, p, conv_w)


def _merge_kernel(a_ref, c_ref, wa_ref, wc_ref, ga_ref, gc_ref, o_ref, wab_ref, wcb_ref):
    @pl.when(pl.program_id(1) == 0)
    def _():
        wab_ref[...] = wa_ref[...].astype(BF16)
        wcb_ref[...] = wc_ref[...].astype(BF16)

    ya = jnp.dot(a_ref[...], wab_ref[...], preferred_element_type=F32)
    yc = jnp.dot(c_ref[...], wcb_ref[...], preferred_element_type=F32)
    o_ref[...] = (jax.nn.sigmoid(ga_ref[...].astype(F32)) * ya
                  + jax.nn.sigmoid(gc_ref[...].astype(F32)) * yc).astype(o_ref.dtype)


def _merge(attn, conv, wa, wc, p, layer, ga0, gc0, *, tm=1024, tn=512):
    t, ka = attn.shape
    kc = conv.shape[1]
    d = wa.shape[2]
    tm, tn = _tile(t, tm), math.gcd(ga0, gc0, d, tn)
    return pl.pallas_call(
        _merge_kernel,
        grid=(d // tn, t // tm),
        in_specs=[
            pl.BlockSpec((tm, ka), lambda j, i: (i, 0)),
            pl.BlockSpec((tm, kc), lambda j, i: (i, 0)),
            pl.BlockSpec((None, ka, tn), lambda j, i: (layer, 0, j)),
            pl.BlockSpec((None, kc, tn), lambda j, i: (layer, 0, j)),
            pl.BlockSpec((tm, tn), lambda j, i: (i, ga0 // tn + j)),
            pl.BlockSpec((tm, tn), lambda j, i: (i, gc0 // tn + j)),
        ],
        out_specs=pl.BlockSpec((tm, tn), lambda j, i: (i, j)),
        out_shape=jax.ShapeDtypeStruct((t, d), BF16),
        scratch_shapes=[pltpu.VMEM((ka, tn), BF16), pltpu.VMEM((kc, tn), BF16)],
        compiler_params=_params(("parallel", "arbitrary")),
        name="gated_merge",
    )(attn, conv, wa, wc, p, p)


def _ln_rows(h, g, b):
    mu = jnp.mean(h, axis=-1, keepdims=True)
    hc = h - mu
    var = jnp.mean(hc * hc, axis=-1, keepdims=True)
    return hc * lax.rsqrt(var + LN_EPS) * g + b


def _route(logits, bias, n_exp):
    epg = n_exp // N_GROUPS
    mx = jnp.max(logits, axis=0, keepdims=True)
    ex = jnp.exp(logits - mx)
    scores = ex / jnp.sum(ex, axis=0, keepdims=True)
    sel = scores + bias
    rows_sel = [sel[e:e + 1, :] for e in range(n_exp)]
    rows_sc = [scores[e:e + 1, :] for e in range(n_exp)]
    best = None
    for g in range(N_GROUPS):
        mem = rows_sel[g * epg:(g + 1) * epg]
        gs = None
        for a in range(epg):
            for c in range(a + 1, epg):
                pair = mem[a] + mem[c]
                gs = pair if gs is None else jnp.maximum(gs, pair)
        if best is None:
            best, grp = gs, jnp.zeros(gs.shape, jnp.int32)
        else:
            upd = gs > best
            best = jnp.where(upd, gs, best)
            grp = jnp.where(upd, g, grp)
    cand_sel, cand_sc = [], []
    for j in range(epg):
        cs, cc = rows_sel[j], rows_sc[j]
        for g in range(1, N_GROUPS):
            cs = jnp.where(grp == g, rows_sel[g * epg + j], cs)
            cc = jnp.where(grp == g, rows_sc[g * epg + j], cc)
        cand_sel.append(cs)
        cand_sc.append(cc)

    def first_argmax(vals, skip):
        bv = bi = bs = None
        for j in range(epg):
            v = vals[j] if skip is None else jnp.where(skip == j, -jnp.inf, vals[j])
            if bv is None:
                bv, bi, bs = v, jnp.zeros(v.shape, jnp.int32), cand_sc[0]
            else:
                upd = v > bv
                bv = jnp.where(upd, v, bv)
                bi = jnp.where(upd, j, bi)
                bs = jnp.where(upd, cand_sc[j], bs)
        return bi, bs

    i1, s1 = first_argmax(cand_sel, None)
    i2, s2 = first_argmax(cand_sel, i1)
    tot = s1 + s2
    zi = jnp.zeros((SUBLANES - TOP_K,) + i1.shape[1:], jnp.int32)
    zf = jnp.zeros((SUBLANES - TOP_K,) + i1.shape[1:], F32)
    idx = jnp.concatenate([grp * epg + i1, grp * epg + i2, zi], axis=0)
    gate = jnp.concatenate([s1 / tot, s2 / tot, zf], axis=0)
    return idx, gate


def _ln_route_kernel(h_ref, g_ref, b_ref, wh_ref, wl_ref, rb_ref, o_ref, op_ref, idx_ref, gate_ref, *, n_exp):
    y = _ln_rows(h_ref[...], g_ref[...], b_ref[...])
    half = y.shape[1] // 2
    o_ref[...] = y
    op_ref[...] = _pack_bf16_pair(y[:, :half], y[:, half:])
    yh = y.astype(BF16)
    yl = (y - yh.astype(F32)).astype(BF16)
    nt = (((1,), (1,)), ((), ()))
    logits = (lax.dot_general(wh_ref[...], yh, nt, preferred_element_type=F32)
              + lax.dot_general(wh_ref[...], yl, nt, preferred_element_type=F32)
              + lax.dot_general(wl_ref[...], yh, nt, preferred_element_type=F32))
    idx_ref[...], gate_ref[...] = _route(logits, rb_ref[...], n_exp)


def _ln_route(h, g, b, layer, w_router_t, bias_col, *, tm=256):
    t, d = h.shape
    n_exp = w_router_t.shape[0]
    tm = _tile(t, tm)
    wh = w_router_t.astype(BF16)
    wl = (w_router_t - wh.astype(F32)).astype(BF16)
    row = pl.BlockSpec((tm, d), lambda i: (i, 0))
    vec = pl.BlockSpec((None, 1, d), lambda i: (layer, 0, 0))
    wsp = pl.BlockSpec((n_exp, d), lambda i: (0, 0))
    rsp = pl.BlockSpec((SUBLANES, tm), lambda i: (0, i))
    return pl.pallas_call(
        functools.partial(_ln_route_kernel, n_exp=n_exp),
        grid=(t // tm,),
        in_specs=[row, vec, vec, wsp, wsp, pl.BlockSpec((n_exp, 1), lambda i: (0, 0))],
        out_specs=[row, pl.BlockSpec((tm, d // 2), lambda i: (i, 0)), rsp, rsp],
        out_shape=[jax.ShapeDtypeStruct((t, d), F32), jax.ShapeDtypeStruct((t, d // 2), jnp.uint32),
                   jax.ShapeDtypeStruct((SUBLANES, t), jnp.int32), jax.ShapeDtypeStruct((SUBLANES, t), F32)],
        compiler_params=_params(("parallel",)),
        name="ln_route",
    )(h, g, b, wh, wl, bias_col)


def _dispatch_kernel(slot_ref, cnt_ref, pst_ref, nu_ref, x_hbm, z_hbm, o_hbm, sem, *, tm, n_exp, epp, nblk):
    i = pl.program_id(0)
    n = pl.num_programs(0)
    step_rows = tm * TOP_K

    def issue(g, c):
        for u in range(ISSUE_UNROLL):
            t = i * tm + g * ISSUE_UNROLL + u
            for k in range(TOP_K):
                pltpu.make_async_copy(x_hbm.at[pl.ds(t, 1), :], o_hbm.at[pl.ds(slot_ref[t * TOP_K + k], 1), :],
                                      sem.at[0]).start(priority=k % 2)
        return c

    lax.fori_loop(0, tm // ISSUE_UNROLL, issue, 0)

    def wait_rows(rows):
        pltpu.make_async_copy(x_hbm.at[pl.ds(0, rows), :], o_hbm.at[pl.ds(0, rows), :], sem.at[0]).wait()

    for j in range(epp):
        e = i * epp + j

        @pl.when(e < n_exp)
        def _():
            cnt = cnt_ref[e]
            first = pst_ref[e] + cnt
            npad = (tm - cnt % tm) % tm

            def zero(r, c):
                pltpu.make_async_copy(z_hbm.at[pl.ds(0, 1), :], o_hbm.at[pl.ds(first + r, 1), :], sem.at[0]).start()
                return c

            def wait_one(r, c):
                wait_rows(1)
                return c

            lax.fori_loop(0, npad, zero, 0)
            lax.fori_loop(0, npad, wait_one, 0)

    @pl.when(i > 0)
    def _():
        wait_rows(step_rows)

    @pl.when(i == n - 1)
    def _():
        wait_rows(step_rows)

        def zero_block(b, c):
            cp = pltpu.make_async_copy(z_hbm, o_hbm.at[pl.ds(pl.multiple_of(b * tm, tm), tm), :], sem.at[0])
            cp.start()
            cp.wait()
            return c

        lax.fori_loop(nu_ref[0], nblk, zero_block, 0)


def _dispatch_rows(slot, counts, pstarts, n_used, xp, n_rows, tm):
    t, w = xp.shape
    n_exp = counts.shape[0]
    assert t % tm == 0 and n_rows % tm == 0
    n_steps = t // tm
    epp = -(-n_exp // n_steps)
    zeros = jnp.zeros((tm, w), xp.dtype)
    grid_spec = pltpu.PrefetchScalarGridSpec(
        num_scalar_prefetch=4,
        grid=(n_steps,),
        in_specs=[pl.BlockSpec(memory_space=pl.ANY), pl.BlockSpec(memory_space=pl.ANY)],
        out_specs=pl.BlockSpec(memory_space=pl.ANY),
        scratch_shapes=[pltpu.SemaphoreType.DMA((1,))],
    )
    return pl.pallas_call(
        functools.partial(_dispatch_kernel, tm=tm, n_exp=n_exp, epp=epp, nblk=n_rows // tm),
        grid_spec=grid_spec,
        out_shape=jax.ShapeDtypeStruct((n_rows, w), xp.dtype),
        compiler_params=_params(("arbitrary",)),
        name="dispatch_rows",
    )(slot, counts, pstarts, n_used, xp, zeros)


def _new_expert(be_ref, i):
    return jnp.logical_or(i == 0, be_ref[i] != be_ref[jnp.maximum(i - 1, 0)])


def _expert_up_kernel(be_ref, nu_ref, x_ref, wg_ref, wu_ref, o_ref, wgb_ref, wub_ref):
    i = pl.program_id(1)

    @pl.when(_new_expert(be_ref, i))
    def _():
        wgb_ref[...] = wg_ref[...].astype(BF16)
        wub_ref[...] = wu_ref[...].astype(BF16)

    @pl.when(i < nu_ref[0])
    def _():
        lo, hi = _unpack_bf16_pair(x_ref[...])
        lo, hi = lo.astype(BF16), hi.astype(BF16)
        half = lo.shape[1]
        hg = (jnp.dot(lo, wgb_ref[:half, :], preferred_element_type=F32)
              + jnp.dot(hi, wgb_ref[half:, :], preferred_element_type=F32))
        hu = (jnp.dot(lo, wub_ref[:half, :], preferred_element_type=F32)
              + jnp.dot(hi, wub_ref[half:, :], preferred_element_type=F32))
        o_ref[...] = (hg * jax.nn.sigmoid(hg) * hu).astype(o_ref.dtype)

    @pl.when(i >= nu_ref[0])
    def _():
        o_ref[...] = jnp.zeros_like(o_ref)


def _expert_down_kernel(be_ref, nu_ref, h_ref, wlo_ref, whi_ref, o_ref, wlob_ref, whib_ref):
    i = pl.program_id(1)

    @pl.when(_new_expert(be_ref, i))
    def _():
        wlob_ref[...] = wlo_ref[...].astype(BF16)
        whib_ref[...] = whi_ref[...].astype(BF16)

    @pl.when(i < nu_ref[0])
    def _():
        h = h_ref[...]
        o_ref[...] = _pack_bf16_pair(jnp.dot(h, wlob_ref[...], preferred_element_type=F32),
                                     jnp.dot(h, whib_ref[...], preferred_element_type=F32))

    @pl.when(i >= nu_ref[0])
    def _():
        o_ref[...] = jnp.zeros_like(o_ref)


def _experts(block_e, n_used, xs, wg, wu, wd, layer, tm, *, tf=512, tn=1024):
    r = xs.shape[0]
    d = 2 * xs.shape[1]
    ff = wg.shape[3]
    tf = _tile(ff, tf)
    nblk = r // tm
    hid = pl.pallas_call(
        _expert_up_kernel,
        grid_spec=pltpu.PrefetchScalarGridSpec(
            num_scalar_prefetch=2,
            grid=(ff // tf, nblk),
            in_specs=[
                pl.BlockSpec((tm, d // 2), lambda f, i, be, nu: (jnp.minimum(i, nu[0] - 1), 0)),
                pl.BlockSpec((None, None, d, tf), lambda f, i, be, nu: (layer, be[i], 0, f)),
                pl.BlockSpec((None, None, d, tf), lambda f, i, be, nu: (layer, be[i], 0, f)),
            ],
            out_specs=pl.BlockSpec((tm, tf), lambda f, i, be, nu: (i, f)),
            scratch_shapes=[pltpu.VMEM((d, tf), BF16), pltpu.VMEM((d, tf), BF16)],
        ),
        out_shape=jax.ShapeDtypeStruct((r, ff), BF16),
        compiler_params=_params(("arbitrary", "arbitrary")),
        name="expert_up",
    )(block_e, n_used, xs, wg, wu)
    half = d // 2
    tn = _tile(half, tn)
    hi0 = half // tn
    return pl.pallas_call(
        _expert_down_kernel,
        grid_spec=pltpu.PrefetchScalarGridSpec(
            num_scalar_prefetch=2,
            grid=(half // tn, nblk),
            in_specs=[
                pl.BlockSpec((tm, ff), lambda n, i, be, nu: (i, 0)),
                pl.BlockSpec((None, None, ff, tn), lambda n, i, be, nu: (layer, be[i], 0, n)),
                pl.BlockSpec((None, None, ff, tn), lambda n, i, be, nu: (layer, be[i], 0, hi0 + n)),
            ],
            out_specs=pl.BlockSpec((tm, tn), lambda n, i, be, nu: (i, n)),
            scratch_shapes=[pltpu.VMEM((ff, tn), BF16), pltpu.VMEM((ff, tn), BF16)],
        ),
        out_shape=jax.ShapeDtypeStruct((r, half), jnp.uint32),
        compiler_params=_params(("arbitrary", "arbitrary")),
        name="expert_down",
    )(block_e, n_used, hid, wd, wd)


def _combine_kernel(slot_ref, x_ref, y_hbm, gt_ref, g_ref, b_ref, o_ref, ob_ref, ybuf, sem, *, alpha, tm):
    i = pl.program_id(0)
    n = pl.num_programs(0)

    def row_copy(row, s, k, r):
        return pltpu.make_async_copy(y_hbm.at[pl.ds(row, 1), :], ybuf.at[s, k, pl.ds(r, 1), :], sem.at[s])

    def start_gather(step, s):
        base = step * (tm * TOP_K)

        def issue(r, c):
            for k in range(TOP_K):
                row_copy(slot_ref[base + r * TOP_K + k], s, k, r).start(priority=k % 2)
            return c

        lax.fori_loop(0, tm, issue, 0, unroll=4)

    def wait_gather(s):
        for k in range(TOP_K):
            pltpu.make_async_copy(y_hbm.at[pl.ds(0, tm), :], ybuf.at[s, k], sem.at[s]).wait()

    @pl.when(i == 0)
    def _():
        start_gather(0, 0)

    @pl.when(i + 1 < n)
    def _():
        start_gather(i + 1, (i + 1) % 2)

    s = i % 2
    wait_gather(s)
    gt = gt_ref[...]
    a_lo, a_hi = _unpack_bf16_pair(ybuf[s, 0])
    b_lo, b_hi = _unpack_bf16_pair(ybuf[s, 1])
    g0, g1 = gt[:, 0:1], gt[:, 1:2]
    ffn = jnp.concatenate([g0 * a_lo + g1 * b_lo, g0 * a_hi + g1 * b_hi], axis=1)
    y = _ln_rows(alpha * x_ref[...] + ffn, g_ref[...], b_ref[...])
    o_ref[...] = y
    ob_ref[...] = y.astype(BF16)


def _combine_ln(slot, x, y, gates_t, g, b, layer, alpha, *, tm=256):
    t, d = x.shape
    tm = _tile(t, tm)
    row = pl.BlockSpec((tm, d), lambda i, sl: (i, 0))
    vec = pl.BlockSpec((None, 1, d), lambda i, sl: (layer, 0, 0))
    grid_spec = pltpu.PrefetchScalarGridSpec(
        num_scalar_prefetch=1,
        grid=(t // tm,),
        in_specs=[row, pl.BlockSpec(memory_space=pl.ANY),
                  pl.BlockSpec((tm, SUBLANES), lambda i, sl: (i, 0)), vec, vec],
        out_specs=[row, row],
        scratch_shapes=[pltpu.VMEM((2, TOP_K, tm, d // 2), jnp.uint32), pltpu.SemaphoreType.DMA((2,))],
    )
    return pl.pallas_call(
        functools.partial(_combine_kernel, alpha=alpha, tm=tm),
        grid_spec=grid_spec,
        out_shape=[jax.ShapeDtypeStruct((t, d), F32), jax.ShapeDtypeStruct((t, d), BF16)],
        compiler_params=_params(("arbitrary",)),
        name="combine_ln",
    )(slot, x, y, gates_t, g, b)


def _rope_tables(seq):
    axis_dim = HEAD_DIM // 2
    rows = seq // GRID_W
    inv = ROPE_THETA ** (-jnp.arange(0, axis_dim, 2, dtype=F32) / axis_dim)
    ang_r = jnp.repeat(jnp.arange(rows, dtype=F32), GRID_W)[:, None] * inv[None, :]
    ang_c = jnp.tile(jnp.arange(GRID_W, dtype=F32), rows)[:, None] * inv[None, :]
    cr, sr, cc, sc = jnp.cos(ang_r), jnp.sin(ang_r), jnp.cos(ang_c), jnp.sin(ang_c)
    cos_t = jnp.concatenate([cr, cr, cc, cc], axis=1)
    sin_t = jnp.concatenate([-sr, sr, -sc, sc], axis=1)
    return cos_t, sin_t


def _dispatch(idx, n_exp, tm):
    t = idx.shape[1]
    n_asg = t * TOP_K
    e_flat = idx.T.reshape(-1)
    onehot = (e_flat[:, None] == jnp.arange(n_exp, dtype=jnp.int32)[None, :]).astype(jnp.int32)
    csum = jnp.cumsum(onehot, axis=0)
    rank = jnp.sum(csum * onehot, axis=1) - 1
    counts = csum[-1]
    pcounts = (counts + tm - 1) // tm * tm
    pends = jnp.cumsum(pcounts)
    pstarts = pends - pcounts
    slot = (pstarts[e_flat] + rank).astype(jnp.int32)
    nblk = n_asg // tm + n_exp
    block_start = jnp.arange(nblk, dtype=jnp.int32) * tm
    block_e = jnp.minimum(jnp.sum(pends[None, :] <= block_start[:, None], axis=-1), n_exp - 1).astype(jnp.int32)
    n_used = (pends[-1] // tm).astype(jnp.int32).reshape(1)
    return slot, counts.astype(jnp.int32), pstarts.astype(jnp.int32), block_e, n_used, nblk * tm


def kernel(x, w_in, q_norm_g, k_norm_g, conv_w, w_attn_proj, w_conv_proj, w_out, ln1_g, ln1_b,
           w_router, router_bias, w_gate, w_up, w_down, ln2_g, ln2_b):
    batch, seq, d = x.shape
    depth = w_in.shape[0]
    t = batch * seq
    attn_w = d // 2
    n_q = attn_w // HEAD_DIM
    n_kv = n_q // GQA_RATIO
    kv_w = n_kv * HEAD_DIM
    conv_wd = d // 2
    n_exp = w_router.shape[1]
    alpha = (2 * depth) ** 0.25
    q_end = attn_w
    k_end = q_end + kv_w
    v_end = k_end + kv_w
    cb_end = v_end + conv_wd
    cc_end = cb_end + conv_wd
    ch_end = cc_end + conv_wd
    ga_end = ch_end + d
    expert_tm = min(512, t)

    cos_t, sin_t = _rope_tables(seq)
    scale = HEAD_DIM ** -0.5 * math.log2(math.e)
    w_router_t = w_router.T
    bias_col = router_bias.reshape(n_exp, 1).astype(F32)
    ln1_g3, ln1_b3 = ln1_g.reshape(depth, 1, d), ln1_b.reshape(depth, 1, d)
    ln2_g3, ln2_b3 = ln2_g.reshape(depth, 1, d), ln2_b.reshape(depth, 1, d)

    xf = x.reshape(t, d)
    xb = xf.astype(BF16)
    for l in range(depth):
        p = _matmul(xb, w_in, l, BF16, name="in_proj")
        gains = jnp.concatenate([jnp.tile(q_norm_g[l] * scale, n_q), jnp.tile(k_norm_g[l], n_kv)]).reshape(1, k_end)
        qk = _qk_prep(p, gains.astype(F32), cos_t, sin_t, seq, k_end)
        attn = _attention(qk, p, batch, seq, n_kv, q_end // HEAD_DIM, k_end // HEAD_DIM)
        conv = _short_conv(p, conv_w, l, seq, v_end, cb_end, cc_end, conv_wd)
        merged = _merge(attn, conv, w_attn_proj, w_conv_proj, p, l, ch_end, ga_end)
        h1 = _matmul(merged, w_out, l, F32, res=xf, alpha=alpha, tm=1024, tn=512, name="out_proj")
        x1, x1p, idx, gate = _ln_route(h1, ln1_g3, ln1_b3, l, w_router_t, bias_col)
        slot, counts, pstarts, block_e, n_used, n_rows = _dispatch(idx[:TOP_K], n_exp, expert_tm)
        xs = _dispatch_rows(slot, counts, pstarts, n_used, x1p, n_rows, expert_tm)
        y = _experts(block_e, n_used, xs, w_gate, w_up, w_down, l, expert_tm)
        xf, xb = _combine_ln(slot, x1, y, gate.T, ln2_g3, ln2_b3, l, alpha)
    return xf.reshape(batch, seq, d)
```

```python
import functools
import math

import jax
import jax.numpy as jnp
from jax import lax
from jax.experimental import pallas as pl
from jax.experimental.pallas import tpu as pltpu

HEAD_DIM = 128
GQA_RATIO = 4
CONV_K = 3
GRID_W = 64
ROPE_THETA = 10000.0
N_GROUPS = 4
TOP_K = 2
RMS_EPS = 1e-6
LN_EPS = 1e-5
LANES = 128
SUBLANES = 8
ONES_ROWS = 2 * SUBLANES
ISSUE_UNROLL = 8
VMEM_LIMIT = 56 * 1024 * 1024

F32 = jnp.float32
BF16 = jnp.bfloat16


def _params(semantics):
    return pltpu.CompilerParams(dimension_semantics=semantics, vmem_limit_bytes=VMEM_LIMIT)


def _tile(dim, pref):
    t = min(dim, pref)
    while dim % t:
        t //= 2
    return t


def _pack_bf16_pair(lo, hi):
    lo_b = lax.bitcast_convert_type(lo.astype(BF16).astype(F32), jnp.uint32) >> 16
    hi_b = lax.bitcast_convert_type(hi.astype(BF16).astype(F32), jnp.uint32) & jnp.uint32(0xFFFF0000)
    return hi_b | lo_b


def _unpack_bf16_pair(w):
    lo = lax.bitcast_convert_type(w << 16, F32)
    hi = lax.bitcast_convert_type(w & jnp.uint32(0xFFFF0000), F32)
    return lo, hi


def _mm_kernel(x_ref, w_ref, o_ref, wb_ref):
    @pl.when(pl.program_id(1) == 0)
    def _():
        wb_ref[...] = w_ref[...].astype(BF16)

    o_ref[...] = jnp.dot(x_ref[...], wb_ref[...], preferred_element_type=F32).astype(o_ref.dtype)


def _mm_res_kernel(x_ref, w_ref, r_ref, o_ref, wb_ref, *, alpha):
    @pl.when(pl.program_id(1) == 0)
    def _():
        wb_ref[...] = w_ref[...].astype(BF16)

    acc = jnp.dot(x_ref[...], wb_ref[...], preferred_element_type=F32)
    o_ref[...] = (alpha * r_ref[...] + acc).astype(o_ref.dtype)


def _matmul(x, w, layer, out_dtype, *, res=None, alpha=1.0, tm=512, tn=1024, name="matmul"):
    m, k = x.shape
    n = w.shape[2]
    tm, tn = _tile(m, tm), _tile(n, tn)
    in_specs = [
        pl.BlockSpec((tm, k), lambda j, i: (i, 0)),
        pl.BlockSpec((None, k, tn), lambda j, i: (layer, 0, j)),
    ]
    args = [x, w]
    body = _mm_kernel
    if res is not None:
        in_specs.append(pl.BlockSpec((tm, tn), lambda j, i: (i, j)))
        args.append(res)
        body = functools.partial(_mm_res_kernel, alpha=alpha)
    return pl.pallas_call(
        body,
        grid=(n // tn, m // tm),
        in_specs=in_specs,
        out_specs=pl.BlockSpec((tm, tn), lambda j, i: (i, j)),
        out_shape=jax.ShapeDtypeStruct((m, n), out_dtype),
        scratch_shapes=[pltpu.VMEM((k, tn), BF16)],
        compiler_params=_params(("parallel", "arbitrary")),
        name=name,
    )(*args)


def _qk_prep_kernel(p_ref, g_ref, c_ref, s_ref, ones_ref, perm_ref, o_ref, *, heads):
    cos = c_ref[...]
    sin = s_ref[...]
    ones = ones_ref[...]
    perm = perm_ref[...]
    for h in range(heads):
        sl = slice(h * HEAD_DIM, (h + 1) * HEAD_DIM)
        xh = p_ref[:, sl].astype(F32)
        ssq = jnp.dot((xh * xh).astype(BF16), ones, preferred_element_type=F32)
        y = xh * lax.rsqrt(ssq * (1.0 / HEAD_DIM) + RMS_EPS) * g_ref[:, sl]
        swapped = jnp.dot(y.astype(BF16), perm, preferred_element_type=F32)
        o_ref[:, sl] = (y * cos + swapped * sin).astype(o_ref.dtype)


def _qk_prep(p, gains, cos_t, sin_t, seq, qk_width, *, tm=512, heads=32):
    t = p.shape[0]
    tm = _tile(seq, tm)
    n_heads = qk_width // HEAD_DIM
    heads = _tile(n_heads, heads)
    wblk = heads * HEAD_DIM
    sblocks = seq // tm
    ones = jnp.ones((HEAD_DIM, HEAD_DIM), BF16)
    lane = jnp.arange(HEAD_DIM)
    partner = jnp.where((lane % 64) < 32, lane + 32, lane - 32)
    perm = (lane[:, None] == partner[None, :]).astype(BF16)
    return pl.pallas_call(
        functools.partial(_qk_prep_kernel, heads=heads),
        grid=(t // tm, n_heads // heads),
        in_specs=[
            pl.BlockSpec((tm, wblk), lambda i, j: (i, j)),
            pl.BlockSpec((1, wblk), lambda i, j: (0, j)),
            pl.BlockSpec((tm, HEAD_DIM), lambda i, j: (i % sblocks, 0)),
            pl.BlockSpec((tm, HEAD_DIM), lambda i, j: (i % sblocks, 0)),
            pl.BlockSpec((HEAD_DIM, HEAD_DIM), lambda i, j: (0, 0)),
            pl.BlockSpec((HEAD_DIM, HEAD_DIM), lambda i, j: (0, 0)),
        ],
        out_specs=pl.BlockSpec((tm, wblk), lambda i, j: (i, j)),
        out_shape=jax.ShapeDtypeStruct((t, qk_width), BF16),
        compiler_params=_params(("parallel", "parallel")),
        name="qk_prep",
    )(p, gains, cos_t, sin_t, ones, perm)


def _flash_kernel(q_ref, k_ref, v_ref, o_ref, qt_ref, vt_ref, *stat_refs, tq, tk, cb, seq, unroll, ahead):
    nkv = seq // tk
    ncb = GQA_RATIO * tq // cb
    acc_refs, m_refs = stat_refs[:ncb], stat_refs[ncb:]

    @pl.when(pl.program_id(2) == 0)
    def _():
        for c in range(nkv):
            vt_ref[c, :HEAD_DIM, :] = v_ref[c * tk:(c + 1) * tk, :].T
            vt_ref[c, HEAD_DIM:, :] = jnp.ones((ONES_ROWS, tk), BF16)

    for g in range(GQA_RATIO):
        qt_ref[:, g * tq:(g + 1) * tq] = q_ref[:, g * HEAD_DIM:(g + 1) * HEAD_DIM].T
    for c in range(ncb):
        m_refs[c][...] = jnp.full(m_refs[c].shape, -jnp.inf, F32)
        acc_refs[c][...] = jnp.zeros(acc_refs[c].shape, F32)

    def body(j, carry):
        tiles = [(u, c) for u in range(unroll) for c in range(ncb)]
        ks, vts = [], []
        for u in range(unroll):
            off = pl.multiple_of((j * unroll + u) * tk, tk)
            ks.append(k_ref[pl.ds(off, tk), :])
            vts.append(vt_ref[j * unroll + u])

        def scores(t):
            u, c = tiles[t]
            return jnp.dot(ks[u], qt_ref[:, c * cb:(c + 1) * cb], preferred_element_type=F32)

        pending = [scores(t) for t in range(min(ahead, len(tiles)))]
        for t, (u, c) in enumerate(tiles):
            st = pending.pop(0)
            if t + ahead < len(tiles):
                pending.append(scores(t + ahead))
            m_old = m_refs[c][...]
            m_new = jnp.maximum(m_old, jnp.max(st, axis=0, keepdims=True))
            alpha = jnp.exp2(m_old - m_new)
            pt = jnp.exp2(st - m_new).astype(BF16)
            m_refs[c][...] = m_new
            acc_refs[c][...] = alpha * acc_refs[c][...] + jnp.dot(vts[u], pt, preferred_element_type=F32)
        return carry

    lax.fori_loop(0, nkv // unroll, body, 0)
    per_g = tq // cb
    for c in range(ncb):
        g, r = divmod(c, per_g)
        acc = acc_refs[c][...]
        out_t = acc[:HEAD_DIM, :] / acc[HEAD_DIM:HEAD_DIM + 1, :]
        o_ref[r * cb:(r + 1) * cb, g * HEAD_DIM:(g + 1) * HEAD_DIM] = out_t.T.astype(o_ref.dtype)


def _attention(qk, p, batch, seq, n_kv, k_col0, v_col0, *, tq=512, tk=256, cb=256, unroll=8, ahead=5):
    tq, tk = _tile(seq, tq), _tile(seq, tk)
    rows = GQA_RATIO * tq
    cb = _tile(rows, cb)
    unroll = _tile(seq // tk, unroll)
    qk3 = qk.reshape(batch, seq, qk.shape[1])
    p3 = p.reshape(batch, seq, p.shape[1])
    gw = GQA_RATIO * HEAD_DIM
    out = pl.pallas_call(
        functools.partial(_flash_kernel, tq=tq, tk=tk, cb=cb, seq=seq, unroll=unroll, ahead=ahead),
        grid=(batch, n_kv, seq // tq),
        in_specs=[
            pl.BlockSpec((None, tq, gw), lambda b, h, i: (b, i, h)),
            pl.BlockSpec((None, seq, HEAD_DIM), lambda b, h, i: (b, 0, k_col0 + h)),
            pl.BlockSpec((None, seq, HEAD_DIM), lambda b, h, i: (b, 0, v_col0 + h)),
        ],
        out_specs=pl.BlockSpec((None, tq, gw), lambda b, h, i: (b, i, h)),
        out_shape=jax.ShapeDtypeStruct((batch, seq, n_kv * gw), BF16),
        scratch_shapes=[
            pltpu.VMEM((HEAD_DIM, rows), BF16),
            pltpu.VMEM((seq // tk, HEAD_DIM + ONES_ROWS, tk), BF16),
        ] + [pltpu.VMEM((HEAD_DIM + ONES_ROWS, cb), F32)] * (rows // cb)
        + [pltpu.VMEM((1, cb), F32)] * (rows // cb),
        compiler_params=_params(("parallel", "parallel", "arbitrary")),
        name="flash_gqa",
    )(qk3, qk3, p3)
    return out.reshape(batch * seq, n_kv * gw)


def _conv_kernel(cb_ref, cc_ref, ch_ref, cbp_ref, chp_ref, cbn_ref, chn_ref, w_ref, o_ref, *, tm, sblocks):
    i = pl.program_id(0)
    first = (i % sblocks) == 0
    last = (i % sblocks) == sblocks - 1
    bx = cb_ref[...].astype(F32) * ch_ref[...].astype(F32)
    prev_row = cbp_ref[SUBLANES - 1:SUBLANES, :].astype(F32) * chp_ref[SUBLANES - 1:SUBLANES, :].astype(F32)
    next_row = cbn_ref[0:1, :].astype(F32) * chn_ref[0:1, :].astype(F32)
    prev_row = jnp.where(first, 0.0, prev_row)
    next_row = jnp.where(last, 0.0, next_row)
    row = lax.broadcasted_iota(jnp.int32, bx.shape, 0)
    prev = jnp.where(row == 0, prev_row, pltpu.roll(bx, 1, axis=0))
    nxt = jnp.where(row == tm - 1, next_row, pltpu.roll(bx, tm - 1, axis=0))
    w = w_ref[...]
    y = w[0:1, :] * prev + w[1:2, :] * bx + w[2:3, :] * nxt
    o_ref[...] = (cc_ref[...].astype(F32) * y).astype(o_ref.dtype)


def _short_conv(p, conv_w, layer, seq, cb0, cc0, ch0, width, *, tm=512, tc=1024):
    t = p.shape[0]
    tm, tc = _tile(seq, tm), math.gcd(cb0, cc0, ch0, width, tc)
    sblocks = seq // tm
    rpb = tm // SUBLANES
    nrb = t // SUBLANES

    def main(c0):
        return pl.BlockSpec((tm, tc), lambda i, j: (i, c0 // tc + j))

    def halo_prev(c0):
        return pl.BlockSpec((SUBLANES, tc), lambda i, j: (jnp.maximum(i * rpb - 1, 0), c0 // tc + j))

    def halo_next(c0):
        return pl.BlockSpec((SUBLANES, tc), lambda i, j: (jnp.minimum((i + 1) * rpb, nrb - 1), c0 // tc + j))

    return pl.pallas_call(
        functools.partial(_conv_kernel, tm=tm, sblocks=sblocks),
        grid=(t // tm, width // tc),
        in_specs=[main(cb0), main(cc0), main(ch0), halo_prev(cb0), halo_prev(ch0),
                  halo_next(cb0), halo_next(ch0),
                  pl.BlockSpec((None, CONV_K, tc), lambda i, j: (layer, 0, j))],
        out_specs=pl.BlockSpec((tm, tc), lambda i, j: (i, j)),
        out_shape=jax.ShapeDtypeStruct((t, width), BF16),
        compiler_params=_params(("parallel", "parallel")),
        name="short_conv",
    )(p, p, p, p, p, p, p, conv_w)


def _merge_kernel(a_ref, c_ref, wa_ref, wc_ref, ga_ref, gc_ref, o_ref, wab_ref, wcb_ref):
    @pl.when(pl.program_id(1) == 0)
    def _():
        wab_ref[...] = wa_ref[...].astype(BF16)
        wcb_ref[...] = wc_ref[...].astype(BF16)

    ya = jnp.dot(a_ref[...], wab_ref[...], preferred_element_type=F32)
    yc = jnp.dot(c_ref[...], wcb_ref[...], preferred_element_type=F32)
    o_ref[...] = (jax.nn.sigmoid(ga_ref[...].astype(F32)) * ya
                  + jax.nn.sigmoid(gc_ref[...].astype(F32)) * yc).astype(o_ref.dtype)


def _merge(attn, conv, wa, wc, p, layer, ga0, gc0, *, tm=1024, tn=512):
    t, ka = attn.shape
    kc = conv.shape[1]
    d = wa.shape[2]
    tm, tn = _tile(t, tm), math.gcd(ga0, gc0, d, tn)
    return pl.pallas_call(
        _merge_kernel,
        grid=(d // tn, t // tm),
        in_specs=[
            pl.BlockSpec((tm, ka), lambda j, i: (i, 0)),
            pl.BlockSpec((tm, kc), lambda j, i: (i, 0)),
            pl.BlockSpec((None, ka, tn), lambda j, i: (layer, 0, j)),
            pl.BlockSpec((None, kc, tn), lambda j, i: (layer, 0, j)),
            pl.BlockSpec((tm, tn), lambda j, i: (i, ga0 // tn + j)),
            pl.BlockSpec((tm, tn), lambda j, i: (i, gc0 // tn + j)),
        ],
        out_specs=pl.BlockSpec((tm, tn), lambda j, i: (i, j)),
        out_shape=jax.ShapeDtypeStruct((t, d), BF16),
        scratch_shapes=[pltpu.VMEM((ka, tn), BF16), pltpu.VMEM((kc, tn), BF16)],
        compiler_params=_params(("parallel", "arbitrary")),
        name="gated_merge",
    )(attn, conv, wa, wc, p, p)


def _ln_rows(h, g, b):
    mu = jnp.mean(h, axis=-1, keepdims=True)
    hc = h - mu
    var = jnp.mean(hc * hc, axis=-1, keepdims=True)
    return hc * lax.rsqrt(var + LN_EPS) * g + b


def _route(logits, bias, n_exp):
    epg = n_exp // N_GROUPS
    mx = jnp.max(logits, axis=0, keepdims=True)
    ex = jnp.exp(logits - mx)
    scores = ex / jnp.sum(ex, axis=0, keepdims=True)
    sel = scores + bias
    rows_sel = [sel[e:e + 1, :] for e in range(n_exp)]
    rows_sc = [scores[e:e + 1, :] for e in range(n_exp)]
    best = None
    for g in range(N_GROUPS):
        mem = rows_sel[g * epg:(g + 1) * epg]
        gs = None
        for a in range(epg):
            for c in range(a + 1, epg):
                pair = mem[a] + mem[c]
                gs = pair if gs is None else jnp.maximum(gs, pair)
        if best is None:
            best, grp = gs, jnp.zeros(gs.shape, jnp.int32)
        else:
            upd = gs > best
            best = jnp.where(upd, gs, best)
            grp = jnp.where(upd, g, grp)
    cand_sel, cand_sc = [], []
    for j in range(epg):
        cs, cc = rows_sel[j], rows_sc[j]
        for g in range(1, N_GROUPS):
            cs = jnp.where(grp == g, rows_sel[g * epg + j], cs)
            cc = jnp.where(grp == g, rows_sc[g * epg + j], cc)
        cand_sel.append(cs)
        cand_sc.append(cc)

    def first_argmax(vals, skip):
        bv = bi = bs = None
        for j in range(epg):
            v = vals[j] if skip is None else jnp.where(skip == j, -jnp.inf, vals[j])
            if bv is None:
                bv, bi, bs = v, jnp.zeros(v.shape, jnp.int32), cand_sc[0]
            else:
                upd = v > bv
                bv = jnp.where(upd, v, bv)
                bi = jnp.where(upd, j, bi)
                bs = jnp.where(upd, cand_sc[j], bs)
        return bi, bs

    i1, s1 = first_argmax(cand_sel, None)
    i2, s2 = first_argmax(cand_sel, i1)
    tot = s1 + s2
    zi = jnp.zeros((SUBLANES - TOP_K,) + i1.shape[1:], jnp.int32)
    zf = jnp.zeros((SUBLANES - TOP_K,) + i1.shape[1:], F32)
    idx = jnp.concatenate([grp * epg + i1, grp * epg + i2, zi], axis=0)
    gate = jnp.concatenate([s1 / tot, s2 / tot, zf], axis=0)
    return idx, gate


def _ln_route_kernel(h_ref, g_ref, b_ref, wh_ref, wl_ref, rb_ref, o_ref, op_ref, idx_ref, gate_ref, *, n_exp):
    y = _ln_rows(h_ref[...], g_ref[...], b_ref[...])
    half = y.shape[1] // 2
    o_ref[...] = y
    op_ref[...] = _pack_bf16_pair(y[:, :half], y[:, half:])
    yh = y.astype(BF16)
    yl = (y - yh.astype(F32)).astype(BF16)
    nt = (((1,), (1,)), ((), ()))
    logits = (lax.dot_general(wh_ref[...], yh, nt, preferred_element_type=F32)
              + lax.dot_general(wh_ref[...], yl, nt, preferred_element_type=F32)
              + lax.dot_general(wl_ref[...], yh, nt, preferred_element_type=F32))
    idx_ref[...], gate_ref[...] = _route(logits, rb_ref[...], n_exp)


def _ln_route(h, g, b, layer, w_router_t, bias_col, *, tm=256):
    t, d = h.shape
    n_exp = w_router_t.shape[0]
    tm = _tile(t, tm)
    wh = w_router_t.astype(BF16)
    wl = (w_router_t - wh.astype(F32)).astype(BF16)
    row = pl.BlockSpec((tm, d), lambda i: (i, 0))
    vec = pl.BlockSpec((None, 1, d), lambda i: (layer, 0, 0))
    wsp = pl.BlockSpec((n_exp, d), lambda i: (0, 0))
    rsp = pl.BlockSpec((SUBLANES, tm), lambda i: (0, i))
    return pl.pallas_call(
        functools.partial(_ln_route_kernel, n_exp=n_exp),
        grid=(t // tm,),
        in_specs=[row, vec, vec, wsp, wsp, pl.BlockSpec((n_exp, 1), lambda i: (0, 0))],
        out_specs=[row, pl.BlockSpec((tm, d // 2), lambda i: (i, 0)), rsp, rsp],
        out_shape=[jax.ShapeDtypeStruct((t, d), F32), jax.ShapeDtypeStruct((t, d // 2), jnp.uint32),
                   jax.ShapeDtypeStruct((SUBLANES, t), jnp.int32), jax.ShapeDtypeStruct((SUBLANES, t), F32)],
        compiler_params=_params(("parallel",)),
        name="ln_route",
    )(h, g, b, wh, wl, bias_col)


def _dispatch_kernel(slot_ref, cnt_ref, pst_ref, nu_ref, x_ref, o_hbm, z_ref, sem, *, tm, n_exp, epp, nblk):
    i = pl.program_id(0)
    n = pl.num_programs(0)

    @pl.when(i == 0)
    def _():
        z_ref[...] = jnp.zeros_like(z_ref)

    def issue(g, c):
        for u in range(ISSUE_UNROLL):
            r = g * ISSUE_UNROLL + u
            for k in range(TOP_K):
                pltpu.make_async_copy(x_ref.at[pl.ds(r, 1), :],
                                      o_hbm.at[pl.ds(slot_ref[(i * tm + r) * TOP_K + k], 1), :],
                                      sem.at[0]).start(priority=k % 2)
        return c

    lax.fori_loop(0, tm // ISSUE_UNROLL, issue, 0)

    def wait_rows(rows):
        pltpu.make_async_copy(z_ref.at[pl.ds(0, rows), :], o_hbm.at[pl.ds(0, rows), :], sem.at[0]).wait()

    for j in range(epp):
        e = i * epp + j

        @pl.when(e < n_exp)
        def _():
            cnt = cnt_ref[e]
            first = pst_ref[e] + cnt
            npad = (tm - cnt % tm) % tm

            def zero(r, c):
                pltpu.make_async_copy(z_ref.at[pl.ds(0, 1), :], o_hbm.at[pl.ds(first + r, 1), :], sem.at[0]).start()
                return c

            def wait_one(r, c):
                wait_rows(1)
                return c

            lax.fori_loop(0, npad, zero, 0)
            lax.fori_loop(0, npad, wait_one, 0)

    for k in range(TOP_K):
        wait_rows(tm)

    @pl.when(i == n - 1)
    def _():
        def zero_block(b, c):
            cp = pltpu.make_async_copy(z_ref, o_hbm.at[pl.ds(pl.multiple_of(b * tm, tm), tm), :], sem.at[0])
            cp.start()
            cp.wait()
            return c

        lax.fori_loop(nu_ref[0], nblk, zero_block, 0)


def _dispatch_rows(slot, counts, pstarts, n_used, xp, n_rows, tm):
    t, w = xp.shape
    n_exp = counts.shape[0]
    assert t % tm == 0 and n_rows % tm == 0
    n_steps = t // tm
    epp = -(-n_exp // n_steps)
    grid_spec = pltpu.PrefetchScalarGridSpec(
        num_scalar_prefetch=4,
        grid=(n_steps,),
        in_specs=[pl.BlockSpec((tm, w), lambda i, sl, cn, ps, nu: (i, 0))],
        out_specs=pl.BlockSpec(memory_space=pl.ANY),
        scratch_shapes=[pltpu.VMEM((tm, w), xp.dtype), pltpu.SemaphoreType.DMA((1,))],
    )
    return pl.pallas_call(
        functools.partial(_dispatch_kernel, tm=tm, n_exp=n_exp, epp=epp, nblk=n_rows // tm),
        grid_spec=grid_spec,
        out_shape=jax.ShapeDtypeStruct((n_rows, w), xp.dtype),
        compiler_params=_params(("arbitrary",)),
        name="dispatch_rows",
    )(slot, counts, pstarts, n_used, xp)


def _new_expert(be_ref, i):
    return jnp.logical_or(i == 0, be_ref[i] != be_ref[jnp.maximum(i - 1, 0)])


def _expert_up_kernel(be_ref, nu_ref, x_ref, wg_ref, wu_ref, o_ref, wgb_ref, wub_ref):
    i = pl.program_id(1)

    @pl.when(_new_expert(be_ref, i))
    def _():
        wgb_ref[...] = wg_ref[...].astype(BF16)
        wub_ref[...] = wu_ref[...].astype(BF16)

    @pl.when(i < nu_ref[0])
    def _():
        lo, hi = _unpack_bf16_pair(x_ref[...])
        lo, hi = lo.astype(BF16), hi.astype(BF16)
        half = lo.shape[1]
        hg = (jnp.dot(lo, wgb_ref[:half, :], preferred_element_type=F32)
              + jnp.dot(hi, wgb_ref[half:, :], preferred_element_type=F32))
        hu = (jnp.dot(lo, wub_ref[:half, :], preferred_element_type=F32)
              + jnp.dot(hi, wub_ref[half:, :], preferred_element_type=F32))
        o_ref[...] = (hg * jax.nn.sigmoid(hg) * hu).astype(o_ref.dtype)

    @pl.when(i >= nu_ref[0])
    def _():
        o_ref[...] = jnp.zeros_like(o_ref)


def _expert_down_kernel(be_ref, nu_ref, h_ref, wlo_ref, whi_ref, o_ref, wlob_ref, whib_ref):
    i = pl.program_id(1)

    @pl.when(_new_expert(be_ref, i))
    def _():
        wlob_ref[...] = wlo_ref[...].astype(BF16)
        whib_ref[...] = whi_ref[...].astype(BF16)

    @pl.when(i < nu_ref[0])
    def _():
        h = h_ref[...]
        o_ref[...] = _pack_bf16_pair(jnp.dot(h, wlob_ref[...], preferred_element_type=F32),
                                     jnp.dot(h, whib_ref[...], preferred_element_type=F32))

    @pl.when(i >= nu_ref[0])
    def _():
        o_ref[...] = jnp.zeros_like(o_ref)


def _experts(block_e, n_used, xs, wg, wu, wd, layer, tm, *, tf=512, tn=1024):
    r = xs.shape[0]
    d = 2 * xs.shape[1]
    ff = wg.shape[3]
    tf = _tile(ff, tf)
    nblk = r // tm
    hid = pl.pallas_call(
        _expert_up_kernel,
        grid_spec=pltpu.PrefetchScalarGridSpec(
            num_scalar_prefetch=2,
            grid=(ff // tf, nblk),
            in_specs=[
                pl.BlockSpec((tm, d // 2), lambda f, i, be, nu: (jnp.minimum(i, nu[0] - 1), 0)),
                pl.BlockSpec((None, None, d, tf), lambda f, i, be, nu: (layer, be[i], 0, f)),
                pl.BlockSpec((None, None, d, tf), lambda f, i, be, nu: (layer, be[i], 0, f)),
            ],
            out_specs=pl.BlockSpec((tm, tf), lambda f, i, be, nu: (i, f)),
            scratch_shapes=[pltpu.VMEM((d, tf), BF16), pltpu.VMEM((d, tf), BF16)],
        ),
        out_shape=jax.ShapeDtypeStruct((r, ff), BF16),
        compiler_params=_params(("arbitrary", "arbitrary")),
        name="expert_up",
    )(block_e, n_used, xs, wg, wu)
    half = d // 2
    tn = _tile(half, tn)
    hi0 = half // tn
    return pl.pallas_call(
        _expert_down_kernel,
        grid_spec=pltpu.PrefetchScalarGridSpec(
            num_scalar_prefetch=2,
            grid=(half // tn, nblk),
            in_specs=[
                pl.BlockSpec((tm, ff), lambda n, i, be, nu: (i, 0)),
                pl.BlockSpec((None, None, ff, tn), lambda n, i, be, nu: (layer, be[i], 0, n)),
                pl.BlockSpec((None, None, ff, tn), lambda n, i, be, nu: (layer, be[i], 0, hi0 + n)),
            ],
            out_specs=pl.BlockSpec((tm, tn), lambda n, i, be, nu: (i, n)),
            scratch_shapes=[pltpu.VMEM((ff, tn), BF16), pltpu.VMEM((ff, tn), BF16)],
        ),
        out_shape=jax.ShapeDtypeStruct((r, half), jnp.uint32),
        compiler_params=_params(("arbitrary", "arbitrary")),
        name="expert_down",
    )(block_e, n_used, hid, wd, wd)


def _combine_kernel(slot_ref, x_ref, y_hbm, gt_ref, g_ref, b_ref, o_ref, ob_ref, ybuf, sem, *, alpha, tm):
    i = pl.program_id(0)
    n = pl.num_programs(0)

    def row_copy(row, s, k, r):
        return pltpu.make_async_copy(y_hbm.at[pl.ds(row, 1), :], ybuf.at[s, k, pl.ds(r, 1), :], sem.at[s])

    def start_gather(step, s):
        base = step * (tm * TOP_K)

        def issue(r, c):
            for k in range(TOP_K):
                row_copy(slot_ref[base + r * TOP_K + k], s, k, r).start(priority=k % 2)
            return c

        lax.fori_loop(0, tm, issue, 0, unroll=4)

    def wait_gather(s):
        for k in range(TOP_K):
            pltpu.make_async_copy(y_hbm.at[pl.ds(0, tm), :], ybuf.at[s, k], sem.at[s]).wait()

    @pl.when(i == 0)
    def _():
        start_gather(0, 0)

    @pl.when(i + 1 < n)
    def _():
        start_gather(i + 1, (i + 1) % 2)

    s = i % 2
    wait_gather(s)
    gt = gt_ref[...]
    a_lo, a_hi = _unpack_bf16_pair(ybuf[s, 0])
    b_lo, b_hi = _unpack_bf16_pair(ybuf[s, 1])
    g0, g1 = gt[:, 0:1], gt[:, 1:2]
    ffn = jnp.concatenate([g0 * a_lo + g1 * b_lo, g0 * a_hi + g1 * b_hi], axis=1)
    y = _ln_rows(alpha * x_ref[...] + ffn, g_ref[...], b_ref[...])
    o_ref[...] = y
    ob_ref[...] = y.astype(BF16)


def _combine_ln(slot, x, y, gates_t, g, b, layer, alpha, *, tm=256):
    t, d = x.shape
    tm = _tile(t, tm)
    row = pl.BlockSpec((tm, d), lambda i, sl: (i, 0))
    vec = pl.BlockSpec((None, 1, d), lambda i, sl: (layer, 0, 0))
    grid_spec = pltpu.PrefetchScalarGridSpec(
        num_scalar_prefetch=1,
        grid=(t // tm,),
        in_specs=[row, pl.BlockSpec(memory_space=pl.ANY),
                  pl.BlockSpec((tm, SUBLANES), lambda i, sl: (i, 0)), vec, vec],
        out_specs=[row, row],
        scratch_shapes=[pltpu.VMEM((2, TOP_K, tm, d // 2), jnp.uint32), pltpu.SemaphoreType.DMA((2,))],
    )
    return pl.pallas_call(
        functools.partial(_combine_kernel, alpha=alpha, tm=tm),
        grid_spec=grid_spec,
        out_shape=[jax.ShapeDtypeStruct((t, d), F32), jax.ShapeDtypeStruct((t, d), BF16)],
        compiler_params=_params(("arbitrary",)),
        name="combine_ln",
    )(slot, x, y, gates_t, g, b)


def _rope_tables(seq):
    axis_dim = HEAD_DIM // 2
    rows = seq // GRID_W
    inv = ROPE_THETA ** (-jnp.arange(0, axis_dim, 2, dtype=F32) / axis_dim)
    ang_r = jnp.repeat(jnp.arange(rows, dtype=F32), GRID_W)[:, None] * inv[None, :]
    ang_c = jnp.tile(jnp.arange(GRID_W, dtype=F32), rows)[:, None] * inv[None, :]
    cr, sr, cc, sc = jnp.cos(ang_r), jnp.sin(ang_r), jnp.cos(ang_c), jnp.sin(ang_c)
    cos_t = jnp.concatenate([cr, cr, cc, cc], axis=1)
    sin_t = jnp.concatenate([-sr, sr, -sc, sc], axis=1)
    return cos_t, sin_t


def _dispatch(idx, n_exp, tm):
    t = idx.shape[1]
    n_asg = t * TOP_K
    e_flat = idx.T.reshape(-1)
    onehot = (e_flat[:, None] == jnp.arange(n_exp, dtype=jnp.int32)[None, :]).astype(jnp.int32)
    csum = jnp.cumsum(onehot, axis=0)
    rank = jnp.sum(csum * onehot, axis=1) - 1
    counts = csum[-1]
    pcounts = (counts + tm - 1) // tm * tm
    pends = jnp.cumsum(pcounts)
    pstarts = pends - pcounts
    slot = (pstarts[e_flat] + rank).astype(jnp.int32)
    nblk = n_asg // tm + n_exp
    block_start = jnp.arange(nblk, dtype=jnp.int32) * tm
    block_e = jnp.minimum(jnp.sum(pends[None, :] <= block_start[:, None], axis=-1), n_exp - 1).astype(jnp.int32)
    n_used = (pends[-1] // tm).astype(jnp.int32).reshape(1)
    return slot, counts.astype(jnp.int32), pstarts.astype(jnp.int32), block_e, n_used, nblk * tm


def kernel(x, w_in, q_norm_g, k_norm_g, conv_w, w_attn_proj, w_conv_proj, w_out, ln1_g, ln1_b,
           w_router, router_bias, w_gate, w_up, w_down, ln2_g, ln2_b):
    batch, seq, d = x.shape
    depth = w_in.shape[0]
    t = batch * seq
    attn_w = d // 2
    n_q = attn_w // HEAD_DIM
    n_kv = n_q // GQA_RATIO
    kv_w = n_kv * HEAD_DIM
    conv_wd = d // 2
    n_exp = w_router.shape[1]
    alpha = (2 * depth) ** 0.25
    q_end = attn_w
    k_end = q_end + kv_w
    v_end = k_end + kv_w
    cb_end = v_end + conv_wd
    cc_end = cb_end + conv_wd
    ch_end = cc_end + conv_wd
    ga_end = ch_end + d
    expert_tm = min(512, t)

    cos_t, sin_t = _rope_tables(seq)
    scale = HEAD_DIM ** -0.5 * math.log2(math.e)
    w_router_t = w_router.T
    bias_col = router_bias.reshape(n_exp, 1).astype(F32)
    ln1_g3, ln1_b3 = ln1_g.reshape(depth, 1, d), ln1_b.reshape(depth, 1, d)
    ln2_g3, ln2_b3 = ln2_g.reshape(depth, 1, d), ln2_b.reshape(depth, 1, d)

    xf = x.reshape(t, d)
    xb = xf.astype(BF16)
    for l in range(depth):
        p = _matmul(xb, w_in, l, BF16, name="in_proj")
        gains = jnp.concatenate([jnp.tile(q_norm_g[l] * scale, n_q), jnp.tile(k_norm_g[l], n_kv)]).reshape(1, k_end)
        qk = _qk_prep(p, gains.astype(F32), cos_t, sin_t, seq, k_end)
        attn = _attention(qk, p, batch, seq, n_kv, q_end // HEAD_DIM, k_end // HEAD_DIM)
        conv = _short_conv(p, conv_w, l, seq, v_end, cb_end, cc_end, conv_wd)
        merged = _merge(attn, conv, w_attn_proj, w_conv_proj, p, l, ch_end, ga_end)
        h1 = _matmul(merged, w_out, l, F32, res=xf, alpha=alpha, tm=1024, tn=512, name="out_proj")
        x1, x1p, idx, gate = _ln_route(h1, ln1_g3, ln1_b3, l, w_router_t, bias_col)
        slot, counts, pstarts, block_e, n_used, n_rows = _dispatch(idx[:TOP_K], n_exp, expert_tm)
        xs = _dispatch_rows(slot, counts, pstarts, n_used, x1p, n_rows, expert_tm)
        y = _experts(block_e, n_used, xs, w_gate, w_up, w_down, l, expert_tm)
        xf, xb = _combine_ln(slot, x1, y, gate.T, ln2_g3, ln2_b3, l, alpha)
    return xf.reshape(batch, seq, d)
```

```python
import functools
import math

import jax
import jax.numpy as jnp
from jax import lax
from jax.experimental import pallas as pl
from jax.experimental.pallas import tpu as pltpu

HEAD_DIM = 128
GQA_RATIO = 4
CONV_K = 3
GRID_W = 64
ROPE_THETA = 10000.0
N_GROUPS = 4
TOP_K = 2
RMS_EPS = 1e-6
LN_EPS = 1e-5
SUBLANES = 8
ONES_ROWS = 2 * SUBLANES
ISSUE_UNROLL = 8
VMEM_LIMIT = 56 * 1024 * 1024

F32 = jnp.float32
BF16 = jnp.bfloat16


def _params(semantics):
    return pltpu.CompilerParams(dimension_semantics=semantics, vmem_limit_bytes=VMEM_LIMIT)


def _tile(dim, pref):
    t = min(dim, pref)
    while dim % t:
        t //= 2
    return t


def _pack_bf16_pair(lo, hi):
    lo_b = lax.bitcast_convert_type(lo.astype(BF16).astype(F32), jnp.uint32) >> 16
    hi_b = lax.bitcast_convert_type(hi.astype(BF16).astype(F32), jnp.uint32) & jnp.uint32(0xFFFF0000)
    return hi_b | lo_b


def _unpack_bf16_pair(w):
    lo = lax.bitcast_convert_type(w << 16, F32)
    hi = lax.bitcast_convert_type(w & jnp.uint32(0xFFFF0000), F32)
    return lo, hi


def _mm_kernel(x_ref, w_ref, o_ref, wb_ref):
    @pl.when(pl.program_id(1) == 0)
    def _():
        wb_ref[...] = w_ref[...].astype(BF16)

    o_ref[...] = jnp.dot(x_ref[...], wb_ref[...], preferred_element_type=F32).astype(o_ref.dtype)


def _mm_res_kernel(x_ref, w_ref, r_ref, o_ref, wb_ref, *, alpha):
    @pl.when(pl.program_id(1) == 0)
    def _():
        wb_ref[...] = w_ref[...].astype(BF16)

    acc = jnp.dot(x_ref[...], wb_ref[...], preferred_element_type=F32)
    o_ref[...] = (alpha * r_ref[...] + acc).astype(o_ref.dtype)


def _matmul(x, w, layer, out_dtype, *, res=None, alpha=1.0, tm=512, tn=1024, name="matmul"):
    m, k = x.shape
    n = w.shape[2]
    tm, tn = _tile(m, tm), _tile(n, tn)
    in_specs = [
        pl.BlockSpec((tm, k), lambda j, i: (i, 0)),
        pl.BlockSpec((None, k, tn), lambda j, i: (layer, 0, j)),
    ]
    args = [x, w]
    body = _mm_kernel
    if res is not None:
        in_specs.append(pl.BlockSpec((tm, tn), lambda j, i: (i, j)))
        args.append(res)
        body = functools.partial(_mm_res_kernel, alpha=alpha)
    return pl.pallas_call(
        body,
        grid=(n // tn, m // tm),
        in_specs=in_specs,
        out_specs=pl.BlockSpec((tm, tn), lambda j, i: (i, j)),
        out_shape=jax.ShapeDtypeStruct((m, n), out_dtype),
        scratch_shapes=[pltpu.VMEM((k, tn), BF16)],
        compiler_params=_params(("parallel", "arbitrary")),
        name=name,
    )(*args)


def _qk_prep_kernel(p_ref, g_ref, c_ref, s_ref, ones_ref, perm_ref, o_ref, *, heads):
    cos = c_ref[...]
    sin = s_ref[...]
    ones = ones_ref[...]
    perm = perm_ref[...]
    for h in range(heads):
        sl = slice(h * HEAD_DIM, (h + 1) * HEAD_DIM)
        xh = p_ref[:, sl].astype(F32)
        ssq = jnp.dot((xh * xh).astype(BF16), ones, preferred_element_type=F32)
        y = xh * lax.rsqrt(ssq * (1.0 / HEAD_DIM) + RMS_EPS) * g_ref[:, sl]
        swapped = jnp.dot(y.astype(BF16), perm, preferred_element_type=F32)
        o_ref[:, sl] = (y * cos + swapped * sin).astype(o_ref.dtype)


def _qk_prep(p, gains, cos_t, sin_t, seq, qk_width, *, tm=512, heads=32):
    t = p.shape[0]
    tm = _tile(seq, tm)
    n_heads = qk_width // HEAD_DIM
    heads = _tile(n_heads, heads)
    wblk = heads * HEAD_DIM
    sblocks = seq // tm
    ones = jnp.ones((HEAD_DIM, HEAD_DIM), BF16)
    axis_dim, half_dim = HEAD_DIM // 2, HEAD_DIM // 4
    lane = jnp.arange(HEAD_DIM)
    partner = jnp.where((lane % axis_dim) < half_dim, lane + half_dim, lane - half_dim)
    perm = (lane[:, None] == partner[None, :]).astype(BF16)
    return pl.pallas_call(
        functools.partial(_qk_prep_kernel, heads=heads),
        grid=(t // tm, n_heads // heads),
        in_specs=[
            pl.BlockSpec((tm, wblk), lambda i, j: (i, j)),
            pl.BlockSpec((1, wblk), lambda i, j: (0, j)),
            pl.BlockSpec((tm, HEAD_DIM), lambda i, j: (i % sblocks, 0)),
            pl.BlockSpec((tm, HEAD_DIM), lambda i, j: (i % sblocks, 0)),
            pl.BlockSpec((HEAD_DIM, HEAD_DIM), lambda i, j: (0, 0)),
            pl.BlockSpec((HEAD_DIM, HEAD_DIM), lambda i, j: (0, 0)),
        ],
        out_specs=pl.BlockSpec((tm, wblk), lambda i, j: (i, j)),
        out_shape=jax.ShapeDtypeStruct((t, qk_width), BF16),
        compiler_params=_params(("parallel", "parallel")),
        name="qk_prep",
    )(p, gains, cos_t, sin_t, ones, perm)


def _flash_kernel(q_ref, k_ref, v_ref, o_ref, qt_ref, vt_ref, *stat_refs, tq, tk, cb, seq, unroll, ahead):
    nkv = seq // tk
    ncb = GQA_RATIO * tq // cb
    acc_refs, m_refs = stat_refs[:ncb], stat_refs[ncb:]

    @pl.when(pl.program_id(2) == 0)
    def _():
        for c in range(nkv):
            vt_ref[c, :HEAD_DIM, :] = v_ref[c * tk:(c + 1) * tk, :].T
            vt_ref[c, HEAD_DIM:, :] = jnp.ones((ONES_ROWS, tk), BF16)

    for g in range(GQA_RATIO):
        qt_ref[:, g * tq:(g + 1) * tq] = q_ref[:, g * HEAD_DIM:(g + 1) * HEAD_DIM].T
    for c in range(ncb):
        m_refs[c][...] = jnp.full(m_refs[c].shape, -jnp.inf, F32)
        acc_refs[c][...] = jnp.zeros(acc_refs[c].shape, F32)

    def body(j, carry):
        tiles = [(u, c) for u in range(unroll) for c in range(ncb)]
        ks, vts = [], []
        for u in range(unroll):
            off = pl.multiple_of((j * unroll + u) * tk, tk)
            ks.append(k_ref[pl.ds(off, tk), :])
            vts.append(vt_ref[j * unroll + u])

        def scores(t):
            u, c = tiles[t]
            return jnp.dot(ks[u], qt_ref[:, c * cb:(c + 1) * cb], preferred_element_type=F32)

        pending = [scores(t) for t in range(min(ahead, len(tiles)))]
        for t, (u, c) in enumerate(tiles):
            st = pending.pop(0)
            if t + ahead < len(tiles):
                pending.append(scores(t + ahead))
            m_old = m_refs[c][...]
            m_new = jnp.maximum(m_old, jnp.max(st, axis=0, keepdims=True))
            alpha = jnp.exp2(m_old - m_new)
            pt = jnp.exp2(st - m_new).astype(BF16)
            m_refs[c][...] = m_new
            acc_refs[c][...] = alpha * acc_refs[c][...] + jnp.dot(vts[u], pt, preferred_element_type=F32)
        return carry

    lax.fori_loop(0, nkv // unroll, body, 0)
    per_g = tq // cb
    for c in range(ncb):
        g, r = divmod(c, per_g)
        acc = acc_refs[c][...]
        out_t = acc[:HEAD_DIM, :] / acc[HEAD_DIM:HEAD_DIM + 1, :]
        o_ref[r * cb:(r + 1) * cb, g * HEAD_DIM:(g + 1) * HEAD_DIM] = out_t.T.astype(o_ref.dtype)


def _attention(qk, p, batch, seq, n_kv, k_col0, v_col0, *, tq=512, tk=256, cb=256, unroll=8, ahead=5):
    tq, tk = _tile(seq, tq), _tile(seq, tk)
    rows = GQA_RATIO * tq
    cb = _tile(rows, cb)
    unroll = _tile(seq // tk, unroll)
    qk3 = qk.reshape(batch, seq, qk.shape[1])
    p3 = p.reshape(batch, seq, p.shape[1])
    gw = GQA_RATIO * HEAD_DIM
    out = pl.pallas_call(
        functools.partial(_flash_kernel, tq=tq, tk=tk, cb=cb, seq=seq, unroll=unroll, ahead=ahead),
        grid=(batch, n_kv, seq // tq),
        in_specs=[
            pl.BlockSpec((None, tq, gw), lambda b, h, i: (b, i, h)),
            pl.BlockSpec((None, seq, HEAD_DIM), lambda b, h, i: (b, 0, k_col0 + h)),
            pl.BlockSpec((None, seq, HEAD_DIM), lambda b, h, i: (b, 0, v_col0 + h)),
        ],
        out_specs=pl.BlockSpec((None, tq, gw), lambda b, h, i: (b, i, h)),
        out_shape=jax.ShapeDtypeStruct((batch, seq, n_kv * gw), BF16),
        scratch_shapes=[
            pltpu.VMEM((HEAD_DIM, rows), BF16),
            pltpu.VMEM((seq // tk, HEAD_DIM + ONES_ROWS, tk), BF16),
        ] + [pltpu.VMEM((HEAD_DIM + ONES_ROWS, cb), F32)] * (rows // cb)
        + [pltpu.VMEM((1, cb), F32)] * (rows // cb),
        compiler_params=_params(("parallel", "parallel", "arbitrary")),
        name="flash_gqa",
    )(qk3, qk3, p3)
    return out.reshape(batch * seq, n_kv * gw)


def _conv_kernel(cb_ref, cc_ref, ch_ref, cbp_ref, chp_ref, cbn_ref, chn_ref, w_ref, o_ref, *, tm, sblocks):
    i = pl.program_id(0)
    first = (i % sblocks) == 0
    last = (i % sblocks) == sblocks - 1
    bx = cb_ref[...].astype(F32) * ch_ref[...].astype(F32)
    prev_row = cbp_ref[SUBLANES - 1:SUBLANES, :].astype(F32) * chp_ref[SUBLANES - 1:SUBLANES, :].astype(F32)
    next_row = cbn_ref[0:1, :].astype(F32) * chn_ref[0:1, :].astype(F32)
    prev_row = jnp.where(first, 0.0, prev_row)
    next_row = jnp.where(last, 0.0, next_row)
    row = lax.broadcasted_iota(jnp.int32, bx.shape, 0)
    prev = jnp.where(row == 0, prev_row, pltpu.roll(bx, 1, axis=0))
    nxt = jnp.where(row == tm - 1, next_row, pltpu.roll(bx, tm - 1, axis=0))
    w = w_ref[...]
    y = w[0:1, :] * prev + w[1:2, :] * bx + w[2:3, :] * nxt
    o_ref[...] = (cc_ref[...].astype(F32) * y).astype(o_ref.dtype)


def _short_conv(p, conv_w, layer, seq, cb0, cc0, ch0, width, *, tm=512, tc=1024):
    t = p.shape[0]
    tm, tc = _tile(seq, tm), math.gcd(cb0, cc0, ch0, width, tc)
    sblocks = seq // tm
    rpb = tm // SUBLANES
    nrb = t // SUBLANES

    def main(c0):
        return pl.BlockSpec((tm, tc), lambda i, j: (i, c0 // tc + j))

    def halo_prev(c0):
        return pl.BlockSpec((SUBLANES, tc), lambda i, j: (jnp.maximum(i * rpb - 1, 0), c0 // tc + j))

    def halo_next(c0):
        return pl.BlockSpec((SUBLANES, tc), lambda i, j: (jnp.minimum((i + 1) * rpb, nrb - 1), c0 // tc + j))

    return pl.pallas_call(
        functools.partial(_conv_kernel, tm=tm, sblocks=sblocks),
        grid=(t // tm, width // tc),
        in_specs=[main(cb0), main(cc0), main(ch0), halo_prev(cb0), halo_prev(ch0),
                  halo_next(cb0), halo_next(ch0),
                  pl.BlockSpec((None, CONV_K, tc), lambda i, j: (layer, 0, j))],
        out_specs=pl.BlockSpec((tm, tc), lambda i, j: (i, j)),
        out_shape=jax.ShapeDtypeStruct((t, width), BF16),
        compiler_params=_params(("parallel", "parallel")),
        name="short_conv",
    )(p, p, p, p, p, p, p, conv_w)


def _merge_kernel(a_ref, c_ref, wa_ref, wc_ref, ga_ref, gc_ref, o_ref, wab_ref, wcb_ref):
    @pl.when(pl.program_id(1) == 0)
    def _():
        wab_ref[...] = wa_ref[...].astype(BF16)
        wcb_ref[...] = wc_ref[...].astype(BF16)

    ya = jnp.dot(a_ref[...], wab_ref[...], preferred_element_type=F32)
    yc = jnp.dot(c_ref[...], wcb_ref[...], preferred_element_type=F32)
    o_ref[...] = (jax.nn.sigmoid(ga_ref[...].astype(F32)) * ya
                  + jax.nn.sigmoid(gc_ref[...].astype(F32)) * yc).astype(o_ref.dtype)


def _merge(attn, conv, wa, wc, p, layer, ga0, gc0, *, tm=1024, tn=512):
    t, ka = attn.shape
    kc = conv.shape[1]
    d = wa.shape[2]
    tm, tn = _tile(t, tm), math.gcd(ga0, gc0, d, tn)
    return pl.pallas_call(
        _merge_kernel,
        grid=(d // tn, t // tm),
        in_specs=[
            pl.BlockSpec((tm, ka), lambda j, i: (i, 0)),
            pl.BlockSpec((tm, kc), lambda j, i: (i, 0)),
            pl.BlockSpec((None, ka, tn), lambda j, i: (layer, 0, j)),
            pl.BlockSpec((None, kc, tn), lambda j, i: (layer, 0, j)),
            pl.BlockSpec((tm, tn), lambda j, i: (i, ga0 // tn + j)),
            pl.BlockSpec((tm, tn), lambda j, i: (i, gc0 // tn + j)),
        ],
        out_specs=pl.BlockSpec((tm, tn), lambda j, i: (i, j)),
        out_shape=jax.ShapeDtypeStruct((t, d), BF16),
        scratch_shapes=[pltpu.VMEM((ka, tn), BF16), pltpu.VMEM((kc, tn), BF16)],
        compiler_params=_params(("parallel", "arbitrary")),
        name="gated_merge",
    )(attn, conv, wa, wc, p, p)


def _ln_rows(h, g, b):
    mu = jnp.mean(h, axis=-1, keepdims=True)
    hc = h - mu
    var = jnp.mean(hc * hc, axis=-1, keepdims=True)
    return hc * lax.rsqrt(var + LN_EPS) * g + b


def _route(logits, bias, n_exp):
    epg = n_exp // N_GROUPS
    mx = jnp.max(logits, axis=0, keepdims=True)
    ex = jnp.exp(logits - mx)
    scores = ex / jnp.sum(ex, axis=0, keepdims=True)
    sel = scores + bias
    rows_sel = [sel[e:e + 1, :] for e in range(n_exp)]
    rows_sc = [scores[e:e + 1, :] for e in range(n_exp)]
    best = None
    for g in range(N_GROUPS):
        mem = rows_sel[g * epg:(g + 1) * epg]
        gs = None
        for a in range(epg):
            for c in range(a + 1, epg):
                pair = mem[a] + mem[c]
                gs = pair if gs is None else jnp.maximum(gs, pair)
        if best is None:
            best, grp = gs, jnp.zeros(gs.shape, jnp.int32)
        else:
            upd = gs > best
            best = jnp.where(upd, gs, best)
            grp = jnp.where(upd, g, grp)
    cand_sel, cand_sc = [], []
    for j in range(epg):
        cs, cc = rows_sel[j], rows_sc[j]
        for g in range(1, N_GROUPS):
            cs = jnp.where(grp == g, rows_sel[g * epg + j], cs)
            cc = jnp.where(grp == g, rows_sc[g * epg + j], cc)
        cand_sel.append(cs)
        cand_sc.append(cc)

    def first_argmax(vals, skip):
        bv = bi = bs = None
        for j in range(epg):
            v = vals[j] if skip is None else jnp.where(skip == j, -jnp.inf, vals[j])
            if bv is None:
                bv, bi, bs = v, jnp.zeros(v.shape, jnp.int32), cand_sc[0]
            else:
                upd = v > bv
                bv = jnp.where(upd, v, bv)
                bi = jnp.where(upd, j, bi)
                bs = jnp.where(upd, cand_sc[j], bs)
        return bi, bs

    i1, s1 = first_argmax(cand_sel, None)
    i2, s2 = first_argmax(cand_sel, i1)
    tot = s1 + s2
    zi = jnp.zeros((SUBLANES - TOP_K,) + i1.shape[1:], jnp.int32)
    zf = jnp.zeros((SUBLANES - TOP_K,) + i1.shape[1:], F32)
    idx = jnp.concatenate([grp * epg + i1, grp * epg + i2, zi], axis=0)
    gate = jnp.concatenate([s1 / tot, s2 / tot, zf], axis=0)
    return idx, gate


def _ln_route_kernel(h_ref, g_ref, b_ref, wh_ref, wl_ref, rb_ref, o_ref, op_ref, idx_ref, gate_ref, *, n_exp):
    y = _ln_rows(h_ref[...], g_ref[...], b_ref[...])
    half = y.shape[1] // 2
    o_ref[...] = y
    op_ref[...] = _pack_bf16_pair(y[:, :half], y[:, half:])
    yh = y.astype(BF16)
    yl = (y - yh.astype(F32)).astype(BF16)
    nt = (((1,), (1,)), ((), ()))
    logits = (lax.dot_general(wh_ref[...], yh, nt, preferred_element_type=F32)
              + lax.dot_general(wh_ref[...], yl, nt, preferred_element_type=F32)
              + lax.dot_general(wl_ref[...], yh, nt, preferred_element_type=F32))
    idx_ref[...], gate_ref[...] = _route(logits, rb_ref[...], n_exp)


def _ln_route(h, g, b, layer, w_router_t, bias_col, *, tm=256):
    t, d = h.shape
    n_exp = w_router_t.shape[0]
    tm = _tile(t, tm)
    wh = w_router_t.astype(BF16)
    wl = (w_router_t - wh.astype(F32)).astype(BF16)
    row = pl.BlockSpec((tm, d), lambda i: (i, 0))
    vec = pl.BlockSpec((None, 1, d), lambda i: (layer, 0, 0))
    wsp = pl.BlockSpec((n_exp, d), lambda i: (0, 0))
    rsp = pl.BlockSpec((SUBLANES, tm), lambda i: (0, i))
    return pl.pallas_call(
        functools.partial(_ln_route_kernel, n_exp=n_exp),
        grid=(t // tm,),
        in_specs=[row, vec, vec, wsp, wsp, pl.BlockSpec((n_exp, 1), lambda i: (0, 0))],
        out_specs=[row, pl.BlockSpec((tm, d // 2), lambda i: (i, 0)), rsp, rsp],
        out_shape=[jax.ShapeDtypeStruct((t, d), F32), jax.ShapeDtypeStruct((t, d // 2), jnp.uint32),
                   jax.ShapeDtypeStruct((SUBLANES, t), jnp.int32), jax.ShapeDtypeStruct((SUBLANES, t), F32)],
        compiler_params=_params(("parallel",)),
        name="ln_route",
    )(h, g, b, wh, wl, bias_col)


def _dispatch_kernel(slot_ref, cnt_ref, pst_ref, nu_ref, x_ref, o_hbm, z_ref, sem, *, tm, n_exp, epp, nblk):
    i = pl.program_id(0)
    n = pl.num_programs(0)

    @pl.when(i == 0)
    def _():
        z_ref[...] = jnp.zeros_like(z_ref)

    def issue(g, c):
        for u in range(ISSUE_UNROLL):
            r = g * ISSUE_UNROLL + u
            for k in range(TOP_K):
                pltpu.make_async_copy(x_ref.at[pl.ds(r, 1), :],
                                      o_hbm.at[pl.ds(slot_ref[(i * tm + r) * TOP_K + k], 1), :],
                                      sem.at[0]).start(priority=k % 2)
        return c

    lax.fori_loop(0, tm // ISSUE_UNROLL, issue, 0)

    def wait_rows(rows):
        pltpu.make_async_copy(z_ref.at[pl.ds(0, rows), :], o_hbm.at[pl.ds(0, rows), :], sem.at[0]).wait()

    for j in range(epp):
        e = i * epp + j

        @pl.when(e < n_exp)
        def _():
            cnt = cnt_ref[e]
            first = pst_ref[e] + cnt
            npad = (tm - cnt % tm) % tm

            def zero(r, c):
                pltpu.make_async_copy(z_ref.at[pl.ds(0, 1), :], o_hbm.at[pl.ds(first + r, 1), :], sem.at[0]).start()
                return c

            def wait_one(r, c):
                wait_rows(1)
                return c

            lax.fori_loop(0, npad, zero, 0)
            lax.fori_loop(0, npad, wait_one, 0)

    for k in range(TOP_K):
        wait_rows(tm)

    @pl.when(i == n - 1)
    def _():
        def zero_block(b, c):
            cp = pltpu.make_async_copy(z_ref, o_hbm.at[pl.ds(pl.multiple_of(b * tm, tm), tm), :], sem.at[0])
            cp.start()
            cp.wait()
            return c

        lax.fori_loop(nu_ref[0], nblk, zero_block, 0)


def _dispatch_rows(slot, counts, pstarts, n_used, xp, n_rows, tm):
    t, w = xp.shape
    n_exp = counts.shape[0]
    assert t % tm == 0 and n_rows % tm == 0
    n_steps = t // tm
    epp = -(-n_exp // n_steps)
    grid_spec = pltpu.PrefetchScalarGridSpec(
        num_scalar_prefetch=4,
        grid=(n_steps,),
        in_specs=[pl.BlockSpec((tm, w), lambda i, sl, cn, ps, nu: (i, 0))],
        out_specs=pl.BlockSpec(memory_space=pl.ANY),
        scratch_shapes=[pltpu.VMEM((tm, w), xp.dtype), pltpu.SemaphoreType.DMA((1,))],
    )
    return pl.pallas_call(
        functools.partial(_dispatch_kernel, tm=tm, n_exp=n_exp, epp=epp, nblk=n_rows // tm),
        grid_spec=grid_spec,
        out_shape=jax.ShapeDtypeStruct((n_rows, w), xp.dtype),
        compiler_params=_params(("arbitrary",)),
        name="dispatch_rows",
    )(slot, counts, pstarts, n_used, xp)


def _new_expert(be_ref, i):
    return jnp.logical_or(i == 0, be_ref[i] != be_ref[jnp.maximum(i - 1, 0)])


def _expert_up_kernel(be_ref, nu_ref, vr_ref, x_ref, wg_ref, wu_ref, o_ref, wgb_ref, wub_ref):
    i = pl.program_id(1)
    tm = x_ref.shape[0]
    hb = tm // 2
    valid = vr_ref[i]

    @pl.when(_new_expert(be_ref, i))
    def _():
        wgb_ref[...] = wg_ref[...].astype(BF16)
        wub_ref[...] = wu_ref[...].astype(BF16)

    def hidden(rows):
        lo, hi = _unpack_bf16_pair(x_ref[rows, :])
        lo, hi = lo.astype(BF16), hi.astype(BF16)
        half = lo.shape[1]
        hg = (jnp.dot(lo, wgb_ref[:half, :], preferred_element_type=F32)
              + jnp.dot(hi, wgb_ref[half:, :], preferred_element_type=F32))
        hu = (jnp.dot(lo, wub_ref[:half, :], preferred_element_type=F32)
              + jnp.dot(hi, wub_ref[half:, :], preferred_element_type=F32))
        return (hg * jax.nn.sigmoid(hg) * hu).astype(o_ref.dtype)

    @pl.when(valid > hb)
    def _():
        o_ref[...] = hidden(slice(None))

    @pl.when(jnp.logical_and(valid > 0, valid <= hb))
    def _():
        o_ref[:hb, :] = hidden(slice(0, hb))
        o_ref[hb:, :] = jnp.zeros((tm - hb, o_ref.shape[1]), o_ref.dtype)

    @pl.when(valid == 0)
    def _():
        o_ref[...] = jnp.zeros_like(o_ref)


def _expert_down_kernel(be_ref, nu_ref, vr_ref, h_ref, wlo_ref, whi_ref, o_ref, wlob_ref, whib_ref):
    i = pl.program_id(1)
    tm = h_ref.shape[0]
    hb = tm // 2
    valid = vr_ref[i]

    @pl.when(_new_expert(be_ref, i))
    def _():
        wlob_ref[...] = wlo_ref[...].astype(BF16)
        whib_ref[...] = whi_ref[...].astype(BF16)

    def rows_out(rows):
        h = h_ref[rows, :]
        return _pack_bf16_pair(jnp.dot(h, wlob_ref[...], preferred_element_type=F32),
                               jnp.dot(h, whib_ref[...], preferred_element_type=F32))

    @pl.when(valid > hb)
    def _():
        o_ref[...] = rows_out(slice(None))

    @pl.when(jnp.logical_and(valid > 0, valid <= hb))
    def _():
        o_ref[:hb, :] = rows_out(slice(0, hb))
        o_ref[hb:, :] = jnp.zeros((tm - hb, o_ref.shape[1]), o_ref.dtype)

    @pl.when(valid == 0)
    def _():
        o_ref[...] = jnp.zeros_like(o_ref)


def _experts(block_e, n_used, valid_rows, xs, wg, wu, wd, layer, tm, *, tf=512, tn=1024):
    r = xs.shape[0]
    d = 2 * xs.shape[1]
    ff = wg.shape[3]
    tf = _tile(ff, tf)
    nblk = r // tm
    hid = pl.pallas_call(
        _expert_up_kernel,
        grid_spec=pltpu.PrefetchScalarGridSpec(
            num_scalar_prefetch=3,
            grid=(ff // tf, nblk),
            in_specs=[
                pl.BlockSpec((tm, d // 2), lambda f, i, be, nu, vr: (jnp.minimum(i, nu[0] - 1), 0)),
                pl.BlockSpec((None, None, d, tf), lambda f, i, be, nu, vr: (layer, be[i], 0, f)),
                pl.BlockSpec((None, None, d, tf), lambda f, i, be, nu, vr: (layer, be[i], 0, f)),
            ],
            out_specs=pl.BlockSpec((tm, tf), lambda f, i, be, nu, vr: (i, f)),
            scratch_shapes=[pltpu.VMEM((d, tf), BF16), pltpu.VMEM((d, tf), BF16)],
        ),
        out_shape=jax.ShapeDtypeStruct((r, ff), BF16),
        compiler_params=_params(("arbitrary", "arbitrary")),
        name="expert_up",
    )(block_e, n_used, valid_rows, xs, wg, wu)
    half = d // 2
    tn = _tile(half, tn)
    hi0 = half // tn
    return pl.pallas_call(
        _expert_down_kernel,
        grid_spec=pltpu.PrefetchScalarGridSpec(
            num_scalar_prefetch=3,
            grid=(half // tn, nblk),
            in_specs=[
                pl.BlockSpec((tm, ff), lambda n, i, be, nu, vr: (i, 0)),
                pl.BlockSpec((None, None, ff, tn), lambda n, i, be, nu, vr: (layer, be[i], 0, n)),
                pl.BlockSpec((None, None, ff, tn), lambda n, i, be, nu, vr: (layer, be[i], 0, hi0 + n)),
            ],
            out_specs=pl.BlockSpec((tm, tn), lambda n, i, be, nu, vr: (i, n)),
            scratch_shapes=[pltpu.VMEM((ff, tn), BF16), pltpu.VMEM((ff, tn), BF16)],
        ),
        out_shape=jax.ShapeDtypeStruct((r, half), jnp.uint32),
        compiler_params=_params(("arbitrary", "arbitrary")),
        name="expert_down",
    )(block_e, n_used, valid_rows, hid, wd, wd)


def _combine_kernel(slot_ref, x_ref, y_hbm, gt_ref, g_ref, b_ref, o_ref, ob_ref, ybuf, sem, *, alpha, tm):
    i = pl.program_id(0)
    n = pl.num_programs(0)

    def row_copy(row, s, k, r):
        return pltpu.make_async_copy(y_hbm.at[pl.ds(row, 1), :], ybuf.at[s, k, pl.ds(r, 1), :], sem.at[s])

    def start_gather(step, s):
        base = step * (tm * TOP_K)

        def issue(r, c):
            for k in range(TOP_K):
                row_copy(slot_ref[base + r * TOP_K + k], s, k, r).start(priority=k % 2)
            return c

        lax.fori_loop(0, tm, issue, 0, unroll=4)

    def wait_gather(s):
        for k in range(TOP_K):
            pltpu.make_async_copy(y_hbm.at[pl.ds(0, tm), :], ybuf.at[s, k], sem.at[s]).wait()

    @pl.when(i == 0)
    def _():
        start_gather(0, 0)

    @pl.when(i + 1 < n)
    def _():
        start_gather(i + 1, (i + 1) % 2)

    s = i % 2
    wait_gather(s)
    gt = gt_ref[...]
    a_lo, a_hi = _unpack_bf16_pair(ybuf[s, 0])
    b_lo, b_hi = _unpack_bf16_pair(ybuf[s, 1])
    g0, g1 = gt[:, 0:1], gt[:, 1:2]
    ffn = jnp.concatenate([g0 * a_lo + g1 * b_lo, g0 * a_hi + g1 * b_hi], axis=1)
    y = _ln_rows(alpha * x_ref[...] + ffn, g_ref[...], b_ref[...])
    o_ref[...] = y
    ob_ref[...] = y.astype(BF16)


def _combine_ln(slot, x, y, gates_t, g, b, layer, alpha, *, tm=256):
    t, d = x.shape
    tm = _tile(t, tm)
    row = pl.BlockSpec((tm, d), lambda i, sl: (i, 0))
    vec = pl.BlockSpec((None, 1, d), lambda i, sl: (layer, 0, 0))
    grid_spec = pltpu.PrefetchScalarGridSpec(
        num_scalar_prefetch=1,
        grid=(t // tm,),
        in_specs=[row, pl.BlockSpec(memory_space=pl.ANY),
                  pl.BlockSpec((tm, SUBLANES), lambda i, sl: (i, 0)), vec, vec],
        out_specs=[row, row],
        scratch_shapes=[pltpu.VMEM((2, TOP_K, tm, d // 2), jnp.uint32), pltpu.SemaphoreType.DMA((2,))],
    )
    return pl.pallas_call(
        functools.partial(_combine_kernel, alpha=alpha, tm=tm),
        grid_spec=grid_spec,
        out_shape=[jax.ShapeDtypeStruct((t, d), F32), jax.ShapeDtypeStruct((t, d), BF16)],
        compiler_params=_params(("arbitrary",)),
        name="combine_ln",
    )(slot, x, y, gates_t, g, b)


def _rope_tables(seq):
    axis_dim = HEAD_DIM // 2
    rows = seq // GRID_W
    inv = ROPE_THETA ** (-jnp.arange(0, axis_dim, 2, dtype=F32) / axis_dim)
    ang_r = jnp.repeat(jnp.arange(rows, dtype=F32), GRID_W)[:, None] * inv[None, :]
    ang_c = jnp.tile(jnp.arange(GRID_W, dtype=F32), rows)[:, None] * inv[None, :]
    cr, sr, cc, sc = jnp.cos(ang_r), jnp.sin(ang_r), jnp.cos(ang_c), jnp.sin(ang_c)
    cos_t = jnp.concatenate([cr, cr, cc, cc], axis=1)
    sin_t = jnp.concatenate([-sr, sr, -sc, sc], axis=1)
    return cos_t, sin_t


def _dispatch(idx, n_exp, tm):
    t = idx.shape[1]
    n_asg = t * TOP_K
    e_flat = idx.T.reshape(-1)
    onehot = (e_flat[:, None] == jnp.arange(n_exp, dtype=jnp.int32)[None, :]).astype(jnp.int32)
    csum = jnp.cumsum(onehot, axis=0)
    rank = jnp.sum(csum * onehot, axis=1) - 1
    counts = csum[-1]
    pcounts = (counts + tm - 1) // tm * tm
    pends = jnp.cumsum(pcounts)
    pstarts = pends - pcounts
    slot = (pstarts[e_flat] + rank).astype(jnp.int32)
    nblk = n_asg // tm + n_exp
    block_start = jnp.arange(nblk, dtype=jnp.int32) * tm
    block_e = jnp.minimum(jnp.sum(pends[None, :] <= block_start[:, None], axis=-1), n_exp - 1).astype(jnp.int32)
    n_used = (pends[-1] // tm).astype(jnp.int32).reshape(1)
    valid_rows = jnp.clip(pstarts[block_e] + counts[block_e] - block_start, 0, tm).astype(jnp.int32)
    return slot, counts.astype(jnp.int32), pstarts.astype(jnp.int32), block_e, n_used, valid_rows, nblk * tm


def kernel(x, w_in, q_norm_g, k_norm_g, conv_w, w_attn_proj, w_conv_proj, w_out, ln1_g, ln1_b,
           w_router, router_bias, w_gate, w_up, w_down, ln2_g, ln2_b):
    batch, seq, d = x.shape
    depth = w_in.shape[0]
    t = batch * seq
    attn_w = d // 2
    n_q = attn_w // HEAD_DIM
    n_kv = n_q // GQA_RATIO
    kv_w = n_kv * HEAD_DIM
    conv_wd = d // 2
    n_exp = w_router.shape[1]
    alpha = (2 * depth) ** 0.25
    q_end = attn_w
    k_end = q_end + kv_w
    v_end = k_end + kv_w
    cb_end = v_end + conv_wd
    cc_end = cb_end + conv_wd
    ch_end = cc_end + conv_wd
    ga_end = ch_end + d
    expert_tm = min(512, t)

    cos_t, sin_t = _rope_tables(seq)
    scale = HEAD_DIM ** -0.5 * math.log2(math.e)
    w_router_t = w_router.T
    bias_col = router_bias.reshape(n_exp, 1).astype(F32)
    ln1_g3, ln1_b3 = ln1_g.reshape(depth, 1, d), ln1_b.reshape(depth, 1, d)
    ln2_g3, ln2_b3 = ln2_g.reshape(depth, 1, d), ln2_b.reshape(depth, 1, d)

    xf = x.reshape(t, d)
    xb = xf.astype(BF16)
    for l in range(depth):
        p = _matmul(xb, w_in, l, BF16, name="in_proj")
        gains = jnp.concatenate([jnp.tile(q_norm_g[l] * scale, n_q), jnp.tile(k_norm_g[l], n_kv)]).reshape(1, k_end)
        qk = _qk_prep(p, gains.astype(F32), cos_t, sin_t, seq, k_end)
        attn = _attention(qk, p, batch, seq, n_kv, q_end // HEAD_DIM, k_end // HEAD_DIM)
        conv = _short_conv(p, conv_w, l, seq, v_end, cb_end, cc_end, conv_wd)
        merged = _merge(attn, conv, w_attn_proj, w_conv_proj, p, l, ch_end, ga_end)
        h1 = _matmul(merged, w_out, l, F32, res=xf, alpha=alpha, tm=1024, tn=512, name="out_proj")
        x1, x1p, idx, gate = _ln_route(h1, ln1_g3, ln1_b3, l, w_router_t, bias_col)
        slot, counts, pstarts, block_e, n_used, valid_rows, n_rows = _dispatch(idx[:TOP_K], n_exp, expert_tm)
        xs = _dispatch_rows(slot, counts, pstarts, n_used, x1p, n_rows, expert_tm)
        y = _experts(block_e, n_used, valid_rows, xs, w_gate, w_up, w_down, l, expert_tm)
        xf, xb = _combine_ln(slot, x1, y, gate.T, ln2_g3, ln2_b3, l, alpha)
    return xf.reshape(batch, seq, d)
```

```python
import functools
import math

import jax
import jax.numpy as jnp
from jax import lax
from jax.experimental import pallas as pl
from jax.experimental.pallas import tpu as pltpu

HEAD_DIM = 128
GQA_RATIO = 4
CONV_K = 3
GRID_W = 64
ROPE_THETA = 10000.0
N_GROUPS = 4
TOP_K = 2
RMS_EPS = 1e-6
LN_EPS = 1e-5
SUBLANES = 8
ONES_ROWS = 2 * SUBLANES
ISSUE_UNROLL = 8
VMEM_LIMIT = 56 * 1024 * 1024

F32 = jnp.float32
BF16 = jnp.bfloat16


def _params(semantics):
    return pltpu.CompilerParams(dimension_semantics=semantics, vmem_limit_bytes=VMEM_LIMIT)


def _tile(dim, pref):
    t = min(dim, pref)
    while dim % t:
        t //= 2
    return t


def _pack_bf16_pair(lo, hi):
    lo_b = lax.bitcast_convert_type(lo.astype(BF16).astype(F32), jnp.uint32) >> 16
    hi_b = lax.bitcast_convert_type(hi.astype(BF16).astype(F32), jnp.uint32) & jnp.uint32(0xFFFF0000)
    return hi_b | lo_b


def _unpack_bf16_pair(w):
    lo = lax.bitcast_convert_type(w << 16, F32)
    hi = lax.bitcast_convert_type(w & jnp.uint32(0xFFFF0000), F32)
    return lo, hi


def _mm_kernel(x_ref, w_ref, o_ref, wb_ref):
    @pl.when(pl.program_id(1) == 0)
    def _():
        wb_ref[...] = w_ref[...].astype(BF16)

    o_ref[...] = jnp.dot(x_ref[...], wb_ref[...], preferred_element_type=F32).astype(o_ref.dtype)


def _mm_res_kernel(x_ref, w_ref, r_ref, o_ref, wb_ref, *, alpha):
    @pl.when(pl.program_id(1) == 0)
    def _():
        wb_ref[...] = w_ref[...].astype(BF16)

    acc = jnp.dot(x_ref[...], wb_ref[...], preferred_element_type=F32)
    o_ref[...] = (alpha * r_ref[...] + acc).astype(o_ref.dtype)


def _matmul(x, w, layer, out_dtype, *, res=None, alpha=1.0, tm=512, tn=1024, name="matmul"):
    m, k = x.shape
    n = w.shape[2]
    tm, tn = _tile(m, tm), _tile(n, tn)
    in_specs = [
        pl.BlockSpec((tm, k), lambda j, i: (i, 0)),
        pl.BlockSpec((None, k, tn), lambda j, i: (layer, 0, j)),
    ]
    args = [x, w]
    body = _mm_kernel
    if res is not None:
        in_specs.append(pl.BlockSpec((tm, tn), lambda j, i: (i, j)))
        args.append(res)
        body = functools.partial(_mm_res_kernel, alpha=alpha)
    return pl.pallas_call(
        body,
        grid=(n // tn, m // tm),
        in_specs=in_specs,
        out_specs=pl.BlockSpec((tm, tn), lambda j, i: (i, j)),
        out_shape=jax.ShapeDtypeStruct((m, n), out_dtype),
        scratch_shapes=[pltpu.VMEM((k, tn), BF16)],
        compiler_params=_params(("parallel", "arbitrary")),
        name=name,
    )(*args)


def _qk_prep_kernel(p_ref, g_ref, c_ref, s_ref, ones_ref, perm_ref, o_ref, *, heads):
    cos = c_ref[...]
    sin = s_ref[...]
    ones = ones_ref[...]
    perm = perm_ref[...]
    for h in range(heads):
        sl = slice(h * HEAD_DIM, (h + 1) * HEAD_DIM)
        xh = p_ref[:, sl].astype(F32)
        ssq = jnp.dot((xh * xh).astype(BF16), ones, preferred_element_type=F32)
        y = xh * lax.rsqrt(ssq * (1.0 / HEAD_DIM) + RMS_EPS) * g_ref[:, sl]
        swapped = jnp.dot(y.astype(BF16), perm, preferred_element_type=F32)
        o_ref[:, sl] = (y * cos + swapped * sin).astype(o_ref.dtype)


def _qk_prep(p, gains, cos_t, sin_t, seq, qk_width, *, tm=512, heads=32):
    t = p.shape[0]
    tm = _tile(seq, tm)
    n_heads = qk_width // HEAD_DIM
    heads = _tile(n_heads, heads)
    wblk = heads * HEAD_DIM
    sblocks = seq // tm
    ones = jnp.ones((HEAD_DIM, HEAD_DIM), BF16)
    axis_dim, half_dim = HEAD_DIM // 2, HEAD_DIM // 4
    lane = jnp.arange(HEAD_DIM)
    partner = jnp.where((lane % axis_dim) < half_dim, lane + half_dim, lane - half_dim)
    perm = (lane[:, None] == partner[None, :]).astype(BF16)
    return pl.pallas_call(
        functools.partial(_qk_prep_kernel, heads=heads),
        grid=(t // tm, n_heads // heads),
        in_specs=[
            pl.BlockSpec((tm, wblk), lambda i, j: (i, j)),
            pl.BlockSpec((1, wblk), lambda i, j: (0, j)),
            pl.BlockSpec((tm, HEAD_DIM), lambda i, j: (i % sblocks, 0)),
            pl.BlockSpec((tm, HEAD_DIM), lambda i, j: (i % sblocks, 0)),
            pl.BlockSpec((HEAD_DIM, HEAD_DIM), lambda i, j: (0, 0)),
            pl.BlockSpec((HEAD_DIM, HEAD_DIM), lambda i, j: (0, 0)),
        ],
        out_specs=pl.BlockSpec((tm, wblk), lambda i, j: (i, j)),
        out_shape=jax.ShapeDtypeStruct((t, qk_width), BF16),
        compiler_params=_params(("parallel", "parallel")),
        name="qk_prep",
    )(p, gains, cos_t, sin_t, ones, perm)


def _flash_kernel(q_ref, k_ref, v_ref, o_ref, qt_ref, vt_ref, *stat_refs, tq, tk, cb, seq, unroll, ahead):
    nkv = seq // tk
    ncb = GQA_RATIO * tq // cb
    acc_refs, m_refs = stat_refs[:ncb], stat_refs[ncb:]

    @pl.when(pl.program_id(2) == 0)
    def _():
        for c in range(nkv):
            vt_ref[c, :HEAD_DIM, :] = v_ref[c * tk:(c + 1) * tk, :].T
            vt_ref[c, HEAD_DIM:, :] = jnp.ones((ONES_ROWS, tk), BF16)

    for g in range(GQA_RATIO):
        qt_ref[:, g * tq:(g + 1) * tq] = q_ref[:, g * HEAD_DIM:(g + 1) * HEAD_DIM].T
    for c in range(ncb):
        m_refs[c][...] = jnp.full(m_refs[c].shape, -jnp.inf, F32)
        acc_refs[c][...] = jnp.zeros(acc_refs[c].shape, F32)

    def body(j, carry):
        tiles = [(u, c) for u in range(unroll) for c in range(ncb)]
        ks, vts = [], []
        for u in range(unroll):
            off = pl.multiple_of((j * unroll + u) * tk, tk)
            ks.append(k_ref[pl.ds(off, tk), :])
            vts.append(vt_ref[j * unroll + u])

        def scores(t):
            u, c = tiles[t]
            return jnp.dot(ks[u], qt_ref[:, c * cb:(c + 1) * cb], preferred_element_type=F32)

        pending = [scores(t) for t in range(min(ahead, len(tiles)))]
        for t, (u, c) in enumerate(tiles):
            st = pending.pop(0)
            if t + ahead < len(tiles):
                pending.append(scores(t + ahead))
            m_old = m_refs[c][...]
            m_new = jnp.maximum(m_old, jnp.max(st, axis=0, keepdims=True))
            alpha = jnp.exp2(m_old - m_new)
            pt = jnp.exp2(st - m_new).astype(BF16)
            m_refs[c][...] = m_new
            acc_refs[c][...] = alpha * acc_refs[c][...] + jnp.dot(vts[u], pt, preferred_element_type=F32)
        return carry

    lax.fori_loop(0, nkv // unroll, body, 0)
    per_g = tq // cb
    for c in range(ncb):
        g, r = divmod(c, per_g)
        acc = acc_refs[c][...]
        out_t = acc[:HEAD_DIM, :] / acc[HEAD_DIM:HEAD_DIM + 1, :]
        o_ref[r * cb:(r + 1) * cb, g * HEAD_DIM:(g + 1) * HEAD_DIM] = out_t.T.astype(o_ref.dtype)


def _attention(qk, p, batch, seq, n_kv, k_col0, v_col0, *, tq=512, tk=256, cb=256, unroll=8, ahead=5):
    tq, tk = _tile(seq, tq), _tile(seq, tk)
    rows = GQA_RATIO * tq
    cb = _tile(rows, cb)
    unroll = _tile(seq // tk, unroll)
    qk3 = qk.reshape(batch, seq, qk.shape[1])
    p3 = p.reshape(batch, seq, p.shape[1])
    gw = GQA_RATIO * HEAD_DIM
    out = pl.pallas_call(
        functools.partial(_flash_kernel, tq=tq, tk=tk, cb=cb, seq=seq, unroll=unroll, ahead=ahead),
        grid=(batch, n_kv, seq // tq),
        in_specs=[
            pl.BlockSpec((None, tq, gw), lambda b, h, i: (b, i, h)),
            pl.BlockSpec((None, seq, HEAD_DIM), lambda b, h, i: (b, 0, k_col0 + h)),
            pl.BlockSpec((None, seq, HEAD_DIM), lambda b, h, i: (b, 0, v_col0 + h)),
        ],
        out_specs=pl.BlockSpec((None, tq, gw), lambda b, h, i: (b, i, h)),
        out_shape=jax.ShapeDtypeStruct((batch, seq, n_kv * gw), BF16),
        scratch_shapes=[
            pltpu.VMEM((HEAD_DIM, rows), BF16),
            pltpu.VMEM((seq // tk, HEAD_DIM + ONES_ROWS, tk), BF16),
        ] + [pltpu.VMEM((HEAD_DIM + ONES_ROWS, cb), F32)] * (rows // cb)
        + [pltpu.VMEM((1, cb), F32)] * (rows // cb),
        compiler_params=_params(("parallel", "parallel", "arbitrary")),
        name="flash_gqa",
    )(qk3, qk3, p3)
    return out.reshape(batch * seq, n_kv * gw)


def _conv_kernel(cb_ref, cc_ref, ch_ref, cbp_ref, chp_ref, cbn_ref, chn_ref, w_ref, o_ref, *, tm, sblocks):
    i = pl.program_id(0)
    first = (i % sblocks) == 0
    last = (i % sblocks) == sblocks - 1
    bx = cb_ref[...].astype(F32) * ch_ref[...].astype(F32)
    prev_row = cbp_ref[SUBLANES - 1:SUBLANES, :].astype(F32) * chp_ref[SUBLANES - 1:SUBLANES, :].astype(F32)
    next_row = cbn_ref[0:1, :].astype(F32) * chn_ref[0:1, :].astype(F32)
    prev_row = jnp.where(first, 0.0, prev_row)
    next_row = jnp.where(last, 0.0, next_row)
    row = lax.broadcasted_iota(jnp.int32, bx.shape, 0)
    prev = jnp.where(row == 0, prev_row, pltpu.roll(bx, 1, axis=0))
    nxt = jnp.where(row == tm - 1, next_row, pltpu.roll(bx, tm - 1, axis=0))
    w = w_ref[...]
    y = w[0:1, :] * prev + w[1:2, :] * bx + w[2:3, :] * nxt
    o_ref[...] = (cc_ref[...].astype(F32) * y).astype(o_ref.dtype)


def _short_conv(p, conv_w, layer, seq, cb0, cc0, ch0, width, *, tm=512, tc=1024):
    t = p.shape[0]
    tm, tc = _tile(seq, tm), math.gcd(cb0, cc0, ch0, width, tc)
    sblocks = seq // tm
    rpb = tm // SUBLANES
    nrb = t // SUBLANES

    def main(c0):
        return pl.BlockSpec((tm, tc), lambda i, j: (i, c0 // tc + j))

    def halo_prev(c0):
        return pl.BlockSpec((SUBLANES, tc), lambda i, j: (jnp.maximum(i * rpb - 1, 0), c0 // tc + j))

    def halo_next(c0):
        return pl.BlockSpec((SUBLANES, tc), lambda i, j: (jnp.minimum((i + 1) * rpb, nrb - 1), c0 // tc + j))

    return pl.pallas_call(
        functools.partial(_conv_kernel, tm=tm, sblocks=sblocks),
        grid=(t // tm, width // tc),
        in_specs=[main(cb0), main(cc0), main(ch0), halo_prev(cb0), halo_prev(ch0),
                  halo_next(cb0), halo_next(ch0),
                  pl.BlockSpec((None, CONV_K, tc), lambda i, j: (layer, 0, j))],
        out_specs=pl.BlockSpec((tm, tc), lambda i, j: (i, j)),
        out_shape=jax.ShapeDtypeStruct((t, width), BF16),
        compiler_params=_params(("parallel", "parallel")),
        name="short_conv",
    )(p, p, p, p, p, p, p, conv_w)


def _merge_kernel(a_ref, c_ref, wa_ref, wc_ref, ga_ref, gc_ref, o_ref, wab_ref, wcb_ref):
    @pl.when(pl.program_id(1) == 0)
    def _():
        wab_ref[...] = wa_ref[...].astype(BF16)
        wcb_ref[...] = wc_ref[...].astype(BF16)

    ya = jnp.dot(a_ref[...], wab_ref[...], preferred_element_type=F32)
    yc = jnp.dot(c_ref[...], wcb_ref[...], preferred_element_type=F32)
    o_ref[...] = (jax.nn.sigmoid(ga_ref[...].astype(F32)) * ya
                  + jax.nn.sigmoid(gc_ref[...].astype(F32)) * yc).astype(o_ref.dtype)


def _merge(attn, conv, wa, wc, p, layer, ga0, gc0, *, tm=1024, tn=512):
    t, ka = attn.shape
    kc = conv.shape[1]
    d = wa.shape[2]
    tm, tn = _tile(t, tm), math.gcd(ga0, gc0, d, tn)
    return pl.pallas_call(
        _merge_kernel,
        grid=(d // tn, t // tm),
        in_specs=[
            pl.BlockSpec((tm, ka), lambda j, i: (i, 0)),
            pl.BlockSpec((tm, kc), lambda j, i: (i, 0)),
            pl.BlockSpec((None, ka, tn), lambda j, i: (layer, 0, j)),
            pl.BlockSpec((None, kc, tn), lambda j, i: (layer, 0, j)),
            pl.BlockSpec((tm, tn), lambda j, i: (i, ga0 // tn + j)),
            pl.BlockSpec((tm, tn), lambda j, i: (i, gc0 // tn + j)),
        ],
        out_specs=pl.BlockSpec((tm, tn), lambda j, i: (i, j)),
        out_shape=jax.ShapeDtypeStruct((t, d), BF16),
        scratch_shapes=[pltpu.VMEM((ka, tn), BF16), pltpu.VMEM((kc, tn), BF16)],
        compiler_params=_params(("parallel", "arbitrary")),
        name="gated_merge",
    )(attn, conv, wa, wc, p, p)


def _ln_rows(h, g, b):
    mu = jnp.mean(h, axis=-1, keepdims=True)
    hc = h - mu
    var = jnp.mean(hc * hc, axis=-1, keepdims=True)
    return hc * lax.rsqrt(var + LN_EPS) * g + b


def _route(logits, bias, n_exp):
    epg = n_exp // N_GROUPS
    mx = jnp.max(logits, axis=0, keepdims=True)
    ex = jnp.exp(logits - mx)
    scores = ex / jnp.sum(ex, axis=0, keepdims=True)
    sel = scores + bias
    rows_sel = [sel[e:e + 1, :] for e in range(n_exp)]
    rows_sc = [scores[e:e + 1, :] for e in range(n_exp)]
    best = None
    for g in range(N_GROUPS):
        mem = rows_sel[g * epg:(g + 1) * epg]
        gs = None
        for a in range(epg):
            for c in range(a + 1, epg):
                pair = mem[a] + mem[c]
                gs = pair if gs is None else jnp.maximum(gs, pair)
        if best is None:
            best, grp = gs, jnp.zeros(gs.shape, jnp.int32)
        else:
            upd = gs > best
            best = jnp.where(upd, gs, best)
            grp = jnp.where(upd, g, grp)
    cand_sel, cand_sc = [], []
    for j in range(epg):
        cs, cc = rows_sel[j], rows_sc[j]
        for g in range(1, N_GROUPS):
            cs = jnp.where(grp == g, rows_sel[g * epg + j], cs)
            cc = jnp.where(grp == g, rows_sc[g * epg + j], cc)
        cand_sel.append(cs)
        cand_sc.append(cc)

    def first_argmax(vals, skip):
        bv = bi = bs = None
        for j in range(epg):
            v = vals[j] if skip is None else jnp.where(skip == j, -jnp.inf, vals[j])
            if bv is None:
                bv, bi, bs = v, jnp.zeros(v.shape, jnp.int32), cand_sc[0]
            else:
                upd = v > bv
                bv = jnp.where(upd, v, bv)
                bi = jnp.where(upd, j, bi)
                bs = jnp.where(upd, cand_sc[j], bs)
        return bi, bs

    i1, s1 = first_argmax(cand_sel, None)
    i2, s2 = first_argmax(cand_sel, i1)
    tot = s1 + s2
    zi = jnp.zeros((SUBLANES - TOP_K,) + i1.shape[1:], jnp.int32)
    zf = jnp.zeros((SUBLANES - TOP_K,) + i1.shape[1:], F32)
    idx = jnp.concatenate([grp * epg + i1, grp * epg + i2, zi], axis=0)
    gate = jnp.concatenate([s1 / tot, s2 / tot, zf], axis=0)
    return idx, gate


def _ln_route_kernel(h_ref, g_ref, b_ref, wh_ref, wl_ref, rb_ref, o_ref, op_ref, idx_ref, gate_ref, *, n_exp):
    y = _ln_rows(h_ref[...], g_ref[...], b_ref[...])
    half = y.shape[1] // 2
    o_ref[...] = y
    op_ref[...] = _pack_bf16_pair(y[:, :half], y[:, half:])
    yh = y.astype(BF16)
    yl = (y - yh.astype(F32)).astype(BF16)
    nt = (((1,), (1,)), ((), ()))
    logits = (lax.dot_general(wh_ref[...], yh, nt, preferred_element_type=F32)
              + lax.dot_general(wh_ref[...], yl, nt, preferred_element_type=F32)
              + lax.dot_general(wl_ref[...], yh, nt, preferred_element_type=F32))
    idx_ref[...], gate_ref[...] = _route(logits, rb_ref[...], n_exp)


def _ln_route(h, g, b, layer, w_router_t, bias_col, *, tm=256):
    t, d = h.shape
    n_exp = w_router_t.shape[0]
    tm = _tile(t, tm)
    wh = w_router_t.astype(BF16)
    wl = (w_router_t - wh.astype(F32)).astype(BF16)
    row = pl.BlockSpec((tm, d), lambda i: (i, 0))
    vec = pl.BlockSpec((None, 1, d), lambda i: (layer, 0, 0))
    wsp = pl.BlockSpec((n_exp, d), lambda i: (0, 0))
    rsp = pl.BlockSpec((SUBLANES, tm), lambda i: (0, i))
    return pl.pallas_call(
        functools.partial(_ln_route_kernel, n_exp=n_exp),
        grid=(t // tm,),
        in_specs=[row, vec, vec, wsp, wsp, pl.BlockSpec((n_exp, 1), lambda i: (0, 0))],
        out_specs=[row, pl.BlockSpec((tm, d // 2), lambda i: (i, 0)), rsp, rsp],
        out_shape=[jax.ShapeDtypeStruct((t, d), F32), jax.ShapeDtypeStruct((t, d // 2), jnp.uint32),
                   jax.ShapeDtypeStruct((SUBLANES, t), jnp.int32), jax.ShapeDtypeStruct((SUBLANES, t), F32)],
        compiler_params=_params(("parallel",)),
        name="ln_route",
    )(h, g, b, wh, wl, bias_col)


def _dispatch_kernel(slot_ref, cnt_ref, pst_ref, nu_ref, x_ref, o_hbm, z_ref, sem, *, tm, n_exp, epp, nblk):
    i = pl.program_id(0)
    n = pl.num_programs(0)

    @pl.when(i == 0)
    def _():
        z_ref[...] = jnp.zeros_like(z_ref)

    def issue(g, c):
        for u in range(ISSUE_UNROLL):
            r = g * ISSUE_UNROLL + u
            for k in range(TOP_K):
                pltpu.make_async_copy(x_ref.at[pl.ds(r, 1), :],
                                      o_hbm.at[pl.ds(slot_ref[(i * tm + r) * TOP_K + k], 1), :],
                                      sem.at[0]).start(priority=k % 2)
        return c

    lax.fori_loop(0, tm // ISSUE_UNROLL, issue, 0)

    def wait_rows(rows):
        pltpu.make_async_copy(z_ref.at[pl.ds(0, rows), :], o_hbm.at[pl.ds(0, rows), :], sem.at[0]).wait()

    for j in range(epp):
        e = i * epp + j

        @pl.when(e < n_exp)
        def _():
            cnt = cnt_ref[e]
            first = pst_ref[e] + cnt
            npad = (tm - cnt % tm) % tm

            def zero(r, c):
                pltpu.make_async_copy(z_ref.at[pl.ds(0, 1), :], o_hbm.at[pl.ds(first + r, 1), :], sem.at[0]).start()
                return c

            def wait_one(r, c):
                wait_rows(1)
                return c

            lax.fori_loop(0, npad, zero, 0)
            lax.fori_loop(0, npad, wait_one, 0)

    for k in range(TOP_K):
        wait_rows(tm)

    @pl.when(i == n - 1)
    def _():
        def zero_block(b, c):
            cp = pltpu.make_async_copy(z_ref, o_hbm.at[pl.ds(pl.multiple_of(b * tm, tm), tm), :], sem.at[0])
            cp.start()
            cp.wait()
            return c

        lax.fori_loop(nu_ref[0], nblk, zero_block, 0)


def _dispatch_rows(slot, counts, pstarts, n_used, xp, n_rows, tm):
    t, w = xp.shape
    n_exp = counts.shape[0]
    assert t % tm == 0 and n_rows % tm == 0
    n_steps = t // tm
    epp = -(-n_exp // n_steps)
    grid_spec = pltpu.PrefetchScalarGridSpec(
        num_scalar_prefetch=4,
        grid=(n_steps,),
        in_specs=[pl.BlockSpec((tm, w), lambda i, sl, cn, ps, nu: (i, 0))],
        out_specs=pl.BlockSpec(memory_space=pl.ANY),
        scratch_shapes=[pltpu.VMEM((tm, w), xp.dtype), pltpu.SemaphoreType.DMA((1,))],
    )
    return pl.pallas_call(
        functools.partial(_dispatch_kernel, tm=tm, n_exp=n_exp, epp=epp, nblk=n_rows // tm),
        grid_spec=grid_spec,
        out_shape=jax.ShapeDtypeStruct((n_rows, w), xp.dtype),
        compiler_params=_params(("arbitrary",)),
        name="dispatch_rows",
    )(slot, counts, pstarts, n_used, xp)


def _new_expert(be_ref, i):
    return jnp.logical_or(i == 0, be_ref[i] != be_ref[jnp.maximum(i - 1, 0)])


def _stage_weights(be_ref, nx_ref, lr_ref, copies, stage_refs, cache_refs):
    t = pl.program_id(0)
    i = pl.program_id(1)
    n_t = pl.num_programs(0)

    def start(e, tile):
        for cp in copies(e, tile):
            cp.start()

    @pl.when(jnp.logical_and(t == 0, i == 0))
    def _():
        start(be_ref[0], 0)

    @pl.when(_new_expert(be_ref, i))
    def _():
        for cp in copies(be_ref[i], t):
            cp.wait()
        for st, ca in zip(stage_refs, cache_refs):
            ca[...] = st[...].astype(BF16)
        last_run = lr_ref[i] == 1

        @pl.when(jnp.logical_not(last_run))
        def _():
            start(nx_ref[i], t)

        @pl.when(jnp.logical_and(last_run, t + 1 < n_t))
        def _():
            start(be_ref[0], t + 1)


def _expert_up_kernel(be_ref, nu_ref, nx_ref, lr_ref, x_ref, wg_hbm, wu_hbm, o_ref,
                      wgf_ref, wuf_ref, wgb_ref, wub_ref, sem, *, layer, tf):
    i = pl.program_id(1)

    def copies(e, tile):
        cols = pl.ds(pl.multiple_of(tile * tf, tf), tf)
        return (pltpu.make_async_copy(wg_hbm.at[layer, e, :, cols], wgf_ref, sem.at[0]),
                pltpu.make_async_copy(wu_hbm.at[layer, e, :, cols], wuf_ref, sem.at[1]))

    _stage_weights(be_ref, nx_ref, lr_ref, copies, (wgf_ref, wuf_ref), (wgb_ref, wub_ref))

    @pl.when(i < nu_ref[0])
    def _():
        lo, hi = _unpack_bf16_pair(x_ref[...])
        lo, hi = lo.astype(BF16), hi.astype(BF16)
        half = lo.shape[1]
        hg = (jnp.dot(lo, wgb_ref[:half, :], preferred_element_type=F32)
              + jnp.dot(hi, wgb_ref[half:, :], preferred_element_type=F32))
        hu = (jnp.dot(lo, wub_ref[:half, :], preferred_element_type=F32)
              + jnp.dot(hi, wub_ref[half:, :], preferred_element_type=F32))
        o_ref[...] = (hg * jax.nn.sigmoid(hg) * hu).astype(o_ref.dtype)

    @pl.when(i >= nu_ref[0])
    def _():
        o_ref[...] = jnp.zeros_like(o_ref)


def _expert_down_kernel(be_ref, nu_ref, nx_ref, lr_ref, h_ref, wd_hbm, o_ref,
                        wlof_ref, whif_ref, wlob_ref, whib_ref, sem, *, layer, tn, hi0):
    i = pl.program_id(1)

    def copies(e, tile):
        lo = pl.ds(pl.multiple_of(tile * tn, tn), tn)
        hi = pl.ds(pl.multiple_of((hi0 + tile) * tn, tn), tn)
        return (pltpu.make_async_copy(wd_hbm.at[layer, e, :, lo], wlof_ref, sem.at[0]),
                pltpu.make_async_copy(wd_hbm.at[layer, e, :, hi], whif_ref, sem.at[1]))

    _stage_weights(be_ref, nx_ref, lr_ref, copies, (wlof_ref, whif_ref), (wlob_ref, whib_ref))

    @pl.when(i < nu_ref[0])
    def _():
        h = h_ref[...]
        o_ref[...] = _pack_bf16_pair(jnp.dot(h, wlob_ref[...], preferred_element_type=F32),
                                     jnp.dot(h, whib_ref[...], preferred_element_type=F32))

    @pl.when(i >= nu_ref[0])
    def _():
        o_ref[...] = jnp.zeros_like(o_ref)


def _experts(block_e, n_used, next_e, last_run, xs, wg, wu, wd, layer, tm, *, tf=512, tn=1024):
    r = xs.shape[0]
    d = 2 * xs.shape[1]
    ff = wg.shape[3]
    tf = _tile(ff, tf)
    nblk = r // tm
    hbm = pl.BlockSpec(memory_space=pl.ANY)
    hid = pl.pallas_call(
        functools.partial(_expert_up_kernel, layer=layer, tf=tf),
        grid_spec=pltpu.PrefetchScalarGridSpec(
            num_scalar_prefetch=4,
            grid=(ff // tf, nblk),
            in_specs=[
                pl.BlockSpec((tm, d // 2), lambda f, i, be, nu, nx, lr: (jnp.minimum(i, nu[0] - 1), 0)),
                hbm, hbm,
            ],
            out_specs=pl.BlockSpec((tm, tf), lambda f, i, be, nu, nx, lr: (i, f)),
            scratch_shapes=[pltpu.VMEM((d, tf), F32), pltpu.VMEM((d, tf), F32),
                            pltpu.VMEM((d, tf), BF16), pltpu.VMEM((d, tf), BF16),
                            pltpu.SemaphoreType.DMA((2,))],
        ),
        out_shape=jax.ShapeDtypeStruct((r, ff), BF16),
        compiler_params=_params(("arbitrary", "arbitrary")),
        name="expert_up",
    )(block_e, n_used, next_e, last_run, xs, wg, wu)
    half = d // 2
    tn = _tile(half, tn)
    hi0 = half // tn
    return pl.pallas_call(
        functools.partial(_expert_down_kernel, layer=layer, tn=tn, hi0=hi0),
        grid_spec=pltpu.PrefetchScalarGridSpec(
            num_scalar_prefetch=4,
            grid=(half // tn, nblk),
            in_specs=[pl.BlockSpec((tm, ff), lambda n, i, be, nu, nx, lr: (i, 0)), hbm],
            out_specs=pl.BlockSpec((tm, tn), lambda n, i, be, nu, nx, lr: (i, n)),
            scratch_shapes=[pltpu.VMEM((ff, tn), F32), pltpu.VMEM((ff, tn), F32),
                            pltpu.VMEM((ff, tn), BF16), pltpu.VMEM((ff, tn), BF16),
                            pltpu.SemaphoreType.DMA((2,))],
        ),
        out_shape=jax.ShapeDtypeStruct((r, half), jnp.uint32),
        compiler_params=_params(("arbitrary", "arbitrary")),
        name="expert_down",
    )(block_e, n_used, next_e, last_run, hid, wd)


def _combine_kernel(slot_ref, x_ref, y_hbm, gt_ref, g_ref, b_ref, o_ref, ob_ref, ybuf, sem, *, alpha, tm):
    i = pl.program_id(0)
    n = pl.num_programs(0)

    def row_copy(row, s, k, r):
        return pltpu.make_async_copy(y_hbm.at[pl.ds(row, 1), :], ybuf.at[s, k, pl.ds(r, 1), :], sem.at[s])

    def start_gather(step, s):
        base = step * (tm * TOP_K)

        def issue(r, c):
            for k in range(TOP_K):
                row_copy(slot_ref[base + r * TOP_K + k], s, k, r).start(priority=k % 2)
            return c

        lax.fori_loop(0, tm, issue, 0, unroll=4)

    def wait_gather(s):
        for k in range(TOP_K):
            pltpu.make_async_copy(y_hbm.at[pl.ds(0, tm), :], ybuf.at[s, k], sem.at[s]).wait()

    @pl.when(i == 0)
    def _():
        start_gather(0, 0)

    @pl.when(i + 1 < n)
    def _():
        start_gather(i + 1, (i + 1) % 2)

    s = i % 2
    wait_gather(s)
    gt = gt_ref[...]
    a_lo, a_hi = _unpack_bf16_pair(ybuf[s, 0])
    b_lo, b_hi = _unpack_bf16_pair(ybuf[s, 1])
    g0, g1 = gt[:, 0:1], gt[:, 1:2]
    ffn = jnp.concatenate([g0 * a_lo + g1 * b_lo, g0 * a_hi + g1 * b_hi], axis=1)
    y = _ln_rows(alpha * x_ref[...] + ffn, g_ref[...], b_ref[...])
    o_ref[...] = y
    ob_ref[...] = y.astype(BF16)


def _combine_ln(slot, x, y, gates_t, g, b, layer, alpha, *, tm=256):
    t, d = x.shape
    tm = _tile(t, tm)
    row = pl.BlockSpec((tm, d), lambda i, sl: (i, 0))
    vec = pl.BlockSpec((None, 1, d), lambda i, sl: (layer, 0, 0))
    grid_spec = pltpu.PrefetchScalarGridSpec(
        num_scalar_prefetch=1,
        grid=(t // tm,),
        in_specs=[row, pl.BlockSpec(memory_space=pl.ANY),
                  pl.BlockSpec((tm, SUBLANES), lambda i, sl: (i, 0)), vec, vec],
        out_specs=[row, row],
        scratch_shapes=[pltpu.VMEM((2, TOP_K, tm, d // 2), jnp.uint32), pltpu.SemaphoreType.DMA((2,))],
    )
    return pl.pallas_call(
        functools.partial(_combine_kernel, alpha=alpha, tm=tm),
        grid_spec=grid_spec,
        out_shape=[jax.ShapeDtypeStruct((t, d), F32), jax.ShapeDtypeStruct((t, d), BF16)],
        compiler_params=_params(("arbitrary",)),
        name="combine_ln",
    )(slot, x, y, gates_t, g, b)


def _rope_tables(seq):
    axis_dim = HEAD_DIM // 2
    rows = seq // GRID_W
    inv = ROPE_THETA ** (-jnp.arange(0, axis_dim, 2, dtype=F32) / axis_dim)
    ang_r = jnp.repeat(jnp.arange(rows, dtype=F32), GRID_W)[:, None] * inv[None, :]
    ang_c = jnp.tile(jnp.arange(GRID_W, dtype=F32), rows)[:, None] * inv[None, :]
    cr, sr, cc, sc = jnp.cos(ang_r), jnp.sin(ang_r), jnp.cos(ang_c), jnp.sin(ang_c)
    cos_t = jnp.concatenate([cr, cr, cc, cc], axis=1)
    sin_t = jnp.concatenate([-sr, sr, -sc, sc], axis=1)
    return cos_t, sin_t


def _dispatch(idx, n_exp, tm):
    t = idx.shape[1]
    n_asg = t * TOP_K
    e_flat = idx.T.reshape(-1)
    onehot = (e_flat[:, None] == jnp.arange(n_exp, dtype=jnp.int32)[None, :]).astype(jnp.int32)
    csum = jnp.cumsum(onehot, axis=0)
    rank = jnp.sum(csum * onehot, axis=1) - 1
    counts = csum[-1]
    pcounts = (counts + tm - 1) // tm * tm
    pends = jnp.cumsum(pcounts)
    pstarts = pends - pcounts
    slot = (pstarts[e_flat] + rank).astype(jnp.int32)
    nblk = n_asg // tm + n_exp
    block_start = jnp.arange(nblk, dtype=jnp.int32) * tm
    block_e = jnp.minimum(jnp.sum(pends[None, :] <= block_start[:, None], axis=-1), n_exp - 1).astype(jnp.int32)
    n_used = (pends[-1] // tm).astype(jnp.int32).reshape(1)
    change_at = jnp.where(block_e[1:] != block_e[:-1], jnp.arange(1, nblk, dtype=jnp.int32), nblk)
    next_start = lax.cummin(jnp.concatenate([change_at, jnp.full((1,), nblk, jnp.int32)]), reverse=True)
    last_run = (next_start >= nblk).astype(jnp.int32)
    next_e = block_e[jnp.minimum(next_start, nblk - 1)]
    return (slot, counts.astype(jnp.int32), pstarts.astype(jnp.int32), block_e, n_used, next_e, last_run,
            nblk * tm)


def kernel(x, w_in, q_norm_g, k_norm_g, conv_w, w_attn_proj, w_conv_proj, w_out, ln1_g, ln1_b,
           w_router, router_bias, w_gate, w_up, w_down, ln2_g, ln2_b):
    batch, seq, d = x.shape
    depth = w_in.shape[0]
    t = batch * seq
    attn_w = d // 2
    n_q = attn_w // HEAD_DIM
    n_kv = n_q // GQA_RATIO
    kv_w = n_kv * HEAD_DIM
    conv_wd = d // 2
    n_exp = w_router.shape[1]
    alpha = (2 * depth) ** 0.25
    q_end = attn_w
    k_end = q_end + kv_w
    v_end = k_end + kv_w
    cb_end = v_end + conv_wd
    cc_end = cb_end + conv_wd
    ch_end = cc_end + conv_wd
    ga_end = ch_end + d
    expert_tm = min(512, t)

    cos_t, sin_t = _rope_tables(seq)
    scale = HEAD_DIM ** -0.5 * math.log2(math.e)
    w_router_t = w_router.T
    bias_col = router_bias.reshape(n_exp, 1).astype(F32)
    ln1_g3, ln1_b3 = ln1_g.reshape(depth, 1, d), ln1_b.reshape(depth, 1, d)
    ln2_g3, ln2_b3 = ln2_g.reshape(depth, 1, d), ln2_b.reshape(depth, 1, d)

    xf = x.reshape(t, d)
    xb = xf.astype(BF16)
    for l in range(depth):
        p = _matmul(xb, w_in, l, BF16, name="in_proj")
        gains = jnp.concatenate([jnp.tile(q_norm_g[l] * scale, n_q), jnp.tile(k_norm_g[l], n_kv)]).reshape(1, k_end)
        qk = _qk_prep(p, gains.astype(F32), cos_t, sin_t, seq, k_end)
        attn = _attention(qk, p, batch, seq, n_kv, q_end // HEAD_DIM, k_end // HEAD_DIM)
        conv = _short_conv(p, conv_w, l, seq, v_end, cb_end, cc_end, conv_wd)
        merged = _merge(attn, conv, w_attn_proj, w_conv_proj, p, l, ch_end, ga_end)
        h1 = _matmul(merged, w_out, l, F32, res=xf, alpha=alpha, tm=1024, tn=512, name="out_proj")
        x1, x1p, idx, gate = _ln_route(h1, ln1_g3, ln1_b3, l, w_router_t, bias_col)
        slot, counts, pstarts, block_e, n_used, next_e, last_run, n_rows = _dispatch(idx[:TOP_K], n_exp, expert_tm)
        xs = _dispatch_rows(slot, counts, pstarts, n_used, x1p, n_rows, expert_tm)
        y = _experts(block_e, n_used, next_e, last_run, xs, w_gate, w_up, w_down, l, expert_tm)
        xf, xb = _combine_ln(slot, x1, y, gate.T, ln2_g3, ln2_b3, l, alpha)
    return xf.reshape(batch, seq, d)
```

```python
import functools
import math

import jax
import jax.numpy as jnp
from jax import lax
from jax.experimental import pallas as pl
from jax.experimental.pallas import tpu as pltpu

HEAD_DIM = 128
GQA_RATIO = 4
CONV_K = 3
GRID_W = 64
ROPE_THETA = 10000.0
N_GROUPS = 4
TOP_K = 2
RMS_EPS = 1e-6
LN_EPS = 1e-5
SUBLANES = 8
ONES_ROWS = 2 * SUBLANES
ISSUE_UNROLL = 8
VMEM_LIMIT = 56 * 1024 * 1024

F32 = jnp.float32
BF16 = jnp.bfloat16


def _params(semantics):
    return pltpu.CompilerParams(dimension_semantics=semantics, vmem_limit_bytes=VMEM_LIMIT)


def _tile(dim, pref):
    t = min(dim, pref)
    while dim % t:
        t //= 2
    return t


def _pack_bf16_pair(lo, hi):
    lo_b = lax.bitcast_convert_type(lo.astype(BF16).astype(F32), jnp.uint32) >> 16
    hi_b = lax.bitcast_convert_type(hi.astype(BF16).astype(F32), jnp.uint32) & jnp.uint32(0xFFFF0000)
    return hi_b | lo_b


def _unpack_bf16_pair(w):
    lo = lax.bitcast_convert_type(w << 16, F32)
    hi = lax.bitcast_convert_type(w & jnp.uint32(0xFFFF0000), F32)
    return lo, hi


def _mm_kernel(x_ref, w_ref, o_ref, wb_ref):
    @pl.when(pl.program_id(1) == 0)
    def _():
        wb_ref[...] = w_ref[...].astype(BF16)

    o_ref[...] = jnp.dot(x_ref[...], wb_ref[...], preferred_element_type=F32).astype(o_ref.dtype)


def _mm_res_kernel(x_ref, w_ref, r_ref, o_ref, wb_ref, *, alpha):
    @pl.when(pl.program_id(1) == 0)
    def _():
        wb_ref[...] = w_ref[...].astype(BF16)

    acc = jnp.dot(x_ref[...], wb_ref[...], preferred_element_type=F32)
    o_ref[...] = (alpha * r_ref[...] + acc).astype(o_ref.dtype)


def _matmul(x, w, layer, out_dtype, *, res=None, alpha=1.0, tm=512, tn=1024, name="matmul"):
    m, k = x.shape
    n = w.shape[2]
    tm, tn = _tile(m, tm), _tile(n, tn)
    in_specs = [
        pl.BlockSpec((tm, k), lambda j, i: (i, 0)),
        pl.BlockSpec((None, k, tn), lambda j, i: (layer, 0, j)),
    ]
    args = [x, w]
    body = _mm_kernel
    if res is not None:
        in_specs.append(pl.BlockSpec((tm, tn), lambda j, i: (i, j)))
        args.append(res)
        body = functools.partial(_mm_res_kernel, alpha=alpha)
    return pl.pallas_call(
        body,
        grid=(n // tn, m // tm),
        in_specs=in_specs,
        out_specs=pl.BlockSpec((tm, tn), lambda j, i: (i, j)),
        out_shape=jax.ShapeDtypeStruct((m, n), out_dtype),
        scratch_shapes=[pltpu.VMEM((k, tn), BF16)],
        compiler_params=_params(("parallel", "arbitrary")),
        name=name,
    )(*args)


def _qk_prep_kernel(p_ref, g_ref, c_ref, s_ref, ones_ref, perm_ref, o_ref, *, heads):
    cos = c_ref[...]
    sin = s_ref[...]
    ones = ones_ref[...]
    perm = perm_ref[...]
    for h in range(heads):
        sl = slice(h * HEAD_DIM, (h + 1) * HEAD_DIM)
        xh = p_ref[:, sl].astype(F32)
        ssq = jnp.dot((xh * xh).astype(BF16), ones, preferred_element_type=F32)
        y = xh * lax.rsqrt(ssq * (1.0 / HEAD_DIM) + RMS_EPS) * g_ref[:, sl]
        swapped = jnp.dot(y.astype(BF16), perm, preferred_element_type=F32)
        o_ref[:, sl] = (y * cos + swapped * sin).astype(o_ref.dtype)


def _qk_prep(p, gains, cos_t, sin_t, seq, qk_width, *, tm=512, heads=32):
    t = p.shape[0]
    tm = _tile(seq, tm)
    n_heads = qk_width // HEAD_DIM
    heads = _tile(n_heads, heads)
    wblk = heads * HEAD_DIM
    sblocks = seq // tm
    ones = jnp.ones((HEAD_DIM, HEAD_DIM), BF16)
    axis_dim, half_dim = HEAD_DIM // 2, HEAD_DIM // 4
    lane = jnp.arange(HEAD_DIM)
    partner = jnp.where((lane % axis_dim) < half_dim, lane + half_dim, lane - half_dim)
    perm = (lane[:, None] == partner[None, :]).astype(BF16)
    return pl.pallas_call(
        functools.partial(_qk_prep_kernel, heads=heads),
        grid=(t // tm, n_heads // heads),
        in_specs=[
            pl.BlockSpec((tm, wblk), lambda i, j: (i, j)),
            pl.BlockSpec((1, wblk), lambda i, j: (0, j)),
            pl.BlockSpec((tm, HEAD_DIM), lambda i, j: (i % sblocks, 0)),
            pl.BlockSpec((tm, HEAD_DIM), lambda i, j: (i % sblocks, 0)),
            pl.BlockSpec((HEAD_DIM, HEAD_DIM), lambda i, j: (0, 0)),
            pl.BlockSpec((HEAD_DIM, HEAD_DIM), lambda i, j: (0, 0)),
        ],
        out_specs=pl.BlockSpec((tm, wblk), lambda i, j: (i, j)),
        out_shape=jax.ShapeDtypeStruct((t, qk_width), BF16),
        compiler_params=_params(("parallel", "parallel")),
        name="qk_prep",
    )(p, gains, cos_t, sin_t, ones, perm)


def _flash_kernel(q_ref, k_ref, v_ref, o_ref, qt_ref, vt_ref, *stat_refs, tq, tk, cb, seq, unroll, ahead):
    nkv = seq // tk
    ncb = GQA_RATIO * tq // cb
    acc_refs, m_refs = stat_refs[:ncb], stat_refs[ncb:]

    @pl.when(pl.program_id(2) == 0)
    def _():
        for c in range(nkv):
            vt_ref[c, :HEAD_DIM, :] = v_ref[c * tk:(c + 1) * tk, :].T
            vt_ref[c, HEAD_DIM:, :] = jnp.ones((ONES_ROWS, tk), BF16)

    for g in range(GQA_RATIO):
        qt_ref[:, g * tq:(g + 1) * tq] = q_ref[:, g * HEAD_DIM:(g + 1) * HEAD_DIM].T
    for c in range(ncb):
        m_refs[c][...] = jnp.full(m_refs[c].shape, -jnp.inf, F32)
        acc_refs[c][...] = jnp.zeros(acc_refs[c].shape, F32)

    def body(j, carry):
        tiles = [(u, c) for u in range(unroll) for c in range(ncb)]
        ks, vts = [], []
        for u in range(unroll):
            off = pl.multiple_of((j * unroll + u) * tk, tk)
            ks.append(k_ref[pl.ds(off, tk), :])
            vts.append(vt_ref[j * unroll + u])

        def scores(t):
            u, c = tiles[t]
            return jnp.dot(ks[u], qt_ref[:, c * cb:(c + 1) * cb], preferred_element_type=F32)

        pending = [scores(t) for t in range(min(ahead, len(tiles)))]
        for t, (u, c) in enumerate(tiles):
            st = pending.pop(0)
            if t + ahead < len(tiles):
                pending.append(scores(t + ahead))
            m_old = m_refs[c][...]
            m_new = jnp.maximum(m_old, jnp.max(st, axis=0, keepdims=True))
            alpha = jnp.exp2(m_old - m_new)
            pt = jnp.exp2(st - m_new).astype(BF16)
            m_refs[c][...] = m_new
            acc_refs[c][...] = alpha * acc_refs[c][...] + jnp.dot(vts[u], pt, preferred_element_type=F32)
        return carry

    lax.fori_loop(0, nkv // unroll, body, 0)
    per_g = tq // cb
    for c in range(ncb):
        g, r = divmod(c, per_g)
        acc = acc_refs[c][...]
        out_t = acc[:HEAD_DIM, :] / acc[HEAD_DIM:HEAD_DIM + 1, :]
        o_ref[r * cb:(r + 1) * cb, g * HEAD_DIM:(g + 1) * HEAD_DIM] = out_t.T.astype(o_ref.dtype)


def _attention(qk, p, batch, seq, n_kv, k_col0, v_col0, *, tq=512, tk=256, cb=256, unroll=8, ahead=5):
    tq, tk = _tile(seq, tq), _tile(seq, tk)
    rows = GQA_RATIO * tq
    cb = _tile(rows, cb)
    unroll = _tile(seq // tk, unroll)
    qk3 = qk.reshape(batch, seq, qk.shape[1])
    p3 = p.reshape(batch, seq, p.shape[1])
    gw = GQA_RATIO * HEAD_DIM
    out = pl.pallas_call(
        functools.partial(_flash_kernel, tq=tq, tk=tk, cb=cb, seq=seq, unroll=unroll, ahead=ahead),
        grid=(batch, n_kv, seq // tq),
        in_specs=[
            pl.BlockSpec((None, tq, gw), lambda b, h, i: (b, i, h)),
            pl.BlockSpec((None, seq, HEAD_DIM), lambda b, h, i: (b, 0, k_col0 + h)),
            pl.BlockSpec((None, seq, HEAD_DIM), lambda b, h, i: (b, 0, v_col0 + h)),
        ],
        out_specs=pl.BlockSpec((None, tq, gw), lambda b, h, i: (b, i, h)),
        out_shape=jax.ShapeDtypeStruct((batch, seq, n_kv * gw), BF16),
        scratch_shapes=[
            pltpu.VMEM((HEAD_DIM, rows), BF16),
            pltpu.VMEM((seq // tk, HEAD_DIM + ONES_ROWS, tk), BF16),
        ] + [pltpu.VMEM((HEAD_DIM + ONES_ROWS, cb), F32)] * (rows // cb)
        + [pltpu.VMEM((1, cb), F32)] * (rows // cb),
        compiler_params=_params(("parallel", "parallel", "arbitrary")),
        name="flash_gqa",
    )(qk3, qk3, p3)
    return out.reshape(batch * seq, n_kv * gw)


def _conv_kernel(cb_ref, cc_ref, ch_ref, cbp_ref, chp_ref, cbn_ref, chn_ref, w_ref, o_ref, *, tm, sblocks):
    i = pl.program_id(0)
    first = (i % sblocks) == 0
    last = (i % sblocks) == sblocks - 1
    bx = cb_ref[...].astype(F32) * ch_ref[...].astype(F32)
    prev_row = cbp_ref[SUBLANES - 1:SUBLANES, :].astype(F32) * chp_ref[SUBLANES - 1:SUBLANES, :].astype(F32)
    next_row = cbn_ref[0:1, :].astype(F32) * chn_ref[0:1, :].astype(F32)
    prev_row = jnp.where(first, 0.0, prev_row)
    next_row = jnp.where(last, 0.0, next_row)
    row = lax.broadcasted_iota(jnp.int32, bx.shape, 0)
    prev = jnp.where(row == 0, prev_row, pltpu.roll(bx, 1, axis=0))
    nxt = jnp.where(row == tm - 1, next_row, pltpu.roll(bx, tm - 1, axis=0))
    w = w_ref[...]
    y = w[0:1, :] * prev + w[1:2, :] * bx + w[2:3, :] * nxt
    o_ref[...] = (cc_ref[...].astype(F32) * y).astype(o_ref.dtype)


def _short_conv(p, conv_w, layer, seq, cb0, cc0, ch0, width, *, tm=512, tc=1024):
    t = p.shape[0]
    tm, tc = _tile(seq, tm), math.gcd(cb0, cc0, ch0, width, tc)
    sblocks = seq // tm
    rpb = tm // SUBLANES
    nrb = t // SUBLANES

    def main(c0):
        return pl.BlockSpec((tm, tc), lambda i, j: (i, c0 // tc + j))

    def halo_prev(c0):
        return pl.BlockSpec((SUBLANES, tc), lambda i, j: (jnp.maximum(i * rpb - 1, 0), c0 // tc + j))

    def halo_next(c0):
        return pl.BlockSpec((SUBLANES, tc), lambda i, j: (jnp.minimum((i + 1) * rpb, nrb - 1), c0 // tc + j))

    return pl.pallas_call(
        functools.partial(_conv_kernel, tm=tm, sblocks=sblocks),
        grid=(t // tm, width // tc),
        in_specs=[main(cb0), main(cc0), main(ch0), halo_prev(cb0), halo_prev(ch0),
                  halo_next(cb0), halo_next(ch0),
                  pl.BlockSpec((None, CONV_K, tc), lambda i, j: (layer, 0, j))],
        out_specs=pl.BlockSpec((tm, tc), lambda i, j: (i, j)),
        out_shape=jax.ShapeDtypeStruct((t, width), BF16),
        compiler_params=_params(("parallel", "parallel")),
        name="short_conv",
    )(p, p, p, p, p, p, p, conv_w)


def _merge_kernel(a_ref, c_ref, wa_ref, wc_ref, ga_ref, gc_ref, o_ref, wab_ref, wcb_ref):
    @pl.when(pl.program_id(1) == 0)
    def _():
        wab_ref[...] = wa_ref[...].astype(BF16)
        wcb_ref[...] = wc_ref[...].astype(BF16)

    ya = jnp.dot(a_ref[...], wab_ref[...], preferred_element_type=F32)
    yc = jnp.dot(c_ref[...], wcb_ref[...], preferred_element_type=F32)
    o_ref[...] = (jax.nn.sigmoid(ga_ref[...].astype(F32)) * ya
                  + jax.nn.sigmoid(gc_ref[...].astype(F32)) * yc).astype(o_ref.dtype)


def _merge(attn, conv, wa, wc, p, layer, ga0, gc0, *, tm=1024, tn=512):
    t, ka = attn.shape
    kc = conv.shape[1]
    d = wa.shape[2]
    tm, tn = _tile(t, tm), math.gcd(ga0, gc0, d, tn)
    return pl.pallas_call(
        _merge_kernel,
        grid=(d // tn, t // tm),
        in_specs=[
            pl.BlockSpec((tm, ka), lambda j, i: (i, 0)),
            pl.BlockSpec((tm, kc), lambda j, i: (i, 0)),
            pl.BlockSpec((None, ka, tn), lambda j, i: (layer, 0, j)),
            pl.BlockSpec((None, kc, tn), lambda j, i: (layer, 0, j)),
            pl.BlockSpec((tm, tn), lambda j, i: (i, ga0 // tn + j)),
            pl.BlockSpec((tm, tn), lambda j, i: (i, gc0 // tn + j)),
        ],
        out_specs=pl.BlockSpec((tm, tn), lambda j, i: (i, j)),
        out_shape=jax.ShapeDtypeStruct((t, d), BF16),
        scratch_shapes=[pltpu.VMEM((ka, tn), BF16), pltpu.VMEM((kc, tn), BF16)],
        compiler_params=_params(("parallel", "arbitrary")),
        name="gated_merge",
    )(attn, conv, wa, wc, p, p)


def _ln_rows(h, g, b):
    mu = jnp.mean(h, axis=-1, keepdims=True)
    hc = h - mu
    var = jnp.mean(hc * hc, axis=-1, keepdims=True)
    return hc * lax.rsqrt(var + LN_EPS) * g + b


def _route(logits, bias, n_exp):
    epg = n_exp // N_GROUPS
    mx = jnp.max(logits, axis=0, keepdims=True)
    ex = jnp.exp(logits - mx)
    scores = ex / jnp.sum(ex, axis=0, keepdims=True)
    sel = scores + bias
    rows_sel = [sel[e:e + 1, :] for e in range(n_exp)]
    rows_sc = [scores[e:e + 1, :] for e in range(n_exp)]
    best = None
    for g in range(N_GROUPS):
        mem = rows_sel[g * epg:(g + 1) * epg]
        gs = None
        for a in range(epg):
            for c in range(a + 1, epg):
                pair = mem[a] + mem[c]
                gs = pair if gs is None else jnp.maximum(gs, pair)
        if best is None:
            best, grp = gs, jnp.zeros(gs.shape, jnp.int32)
        else:
            upd = gs > best
            best = jnp.where(upd, gs, best)
            grp = jnp.where(upd, g, grp)
    cand_sel, cand_sc = [], []
    for j in range(epg):
        cs, cc = rows_sel[j], rows_sc[j]
        for g in range(1, N_GROUPS):
            cs = jnp.where(grp == g, rows_sel[g * epg + j], cs)
            cc = jnp.where(grp == g, rows_sc[g * epg + j], cc)
        cand_sel.append(cs)
        cand_sc.append(cc)

    def first_argmax(vals, skip):
        bv = bi = bs = None
        for j in range(epg):
            v = vals[j] if skip is None else jnp.where(skip == j, -jnp.inf, vals[j])
            if bv is None:
                bv, bi, bs = v, jnp.zeros(v.shape, jnp.int32), cand_sc[0]
            else:
                upd = v > bv
                bv = jnp.where(upd, v, bv)
                bi = jnp.where(upd, j, bi)
                bs = jnp.where(upd, cand_sc[j], bs)
        return bi, bs

    i1, s1 = first_argmax(cand_sel, None)
    i2, s2 = first_argmax(cand_sel, i1)
    tot = s1 + s2
    zi = jnp.zeros((SUBLANES - TOP_K,) + i1.shape[1:], jnp.int32)
    zf = jnp.zeros((SUBLANES - TOP_K,) + i1.shape[1:], F32)
    idx = jnp.concatenate([grp * epg + i1, grp * epg + i2, zi], axis=0)
    gate = jnp.concatenate([s1 / tot, s2 / tot, zf], axis=0)
    return idx, gate


def _ln_route_kernel(h_ref, g_ref, b_ref, wh_ref, wl_ref, rb_ref, o_ref, op_ref, idx_ref, gate_ref, *, n_exp):
    y = _ln_rows(h_ref[...], g_ref[...], b_ref[...])
    half = y.shape[1] // 2
    o_ref[...] = y
    op_ref[...] = _pack_bf16_pair(y[:, :half], y[:, half:])
    yh = y.astype(BF16)
    yl = (y - yh.astype(F32)).astype(BF16)
    nt = (((1,), (1,)), ((), ()))
    logits = (lax.dot_general(wh_ref[...], yh, nt, preferred_element_type=F32)
              + lax.dot_general(wh_ref[...], yl, nt, preferred_element_type=F32)
              + lax.dot_general(wl_ref[...], yh, nt, preferred_element_type=F32))
    idx_ref[...], gate_ref[...] = _route(logits, rb_ref[...], n_exp)


def _ln_route(h, g, b, layer, w_router_t, bias_col, *, tm=256):
    t, d = h.shape
    n_exp = w_router_t.shape[0]
    tm = _tile(t, tm)
    wh = w_router_t.astype(BF16)
    wl = (w_router_t - wh.astype(F32)).astype(BF16)
    row = pl.BlockSpec((tm, d), lambda i: (i, 0))
    vec = pl.BlockSpec((None, 1, d), lambda i: (layer, 0, 0))
    wsp = pl.BlockSpec((n_exp, d), lambda i: (0, 0))
    rsp = pl.BlockSpec((SUBLANES, tm), lambda i: (0, i))
    return pl.pallas_call(
        functools.partial(_ln_route_kernel, n_exp=n_exp),
        grid=(t // tm,),
        in_specs=[row, vec, vec, wsp, wsp, pl.BlockSpec((n_exp, 1), lambda i: (0, 0))],
        out_specs=[row, pl.BlockSpec((tm, d // 2), lambda i: (i, 0)), rsp, rsp],
        out_shape=[jax.ShapeDtypeStruct((t, d), F32), jax.ShapeDtypeStruct((t, d // 2), jnp.uint32),
                   jax.ShapeDtypeStruct((SUBLANES, t), jnp.int32), jax.ShapeDtypeStruct((SUBLANES, t), F32)],
        compiler_params=_params(("parallel",)),
        name="ln_route",
    )(h, g, b, wh, wl, bias_col)


def _dispatch_kernel(slot_ref, cnt_ref, pst_ref, nu_ref, x_ref, o_hbm, z_ref, sem, *, tm, n_exp, epp, nblk):
    i = pl.program_id(0)
    n = pl.num_programs(0)

    @pl.when(i == 0)
    def _():
        z_ref[...] = jnp.zeros_like(z_ref)

    def issue(g, c):
        for u in range(ISSUE_UNROLL):
            r = g * ISSUE_UNROLL + u
            for k in range(TOP_K):
                pltpu.make_async_copy(x_ref.at[pl.ds(r, 1), :],
                                      o_hbm.at[pl.ds(slot_ref[(i * tm + r) * TOP_K + k], 1), :],
                                      sem.at[0]).start(priority=k % 2)
        return c

    lax.fori_loop(0, tm // ISSUE_UNROLL, issue, 0)

    def wait_rows(rows):
        pltpu.make_async_copy(z_ref.at[pl.ds(0, rows), :], o_hbm.at[pl.ds(0, rows), :], sem.at[0]).wait()

    for j in range(epp):
        e = i * epp + j

        @pl.when(e < n_exp)
        def _():
            cnt = cnt_ref[e]
            first = pst_ref[e] + cnt
            npad = (tm - cnt % tm) % tm

            def zero(r, c):
                pltpu.make_async_copy(z_ref.at[pl.ds(0, 1), :], o_hbm.at[pl.ds(first + r, 1), :], sem.at[0]).start()
                return c

            def wait_one(r, c):
                wait_rows(1)
                return c

            lax.fori_loop(0, npad, zero, 0)
            lax.fori_loop(0, npad, wait_one, 0)

    for k in range(TOP_K):
        wait_rows(tm)

    @pl.when(i == n - 1)
    def _():
        def zero_block(b, c):
            cp = pltpu.make_async_copy(z_ref, o_hbm.at[pl.ds(pl.multiple_of(b * tm, tm), tm), :], sem.at[0])
            cp.start()
            cp.wait()
            return c

        lax.fori_loop(nu_ref[0], nblk, zero_block, 0)


def _dispatch_rows(slot, counts, pstarts, n_used, xp, n_rows, tm):
    t, w = xp.shape
    n_exp = counts.shape[0]
    assert t % tm == 0 and n_rows % tm == 0
    n_steps = t // tm
    epp = -(-n_exp // n_steps)
    grid_spec = pltpu.PrefetchScalarGridSpec(
        num_scalar_prefetch=4,
        grid=(n_steps,),
        in_specs=[pl.BlockSpec((tm, w), lambda i, sl, cn, ps, nu: (i, 0))],
        out_specs=pl.BlockSpec(memory_space=pl.ANY),
        scratch_shapes=[pltpu.VMEM((tm, w), xp.dtype), pltpu.SemaphoreType.DMA((1,))],
    )
    return pl.pallas_call(
        functools.partial(_dispatch_kernel, tm=tm, n_exp=n_exp, epp=epp, nblk=n_rows // tm),
        grid_spec=grid_spec,
        out_shape=jax.ShapeDtypeStruct((n_rows, w), xp.dtype),
        compiler_params=_params(("arbitrary",)),
        name="dispatch_rows",
    )(slot, counts, pstarts, n_used, xp)


def _new_expert(be_ref, i):
    return jnp.logical_or(i == 0, be_ref[i] != be_ref[jnp.maximum(i - 1, 0)])


def _stage_weights(be_ref, nx_ref, lr_ref, copies, stage_refs, cache_refs):
    t = pl.program_id(0)
    i = pl.program_id(1)
    n_t = pl.num_programs(0)

    def start(e, tile):
        for cp in copies(e, tile):
            cp.start()

    @pl.when(jnp.logical_and(t == 0, i == 0))
    def _():
        start(be_ref[0], 0)

    @pl.when(_new_expert(be_ref, i))
    def _():
        for cp in copies(be_ref[i], t):
            cp.wait()
        for st, ca in zip(stage_refs, cache_refs):
            ca[...] = st[...].astype(BF16)
        last_run = lr_ref[i] == 1

        @pl.when(jnp.logical_not(last_run))
        def _():
            start(nx_ref[i], t)

        @pl.when(jnp.logical_and(last_run, t + 1 < n_t))
        def _():
            start(be_ref[0], t + 1)


def _half_block_compute(valid, compute, o_ref):
    tm = o_ref.shape[0]
    hb = tm // 2

    @pl.when(valid > hb)
    def _():
        o_ref[...] = compute(slice(None))

    @pl.when(jnp.logical_and(valid > 0, valid <= hb))
    def _():
        o_ref[:hb, :] = compute(slice(0, hb))
        o_ref[hb:, :] = jnp.zeros((tm - hb, o_ref.shape[1]), o_ref.dtype)

    @pl.when(valid == 0)
    def _():
        o_ref[...] = jnp.zeros_like(o_ref)


def _expert_up_kernel(be_ref, nu_ref, nx_ref, lr_ref, vr_ref, x_ref, wg_hbm, wu_hbm, o_ref,
                      wgf_ref, wuf_ref, wgb_ref, wub_ref, sem, *, layer, tf):
    i = pl.program_id(1)

    def copies(e, tile):
        cols = pl.ds(pl.multiple_of(tile * tf, tf), tf)
        return (pltpu.make_async_copy(wg_hbm.at[layer, e, :, cols], wgf_ref, sem.at[0]),
                pltpu.make_async_copy(wu_hbm.at[layer, e, :, cols], wuf_ref, sem.at[1]))

    _stage_weights(be_ref, nx_ref, lr_ref, copies, (wgf_ref, wuf_ref), (wgb_ref, wub_ref))

    def hidden(rows):
        lo, hi = _unpack_bf16_pair(x_ref[rows, :])
        lo, hi = lo.astype(BF16), hi.astype(BF16)
        half = lo.shape[1]
        hg = (jnp.dot(lo, wgb_ref[:half, :], preferred_element_type=F32)
              + jnp.dot(hi, wgb_ref[half:, :], preferred_element_type=F32))
        hu = (jnp.dot(lo, wub_ref[:half, :], preferred_element_type=F32)
              + jnp.dot(hi, wub_ref[half:, :], preferred_element_type=F32))
        return (hg * jax.nn.sigmoid(hg) * hu).astype(o_ref.dtype)

    _half_block_compute(vr_ref[i], hidden, o_ref)


def _expert_down_kernel(be_ref, nu_ref, nx_ref, lr_ref, vr_ref, h_ref, wd_hbm, o_ref,
                        wlof_ref, whif_ref, wlob_ref, whib_ref, sem, *, layer, tn, hi0):
    i = pl.program_id(1)

    def copies(e, tile):
        lo = pl.ds(pl.multiple_of(tile * tn, tn), tn)
        hi = pl.ds(pl.multiple_of((hi0 + tile) * tn, tn), tn)
        return (pltpu.make_async_copy(wd_hbm.at[layer, e, :, lo], wlof_ref, sem.at[0]),
                pltpu.make_async_copy(wd_hbm.at[layer, e, :, hi], whif_ref, sem.at[1]))

    _stage_weights(be_ref, nx_ref, lr_ref, copies, (wlof_ref, whif_ref), (wlob_ref, whib_ref))

    def rows_out(rows):
        h = h_ref[rows, :]
        return _pack_bf16_pair(jnp.dot(h, wlob_ref[...], preferred_element_type=F32),
                               jnp.dot(h, whib_ref[...], preferred_element_type=F32))

    _half_block_compute(vr_ref[i], rows_out, o_ref)


def _experts(block_e, n_used, next_e, last_run, valid_rows, xs, wg, wu, wd, layer, tm, *, tf=512, tn=1024):
    r = xs.shape[0]
    d = 2 * xs.shape[1]
    ff = wg.shape[3]
    tf = _tile(ff, tf)
    nblk = r // tm
    hbm = pl.BlockSpec(memory_space=pl.ANY)
    hid = pl.pallas_call(
        functools.partial(_expert_up_kernel, layer=layer, tf=tf),
        grid_spec=pltpu.PrefetchScalarGridSpec(
            num_scalar_prefetch=5,
            grid=(ff // tf, nblk),
            in_specs=[
                pl.BlockSpec((tm, d // 2), lambda f, i, be, nu, nx, lr, vr: (jnp.minimum(i, nu[0] - 1), 0)),
                hbm, hbm,
            ],
            out_specs=pl.BlockSpec((tm, tf), lambda f, i, be, nu, nx, lr, vr: (i, f)),
            scratch_shapes=[pltpu.VMEM((d, tf), F32), pltpu.VMEM((d, tf), F32),
                            pltpu.VMEM((d, tf), BF16), pltpu.VMEM((d, tf), BF16),
                            pltpu.SemaphoreType.DMA((2,))],
        ),
        out_shape=jax.ShapeDtypeStruct((r, ff), BF16),
        compiler_params=_params(("arbitrary", "arbitrary")),
        name="expert_up",
    )(block_e, n_used, next_e, last_run, valid_rows, xs, wg, wu)
    half = d // 2
    tn = _tile(half, tn)
    hi0 = half // tn
    return pl.pallas_call(
        functools.partial(_expert_down_kernel, layer=layer, tn=tn, hi0=hi0),
        grid_spec=pltpu.PrefetchScalarGridSpec(
            num_scalar_prefetch=5,
            grid=(half // tn, nblk),
            in_specs=[pl.BlockSpec((tm, ff), lambda n, i, be, nu, nx, lr, vr: (i, 0)), hbm],
            out_specs=pl.BlockSpec((tm, tn), lambda n, i, be, nu, nx, lr, vr: (i, n)),
            scratch_shapes=[pltpu.VMEM((ff, tn), F32), pltpu.VMEM((ff, tn), F32),
                            pltpu.VMEM((ff, tn), BF16), pltpu.VMEM((ff, tn), BF16),
                            pltpu.SemaphoreType.DMA((2,))],
        ),
        out_shape=jax.ShapeDtypeStruct((r, half), jnp.uint32),
        compiler_params=_params(("arbitrary", "arbitrary")),
        name="expert_down",
    )(block_e, n_used, next_e, last_run, valid_rows, hid, wd)


def _combine_kernel(slot_ref, x_ref, y_hbm, gt_ref, g_ref, b_ref, o_ref, ob_ref, ybuf, sem, *, alpha, tm):
    i = pl.program_id(0)
    n = pl.num_programs(0)

    def row_copy(row, s, k, r):
        return pltpu.make_async_copy(y_hbm.at[pl.ds(row, 1), :], ybuf.at[s, k, pl.ds(r, 1), :], sem.at[s])

    def start_gather(step, s):
        base = step * (tm * TOP_K)

        def issue(r, c):
            for k in range(TOP_K):
                row_copy(slot_ref[base + r * TOP_K + k], s, k, r).start(priority=k % 2)
            return c

        lax.fori_loop(0, tm, issue, 0, unroll=4)

    def wait_gather(s):
        for k in range(TOP_K):
            pltpu.make_async_copy(y_hbm.at[pl.ds(0, tm), :], ybuf.at[s, k], sem.at[s]).wait()

    @pl.when(i == 0)
    def _():
        start_gather(0, 0)

    @pl.when(i + 1 < n)
    def _():
        start_gather(i + 1, (i + 1) % 2)

    s = i % 2
    wait_gather(s)
    gt = gt_ref[...]
    a_lo, a_hi = _unpack_bf16_pair(ybuf[s, 0])
    b_lo, b_hi = _unpack_bf16_pair(ybuf[s, 1])
    g0, g1 = gt[:, 0:1], gt[:, 1:2]
    ffn = jnp.concatenate([g0 * a_lo + g1 * b_lo, g0 * a_hi + g1 * b_hi], axis=1)
    y = _ln_rows(alpha * x_ref[...] + ffn, g_ref[...], b_ref[...])
    o_ref[...] = y
    ob_ref[...] = y.astype(BF16)


def _combine_ln(slot, x, y, gates_t, g, b, layer, alpha, *, tm=256):
    t, d = x.shape
    tm = _tile(t, tm)
    row = pl.BlockSpec((tm, d), lambda i, sl: (i, 0))
    vec = pl.BlockSpec((None, 1, d), lambda i, sl: (layer, 0, 0))
    grid_spec = pltpu.PrefetchScalarGridSpec(
        num_scalar_prefetch=1,
        grid=(t // tm,),
        in_specs=[row, pl.BlockSpec(memory_space=pl.ANY),
                  pl.BlockSpec((tm, SUBLANES), lambda i, sl: (i, 0)), vec, vec],
        out_specs=[row, row],
        scratch_shapes=[pltpu.VMEM((2, TOP_K, tm, d // 2), jnp.uint32), pltpu.SemaphoreType.DMA((2,))],
    )
    return pl.pallas_call(
        functools.partial(_combine_kernel, alpha=alpha, tm=tm),
        grid_spec=grid_spec,
        out_shape=[jax.ShapeDtypeStruct((t, d), F32), jax.ShapeDtypeStruct((t, d), BF16)],
        compiler_params=_params(("arbitrary",)),
        name="combine_ln",
    )(slot, x, y, gates_t, g, b)


def _rope_tables(seq):
    axis_dim = HEAD_DIM // 2
    rows = seq // GRID_W
    inv = ROPE_THETA ** (-jnp.arange(0, axis_dim, 2, dtype=F32) / axis_dim)
    ang_r = jnp.repeat(jnp.arange(rows, dtype=F32), GRID_W)[:, None] * inv[None, :]
    ang_c = jnp.tile(jnp.arange(GRID_W, dtype=F32), rows)[:, None] * inv[None, :]
    cr, sr, cc, sc = jnp.cos(ang_r), jnp.sin(ang_r), jnp.cos(ang_c), jnp.sin(ang_c)
    cos_t = jnp.concatenate([cr, cr, cc, cc], axis=1)
    sin_t = jnp.concatenate([-sr, sr, -sc, sc], axis=1)
    return cos_t, sin_t


def _dispatch(idx, n_exp, tm):
    t = idx.shape[1]
    n_asg = t * TOP_K
    e_flat = idx.T.reshape(-1)
    onehot = (e_flat[:, None] == jnp.arange(n_exp, dtype=jnp.int32)[None, :]).astype(jnp.int32)
    csum = jnp.cumsum(onehot, axis=0)
    rank = jnp.sum(csum * onehot, axis=1) - 1
    counts = csum[-1]
    pcounts = (counts + tm - 1) // tm * tm
    pends = jnp.cumsum(pcounts)
    pstarts = pends - pcounts
    slot = (pstarts[e_flat] + rank).astype(jnp.int32)
    nblk = n_asg // tm + n_exp
    block_start = jnp.arange(nblk, dtype=jnp.int32) * tm
    block_e = jnp.minimum(jnp.sum(pends[None, :] <= block_start[:, None], axis=-1), n_exp - 1).astype(jnp.int32)
    n_used = (pends[-1] // tm).astype(jnp.int32).reshape(1)
    change_at = jnp.where(block_e[1:] != block_e[:-1], jnp.arange(1, nblk, dtype=jnp.int32), nblk)
    next_start = lax.cummin(jnp.concatenate([change_at, jnp.full((1,), nblk, jnp.int32)]), reverse=True)
    last_run = (next_start >= nblk).astype(jnp.int32)
    next_e = block_e[jnp.minimum(next_start, nblk - 1)]
    valid_rows = jnp.clip(pstarts[block_e] + counts[block_e] - block_start, 0, tm).astype(jnp.int32)
    return (slot, counts.astype(jnp.int32), pstarts.astype(jnp.int32), block_e, n_used, next_e, last_run,
            valid_rows, nblk * tm)


def kernel(x, w_in, q_norm_g, k_norm_g, conv_w, w_attn_proj, w_conv_proj, w_out, ln1_g, ln1_b,
           w_router, router_bias, w_gate, w_up, w_down, ln2_g, ln2_b):
    batch, seq, d = x.shape
    depth = w_in.shape[0]
    t = batch * seq
    attn_w = d // 2
    n_q = attn_w // HEAD_DIM
    n_kv = n_q // GQA_RATIO
    kv_w = n_kv * HEAD_DIM
    conv_wd = d // 2
    n_exp = w_router.shape[1]
    alpha = (2 * depth) ** 0.25
    q_end = attn_w
    k_end = q_end + kv_w
    v_end = k_end + kv_w
    cb_end = v_end + conv_wd
    cc_end = cb_end + conv_wd
    ch_end = cc_end + conv_wd
    ga_end = ch_end + d
    expert_tm = min(512, t)

    cos_t, sin_t = _rope_tables(seq)
    scale = HEAD_DIM ** -0.5 * math.log2(math.e)
    w_router_t = w_router.T
    bias_col = router_bias.reshape(n_exp, 1).astype(F32)
    ln1_g3, ln1_b3 = ln1_g.reshape(depth, 1, d), ln1_b.reshape(depth, 1, d)
    ln2_g3, ln2_b3 = ln2_g.reshape(depth, 1, d), ln2_b.reshape(depth, 1, d)

    xf = x.reshape(t, d)
    xb = xf.astype(BF16)
    for l in range(depth):
        p = _matmul(xb, w_in, l, BF16, name="in_proj")
        gains = jnp.concatenate([jnp.tile(q_norm_g[l] * scale, n_q), jnp.tile(k_norm_g[l], n_kv)]).reshape(1, k_end)
        qk = _qk_prep(p, gains.astype(F32), cos_t, sin_t, seq, k_end)
        attn = _attention(qk, p, batch, seq, n_kv, q_end // HEAD_DIM, k_end // HEAD_DIM)
        conv = _short_conv(p, conv_w, l, seq, v_end, cb_end, cc_end, conv_wd)
        merged = _merge(attn, conv, w_attn_proj, w_conv_proj, p, l, ch_end, ga_end)
        h1 = _matmul(merged, w_out, l, F32, res=xf, alpha=alpha, tm=1024, tn=512, name="out_proj")
        x1, x1p, idx, gate = _ln_route(h1, ln1_g3, ln1_b3, l, w_router_t, bias_col)
        slot, counts, pstarts, block_e, n_used, next_e, last_run, valid_rows, n_rows = _dispatch(
            idx[:TOP_K], n_exp, expert_tm)
        xs = _dispatch_rows(slot, counts, pstarts, n_used, x1p, n_rows, expert_tm)
        y = _experts(block_e, n_used, next_e, last_run, valid_rows, xs, w_gate, w_up, w_down, l, expert_tm)
        xf, xb = _combine_ln(slot, x1, y, gate.T, ln2_g3, ln2_b3, l, alpha)
    return xf.reshape(batch, seq, d)
```

```python
import functools
import math

import jax
import jax.numpy as jnp
from jax import lax
from jax.experimental import pallas as pl
from jax.experimental.pallas import tpu as pltpu

HEAD_DIM = 128
GQA_RATIO = 4
CONV_K = 3
GRID_W = 64
ROPE_THETA = 10000.0
N_GROUPS = 4
TOP_K = 2
RMS_EPS = 1e-6
LN_EPS = 1e-5
SUBLANES = 8
ONES_ROWS = 2 * SUBLANES
ISSUE_UNROLL = 8
VMEM_LIMIT = 56 * 1024 * 1024

F32 = jnp.float32
BF16 = jnp.bfloat16


def _params(semantics):
    return pltpu.CompilerParams(dimension_semantics=semantics, vmem_limit_bytes=VMEM_LIMIT)


def _tile(dim, pref):
    t = min(dim, pref)
    while dim % t:
        t //= 2
    return t


def _pack_bf16_pair(lo, hi):
    lo_b = lax.bitcast_convert_type(lo.astype(BF16).astype(F32), jnp.uint32) >> 16
    hi_b = lax.bitcast_convert_type(hi.astype(BF16).astype(F32), jnp.uint32) & jnp.uint32(0xFFFF0000)
    return hi_b | lo_b


def _unpack_bf16_pair(w):
    lo = lax.bitcast_convert_type(w << 16, F32)
    hi = lax.bitcast_convert_type(w & jnp.uint32(0xFFFF0000), F32)
    return lo, hi


def _stage_weight_tile(w_hbm, stage_ref, cache_ref, sem, layer, tn):
    j = pl.program_id(0)
    i = pl.program_id(1)

    def tile_copy(jj):
        cols = pl.ds(pl.multiple_of(jj * tn, tn), tn)
        return pltpu.make_async_copy(w_hbm.at[layer, :, cols], stage_ref, sem.at[0])

    @pl.when(jnp.logical_and(j == 0, i == 0))
    def _():
        tile_copy(0).start()

    @pl.when(i == 0)
    def _():
        tile_copy(j).wait()
        cache_ref[...] = stage_ref[...].astype(BF16)

        @pl.when(j + 1 < pl.num_programs(0))
        def _():
            tile_copy(j + 1).start()


def _mm_kernel(x_ref, w_hbm, o_ref, ws_ref, wb_ref, sem, *, layer, tn):
    _stage_weight_tile(w_hbm, ws_ref, wb_ref, sem, layer, tn)
    o_ref[...] = jnp.dot(x_ref[...], wb_ref[...], preferred_element_type=F32).astype(o_ref.dtype)


def _mm_res_kernel(x_ref, w_hbm, r_ref, o_ref, ws_ref, wb_ref, sem, *, layer, tn, alpha):
    _stage_weight_tile(w_hbm, ws_ref, wb_ref, sem, layer, tn)
    acc = jnp.dot(x_ref[...], wb_ref[...], preferred_element_type=F32)
    o_ref[...] = (alpha * r_ref[...] + acc).astype(o_ref.dtype)


def _matmul(x, w, layer, out_dtype, *, res=None, alpha=1.0, tm=1024, tn=1024, name="matmul"):
    m, k = x.shape
    n = w.shape[2]
    tm, tn = _tile(m, tm), _tile(n, tn)
    in_specs = [pl.BlockSpec((tm, k), lambda j, i: (i, 0)), pl.BlockSpec(memory_space=pl.ANY)]
    args = [x, w]
    body = functools.partial(_mm_kernel, layer=layer, tn=tn)
    if res is not None:
        in_specs.append(pl.BlockSpec((tm, tn), lambda j, i: (i, j)))
        args.append(res)
        body = functools.partial(_mm_res_kernel, layer=layer, tn=tn, alpha=alpha)
    return pl.pallas_call(
        body,
        grid=(n // tn, m // tm),
        in_specs=in_specs,
        out_specs=pl.BlockSpec((tm, tn), lambda j, i: (i, j)),
        out_shape=jax.ShapeDtypeStruct((m, n), out_dtype),
        scratch_shapes=[pltpu.VMEM((k, tn), F32), pltpu.VMEM((k, tn), BF16), pltpu.SemaphoreType.DMA((1,))],
        compiler_params=_params(("arbitrary", "arbitrary")),
        name=name,
    )(*args)


def _qk_prep_kernel(p_ref, g_ref, c_ref, s_ref, ones_ref, perm_ref, o_ref, *, heads):
    cos = c_ref[...]
    sin = s_ref[...]
    ones = ones_ref[...]
    perm = perm_ref[...]
    for h in range(heads):
        sl = slice(h * HEAD_DIM, (h + 1) * HEAD_DIM)
        xh = p_ref[:, sl].astype(F32)
        ssq = jnp.dot((xh * xh).astype(BF16), ones, preferred_element_type=F32)
        y = xh * lax.rsqrt(ssq * (1.0 / HEAD_DIM) + RMS_EPS) * g_ref[:, sl]
        swapped = jnp.dot(y.astype(BF16), perm, preferred_element_type=F32)
        o_ref[:, sl] = (y * cos + swapped * sin).astype(o_ref.dtype)


def _qk_prep(p, gains, cos_t, sin_t, seq, qk_width, *, tm=512, heads=32):
    t = p.shape[0]
    tm = _tile(seq, tm)
    n_heads = qk_width // HEAD_DIM
    heads = _tile(n_heads, heads)
    wblk = heads * HEAD_DIM
    sblocks = seq // tm
    ones = jnp.ones((HEAD_DIM, HEAD_DIM), BF16)
    axis_dim, half_dim = HEAD_DIM // 2, HEAD_DIM // 4
    lane = jnp.arange(HEAD_DIM)
    partner = jnp.where((lane % axis_dim) < half_dim, lane + half_dim, lane - half_dim)
    perm = (lane[:, None] == partner[None, :]).astype(BF16)
    return pl.pallas_call(
        functools.partial(_qk_prep_kernel, heads=heads),
        grid=(t // tm, n_heads // heads),
        in_specs=[
            pl.BlockSpec((tm, wblk), lambda i, j: (i, j)),
            pl.BlockSpec((1, wblk), lambda i, j: (0, j)),
            pl.BlockSpec((tm, HEAD_DIM), lambda i, j: (i % sblocks, 0)),
            pl.BlockSpec((tm, HEAD_DIM), lambda i, j: (i % sblocks, 0)),
            pl.BlockSpec((HEAD_DIM, HEAD_DIM), lambda i, j: (0, 0)),
            pl.BlockSpec((HEAD_DIM, HEAD_DIM), lambda i, j: (0, 0)),
        ],
        out_specs=pl.BlockSpec((tm, wblk), lambda i, j: (i, j)),
        out_shape=jax.ShapeDtypeStruct((t, qk_width), BF16),
        compiler_params=_params(("parallel", "parallel")),
        name="qk_prep",
    )(p, gains, cos_t, sin_t, ones, perm)


def _flash_kernel(q_ref, k_ref, v_ref, o_ref, qt_ref, vt_ref, *stat_refs, tq, tk, cb, seq, unroll, ahead):
    nkv = seq // tk
    ncb = GQA_RATIO * tq // cb
    acc_refs, m_refs = stat_refs[:ncb], stat_refs[ncb:]

    @pl.when(pl.program_id(2) == 0)
    def _():
        for c in range(nkv):
            vt_ref[c, :HEAD_DIM, :] = v_ref[c * tk:(c + 1) * tk, :].T
            vt_ref[c, HEAD_DIM:, :] = jnp.ones((ONES_ROWS, tk), BF16)

    for g in range(GQA_RATIO):
        qt_ref[:, g * tq:(g + 1) * tq] = q_ref[:, g * HEAD_DIM:(g + 1) * HEAD_DIM].T
    for c in range(ncb):
        m_refs[c][...] = jnp.full(m_refs[c].shape, -jnp.inf, F32)
        acc_refs[c][...] = jnp.zeros(acc_refs[c].shape, F32)

    def body(j, carry):
        tiles = [(u, c) for u in range(unroll) for c in range(ncb)]
        ks, vts = [], []
        for u in range(unroll):
            off = pl.multiple_of((j * unroll + u) * tk, tk)
            ks.append(k_ref[pl.ds(off, tk), :])
            vts.append(vt_ref[j * unroll + u])

        def scores(t):
            u, c = tiles[t]
            return jnp.dot(ks[u], qt_ref[:, c * cb:(c + 1) * cb], preferred_element_type=F32)

        pending = [scores(t) for t in range(min(ahead, len(tiles)))]
        for t, (u, c) in enumerate(tiles):
            st = pending.pop(0)
            if t + ahead < len(tiles):
                pending.append(scores(t + ahead))
            m_old = m_refs[c][...]
            m_new = jnp.maximum(m_old, jnp.max(st, axis=0, keepdims=True))
            alpha = jnp.exp2(m_old - m_new)
            pt = jnp.exp2(st - m_new).astype(BF16)
            m_refs[c][...] = m_new
            acc_refs[c][...] = alpha * acc_refs[c][...] + jnp.dot(vts[u], pt, preferred_element_type=F32)
        return carry

    lax.fori_loop(0, nkv // unroll, body, 0)
    per_g = tq // cb
    for c in range(ncb):
        g, r = divmod(c, per_g)
        acc = acc_refs[c][...]
        out_t = acc[:HEAD_DIM, :] / acc[HEAD_DIM:HEAD_DIM + 1, :]
        o_ref[r * cb:(r + 1) * cb, g * HEAD_DIM:(g + 1) * HEAD_DIM] = out_t.T.astype(o_ref.dtype)


def _attention(qk, p, batch, seq, n_kv, k_col0, v_col0, *, tq=512, tk=256, cb=256, unroll=8, ahead=5):
    tq, tk = _tile(seq, tq), _tile(seq, tk)
    rows = GQA_RATIO * tq
    cb = _tile(rows, cb)
    unroll = _tile(seq // tk, unroll)
    qk3 = qk.reshape(batch, seq, qk.shape[1])
    p3 = p.reshape(batch, seq, p.shape[1])
    gw = GQA_RATIO * HEAD_DIM
    out = pl.pallas_call(
        functools.partial(_flash_kernel, tq=tq, tk=tk, cb=cb, seq=seq, unroll=unroll, ahead=ahead),
        grid=(batch, n_kv, seq // tq),
        in_specs=[
            pl.BlockSpec((None, tq, gw), lambda b, h, i: (b, i, h)),
            pl.BlockSpec((None, seq, HEAD_DIM), lambda b, h, i: (b, 0, k_col0 + h)),
            pl.BlockSpec((None, seq, HEAD_DIM), lambda b, h, i: (b, 0, v_col0 + h)),
        ],
        out_specs=pl.BlockSpec((None, tq, gw), lambda b, h, i: (b, i, h)),
        out_shape=jax.ShapeDtypeStruct((batch, seq, n_kv * gw), BF16),
        scratch_shapes=[
            pltpu.VMEM((HEAD_DIM, rows), BF16),
            pltpu.VMEM((seq // tk, HEAD_DIM + ONES_ROWS, tk), BF16),
        ] + [pltpu.VMEM((HEAD_DIM + ONES_ROWS, cb), F32)] * (rows // cb)
        + [pltpu.VMEM((1, cb), F32)] * (rows // cb),
        compiler_params=_params(("parallel", "parallel", "arbitrary")),
        name="flash_gqa",
    )(qk3, qk3, p3)
    return out.reshape(batch * seq, n_kv * gw)


def _conv_kernel(cb_ref, cc_ref, ch_ref, cbp_ref, chp_ref, cbn_ref, chn_ref, w_ref, o_ref, *, tm, sblocks):
    i = pl.program_id(0)
    first = (i % sblocks) == 0
    last = (i % sblocks) == sblocks - 1
    bx = cb_ref[...].astype(F32) * ch_ref[...].astype(F32)
    prev_row = cbp_ref[SUBLANES - 1:SUBLANES, :].astype(F32) * chp_ref[SUBLANES - 1:SUBLANES, :].astype(F32)
    next_row = cbn_ref[0:1, :].astype(F32) * chn_ref[0:1, :].astype(F32)
    prev_row = jnp.where(first, 0.0, prev_row)
    next_row = jnp.where(last, 0.0, next_row)
    row = lax.broadcasted_iota(jnp.int32, bx.shape, 0)
    prev = jnp.where(row == 0, prev_row, pltpu.roll(bx, 1, axis=0))
    nxt = jnp.where(row == tm - 1, next_row, pltpu.roll(bx, tm - 1, axis=0))
    w = w_ref[...]
    y = w[0:1, :] * prev + w[1:2, :] * bx + w[2:3, :] * nxt
    o_ref[...] = (cc_ref[...].astype(F32) * y).astype(o_ref.dtype)


def _short_conv(p, conv_w, layer, seq, cb0, cc0, ch0, width, *, tm=512, tc=1024):
    t = p.shape[0]
    tm, tc = _tile(seq, tm), math.gcd(cb0, cc0, ch0, width, tc)
    sblocks = seq // tm
    rpb = tm // SUBLANES
    nrb = t // SUBLANES

    def main(c0):
        return pl.BlockSpec((tm, tc), lambda i, j: (i, c0 // tc + j))

    def halo_prev(c0):
        return pl.BlockSpec((SUBLANES, tc), lambda i, j: (jnp.maximum(i * rpb - 1, 0), c0 // tc + j))

    def halo_next(c0):
        return pl.BlockSpec((SUBLANES, tc), lambda i, j: (jnp.minimum((i + 1) * rpb, nrb - 1), c0 // tc + j))

    return pl.pallas_call(
        functools.partial(_conv_kernel, tm=tm, sblocks=sblocks),
        grid=(t // tm, width // tc),
        in_specs=[main(cb0), main(cc0), main(ch0), halo_prev(cb0), halo_prev(ch0),
                  halo_next(cb0), halo_next(ch0),
                  pl.BlockSpec((None, CONV_K, tc), lambda i, j: (layer, 0, j))],
        out_specs=pl.BlockSpec((tm, tc), lambda i, j: (i, j)),
        out_shape=jax.ShapeDtypeStruct((t, width), BF16),
        compiler_params=_params(("parallel", "parallel")),
        name="short_conv",
    )(p, p, p, p, p, p, p, conv_w)


def _merge_kernel(a_ref, c_ref, wa_ref, wc_ref, ga_ref, gc_ref, o_ref, wab_ref, wcb_ref):
    @pl.when(pl.program_id(1) == 0)
    def _():
        wab_ref[...] = wa_ref[...].astype(BF16)
        wcb_ref[...] = wc_ref[...].astype(BF16)

    ya = jnp.dot(a_ref[...], wab_ref[...], preferred_element_type=F32)
    yc = jnp.dot(c_ref[...], wcb_ref[...], preferred_element_type=F32)
    o_ref[...] = (jax.nn.sigmoid(ga_ref[...].astype(F32)) * ya
                  + jax.nn.sigmoid(gc_ref[...].astype(F32)) * yc).astype(o_ref.dtype)


def _merge(attn, conv, wa, wc, p, layer, ga0, gc0, *, tm=1024, tn=512):
    t, ka = attn.shape
    kc = conv.shape[1]
    d = wa.shape[2]
    tm, tn = _tile(t, tm), math.gcd(ga0, gc0, d, tn)
    return pl.pallas_call(
        _merge_kernel,
        grid=(d // tn, t // tm),
        in_specs=[
            pl.BlockSpec((tm, ka), lambda j, i: (i, 0)),
            pl.BlockSpec((tm, kc), lambda j, i: (i, 0)),
            pl.BlockSpec((None, ka, tn), lambda j, i: (layer, 0, j)),
            pl.BlockSpec((None, kc, tn), lambda j, i: (layer, 0, j)),
            pl.BlockSpec((tm, tn), lambda j, i: (i, ga0 // tn + j)),
            pl.BlockSpec((tm, tn), lambda j, i: (i, gc0 // tn + j)),
        ],
        out_specs=pl.BlockSpec((tm, tn), lambda j, i: (i, j)),
        out_shape=jax.ShapeDtypeStruct((t, d), BF16),
        scratch_shapes=[pltpu.VMEM((ka, tn), BF16), pltpu.VMEM((kc, tn), BF16)],
        compiler_params=_params(("parallel", "arbitrary")),
        name="gated_merge",
    )(attn, conv, wa, wc, p, p)


def _ln_rows(h, g, b):
    mu = jnp.mean(h, axis=-1, keepdims=True)
    hc = h - mu
    var = jnp.mean(hc * hc, axis=-1, keepdims=True)
    return hc * lax.rsqrt(var + LN_EPS) * g + b


def _route(logits, bias, n_exp):
    epg = n_exp // N_GROUPS
    mx = jnp.max(logits, axis=0, keepdims=True)
    ex = jnp.exp(logits - mx)
    scores = ex / jnp.sum(ex, axis=0, keepdims=True)
    sel = scores + bias
    rows_sel = [sel[e:e + 1, :] for e in range(n_exp)]
    rows_sc = [scores[e:e + 1, :] for e in range(n_exp)]
    best = None
    for g in range(N_GROUPS):
        mem = rows_sel[g * epg:(g + 1) * epg]
        gs = None
        for a in range(epg):
            for c in range(a + 1, epg):
                pair = mem[a] + mem[c]
                gs = pair if gs is None else jnp.maximum(gs, pair)
        if best is None:
            best, grp = gs, jnp.zeros(gs.shape, jnp.int32)
        else:
            upd = gs > best
            best = jnp.where(upd, gs, best)
            grp = jnp.where(upd, g, grp)
    cand_sel, cand_sc = [], []
    for j in range(epg):
        cs, cc = rows_sel[j], rows_sc[j]
        for g in range(1, N_GROUPS):
            cs = jnp.where(grp == g, rows_sel[g * epg + j], cs)
            cc = jnp.where(grp == g, rows_sc[g * epg + j], cc)
        cand_sel.append(cs)
        cand_sc.append(cc)

    def first_argmax(vals, skip):
        bv = bi = bs = None
        for j in range(epg):
            v = vals[j] if skip is None else jnp.where(skip == j, -jnp.inf, vals[j])
            if bv is None:
                bv, bi, bs = v, jnp.zeros(v.shape, jnp.int32), cand_sc[0]
            else:
                upd = v > bv
                bv = jnp.where(upd, v, bv)
                bi = jnp.where(upd, j, bi)
                bs = jnp.where(upd, cand_sc[j], bs)
        return bi, bs

    i1, s1 = first_argmax(cand_sel, None)
    i2, s2 = first_argmax(cand_sel, i1)
    tot = s1 + s2
    zi = jnp.zeros((SUBLANES - TOP_K,) + i1.shape[1:], jnp.int32)
    zf = jnp.zeros((SUBLANES - TOP_K,) + i1.shape[1:], F32)
    idx = jnp.concatenate([grp * epg + i1, grp * epg + i2, zi], axis=0)
    gate = jnp.concatenate([s1 / tot, s2 / tot, zf], axis=0)
    return idx, gate


def _ln_route_kernel(h_ref, g_ref, b_ref, wh_ref, wl_ref, rb_ref, o_ref, op_ref, idx_ref, gate_ref, *, n_exp):
    y = _ln_rows(h_ref[...], g_ref[...], b_ref[...])
    half = y.shape[1] // 2
    o_ref[...] = y
    op_ref[...] = _pack_bf16_pair(y[:, :half], y[:, half:])
    yh = y.astype(BF16)
    yl = (y - yh.astype(F32)).astype(BF16)
    nt = (((1,), (1,)), ((), ()))
    logits = (lax.dot_general(wh_ref[...], yh, nt, preferred_element_type=F32)
              + lax.dot_general(wh_ref[...], yl, nt, preferred_element_type=F32)
              + lax.dot_general(wl_ref[...], yh, nt, preferred_element_type=F32))
    idx_ref[...], gate_ref[...] = _route(logits, rb_ref[...], n_exp)


def _ln_route(h, g, b, layer, w_router_t, bias_col, *, tm=256):
    t, d = h.shape
    n_exp = w_router_t.shape[0]
    tm = _tile(t, tm)
    wh = w_router_t.astype(BF16)
    wl = (w_router_t - wh.astype(F32)).astype(BF16)
    row = pl.BlockSpec((tm, d), lambda i: (i, 0))
    vec = pl.BlockSpec((None, 1, d), lambda i: (layer, 0, 0))
    wsp = pl.BlockSpec((n_exp, d), lambda i: (0, 0))
    rsp = pl.BlockSpec((SUBLANES, tm), lambda i: (0, i))
    return pl.pallas_call(
        functools.partial(_ln_route_kernel, n_exp=n_exp),
        grid=(t // tm,),
        in_specs=[row, vec, vec, wsp, wsp, pl.BlockSpec((n_exp, 1), lambda i: (0, 0))],
        out_specs=[row, pl.BlockSpec((tm, d // 2), lambda i: (i, 0)), rsp, rsp],
        out_shape=[jax.ShapeDtypeStruct((t, d), F32), jax.ShapeDtypeStruct((t, d // 2), jnp.uint32),
                   jax.ShapeDtypeStruct((SUBLANES, t), jnp.int32), jax.ShapeDtypeStruct((SUBLANES, t), F32)],
        compiler_params=_params(("parallel",)),
        name="ln_route",
    )(h, g, b, wh, wl, bias_col)


def _dispatch_kernel(slot_ref, cnt_ref, pst_ref, nu_ref, x_ref, o_hbm, z_ref, sem, *, tm, n_exp, epp, nblk):
    i = pl.program_id(0)
    n = pl.num_programs(0)

    @pl.when(i == 0)
    def _():
        z_ref[...] = jnp.zeros_like(z_ref)

    def issue(g, c):
        for u in range(ISSUE_UNROLL):
            r = g * ISSUE_UNROLL + u
            for k in range(TOP_K):
                pltpu.make_async_copy(x_ref.at[pl.ds(r, 1), :],
                                      o_hbm.at[pl.ds(slot_ref[(i * tm + r) * TOP_K + k], 1), :],
                                      sem.at[0]).start(priority=k % 2)
        return c

    lax.fori_loop(0, tm // ISSUE_UNROLL, issue, 0)

    def wait_rows(rows):
        pltpu.make_async_copy(z_ref.at[pl.ds(0, rows), :], o_hbm.at[pl.ds(0, rows), :], sem.at[0]).wait()

    for j in range(epp):
        e = i * epp + j

        @pl.when(e < n_exp)
        def _():
            cnt = cnt_ref[e]
            first = pst_ref[e] + cnt
            npad = (tm - cnt % tm) % tm

            def zero(r, c):
                pltpu.make_async_copy(z_ref.at[pl.ds(0, 1), :], o_hbm.at[pl.ds(first + r, 1), :], sem.at[0]).start()
                return c

            def wait_one(r, c):
                wait_rows(1)
                return c

            lax.fori_loop(0, npad, zero, 0)
            lax.fori_loop(0, npad, wait_one, 0)

    for k in range(TOP_K):
        wait_rows(tm)

    @pl.when(i == n - 1)
    def _():
        def zero_block(b, c):
            cp = pltpu.make_async_copy(z_ref, o_hbm.at[pl.ds(pl.multiple_of(b * tm, tm), tm), :], sem.at[0])
            cp.start()
            cp.wait()
            return c

        lax.fori_loop(nu_ref[0], nblk, zero_block, 0)


def _dispatch_rows(slot, counts, pstarts, n_used, xp, n_rows, tm):
    t, w = xp.shape
    n_exp = counts.shape[0]
    assert t % tm == 0 and n_rows % tm == 0
    n_steps = t // tm
    epp = -(-n_exp // n_steps)
    grid_spec = pltpu.PrefetchScalarGridSpec(
        num_scalar_prefetch=4,
        grid=(n_steps,),
        in_specs=[pl.BlockSpec((tm, w), lambda i, sl, cn, ps, nu: (i, 0))],
        out_specs=pl.BlockSpec(memory_space=pl.ANY),
        scratch_shapes=[pltpu.VMEM((tm, w), xp.dtype), pltpu.SemaphoreType.DMA((1,))],
    )
    return pl.pallas_call(
        functools.partial(_dispatch_kernel, tm=tm, n_exp=n_exp, epp=epp, nblk=n_rows // tm),
        grid_spec=grid_spec,
        out_shape=jax.ShapeDtypeStruct((n_rows, w), xp.dtype),
        compiler_params=_params(("arbitrary",)),
        name="dispatch_rows",
    )(slot, counts, pstarts, n_used, xp)


def _new_expert(be_ref, i):
    return jnp.logical_or(i == 0, be_ref[i] != be_ref[jnp.maximum(i - 1, 0)])


def _stage_weights(be_ref, nx_ref, lr_ref, copies, stage_refs, cache_refs):
    t = pl.program_id(0)
    i = pl.program_id(1)
    n_t = pl.num_programs(0)

    def start(e, tile):
        for cp in copies(e, tile):
            cp.start()

    @pl.when(jnp.logical_and(t == 0, i == 0))
    def _():
        start(be_ref[0], 0)

    @pl.when(_new_expert(be_ref, i))
    def _():
        for cp in copies(be_ref[i], t):
            cp.wait()
        for st, ca in zip(stage_refs, cache_refs):
            ca[...] = st[...].astype(BF16)
        last_run = lr_ref[i] == 1

        @pl.when(jnp.logical_not(last_run))
        def _():
            start(nx_ref[i], t)

        @pl.when(jnp.logical_and(last_run, t + 1 < n_t))
        def _():
            start(be_ref[0], t + 1)


def _half_block_compute(valid, compute, o_ref):
    tm = o_ref.shape[0]
    hb = tm // 2

    @pl.when(valid > hb)
    def _():
        o_ref[...] = compute(slice(None))

    @pl.when(jnp.logical_and(valid > 0, valid <= hb))
    def _():
        o_ref[:hb, :] = compute(slice(0, hb))
        o_ref[hb:, :] = jnp.zeros((tm - hb, o_ref.shape[1]), o_ref.dtype)

    @pl.when(valid == 0)
    def _():
        o_ref[...] = jnp.zeros_like(o_ref)


def _expert_up_kernel(be_ref, nu_ref, nx_ref, lr_ref, vr_ref, x_ref, wg_hbm, wu_hbm, o_ref,
                      wgf_ref, wuf_ref, wgb_ref, wub_ref, sem, *, layer, tf):
    i = pl.program_id(1)

    def copies(e, tile):
        cols = pl.ds(pl.multiple_of(tile * tf, tf), tf)
        return (pltpu.make_async_copy(wg_hbm.at[layer, e, :, cols], wgf_ref, sem.at[0]),
                pltpu.make_async_copy(wu_hbm.at[layer, e, :, cols], wuf_ref, sem.at[1]))

    _stage_weights(be_ref, nx_ref, lr_ref, copies, (wgf_ref, wuf_ref), (wgb_ref, wub_ref))

    def hidden(rows):
        lo, hi = _unpack_bf16_pair(x_ref[rows, :])
        lo, hi = lo.astype(BF16), hi.astype(BF16)
        half = lo.shape[1]
        hg = (jnp.dot(lo, wgb_ref[:half, :], preferred_element_type=F32)
              + jnp.dot(hi, wgb_ref[half:, :], preferred_element_type=F32))
        hu = (jnp.dot(lo, wub_ref[:half, :], preferred_element_type=F32)
              + jnp.dot(hi, wub_ref[half:, :], preferred_element_type=F32))
        return (hg * jax.nn.sigmoid(hg) * hu).astype(o_ref.dtype)

    _half_block_compute(vr_ref[i], hidden, o_ref)


def _expert_down_kernel(be_ref, nu_ref, nx_ref, lr_ref, vr_ref, h_ref, wd_hbm, o_ref,
                        wlof_ref, whif_ref, wlob_ref, whib_ref, sem, *, layer, tn, hi0):
    i = pl.program_id(1)

    def copies(e, tile):
        lo = pl.ds(pl.multiple_of(tile * tn, tn), tn)
        hi = pl.ds(pl.multiple_of((hi0 + tile) * tn, tn), tn)
        return (pltpu.make_async_copy(wd_hbm.at[layer, e, :, lo], wlof_ref, sem.at[0]),
                pltpu.make_async_copy(wd_hbm.at[layer, e, :, hi], whif_ref, sem.at[1]))

    _stage_weights(be_ref, nx_ref, lr_ref, copies, (wlof_ref, whif_ref), (wlob_ref, whib_ref))

    def rows_out(rows):
        h = h_ref[rows, :]
        return _pack_bf16_pair(jnp.dot(h, wlob_ref[...], preferred_element_type=F32),
                               jnp.dot(h, whib_ref[...], preferred_element_type=F32))

    _half_block_compute(vr_ref[i], rows_out, o_ref)


def _experts(block_e, n_used, next_e, last_run, valid_rows, xs, wg, wu, wd, layer, tm, *, tf=512, tn=1024):
    r = xs.shape[0]
    d = 2 * xs.shape[1]
    ff = wg.shape[3]
    tf = _tile(ff, tf)
    nblk = r // tm
    hbm = pl.BlockSpec(memory_space=pl.ANY)
    hid = pl.pallas_call(
        functools.partial(_expert_up_kernel, layer=layer, tf=tf),
        grid_spec=pltpu.PrefetchScalarGridSpec(
            num_scalar_prefetch=5,
            grid=(ff // tf, nblk),
            in_specs=[
                pl.BlockSpec((tm, d // 2), lambda f, i, be, nu, nx, lr, vr: (jnp.minimum(i, nu[0] - 1), 0)),
                hbm, hbm,
            ],
            out_specs=pl.BlockSpec((tm, tf), lambda f, i, be, nu, nx, lr, vr: (i, f)),
            scratch_shapes=[pltpu.VMEM((d, tf), F32), pltpu.VMEM((d, tf), F32),
                            pltpu.VMEM((d, tf), BF16), pltpu.VMEM((d, tf), BF16),
                            pltpu.SemaphoreType.DMA((2,))],
        ),
        out_shape=jax.ShapeDtypeStruct((r, ff), BF16),
        compiler_params=_params(("arbitrary", "arbitrary")),
        name="expert_up",
    )(block_e, n_used, next_e, last_run, valid_rows, xs, wg, wu)
    half = d // 2
    tn = _tile(half, tn)
    hi0 = half // tn
    return pl.pallas_call(
        functools.partial(_expert_down_kernel, layer=layer, tn=tn, hi0=hi0),
        grid_spec=pltpu.PrefetchScalarGridSpec(
            num_scalar_prefetch=5,
            grid=(half // tn, nblk),
            in_specs=[pl.BlockSpec((tm, ff), lambda n, i, be, nu, nx, lr, vr: (i, 0)), hbm],
            out_specs=pl.BlockSpec((tm, tn), lambda n, i, be, nu, nx, lr, vr: (i, n)),
            scratch_shapes=[pltpu.VMEM((ff, tn), F32), pltpu.VMEM((ff, tn), F32),
                            pltpu.VMEM((ff, tn), BF16), pltpu.VMEM((ff, tn), BF16),
                            pltpu.SemaphoreType.DMA((2,))],
        ),
        out_shape=jax.ShapeDtypeStruct((r, half), jnp.uint32),
        compiler_params=_params(("arbitrary", "arbitrary")),
        name="expert_down",
    )(block_e, n_used, next_e, last_run, valid_rows, hid, wd)


def _combine_kernel(slot_ref, x_ref, y_hbm, gt_ref, g_ref, b_ref, o_ref, ob_ref, ybuf, sem, *, alpha, tm):
    i = pl.program_id(0)
    n = pl.num_programs(0)

    def row_copy(row, s, k, r):
        return pltpu.make_async_copy(y_hbm.at[pl.ds(row, 1), :], ybuf.at[s, k, pl.ds(r, 1), :], sem.at[s])

    def start_gather(step, s):
        base = step * (tm * TOP_K)

        def issue(r, c):
            for k in range(TOP_K):
                row_copy(slot_ref[base + r * TOP_K + k], s, k, r).start(priority=k % 2)
            return c

        lax.fori_loop(0, tm, issue, 0, unroll=4)

    def wait_gather(s):
        for k in range(TOP_K):
            pltpu.make_async_copy(y_hbm.at[pl.ds(0, tm), :], ybuf.at[s, k], sem.at[s]).wait()

    @pl.when(i == 0)
    def _():
        start_gather(0, 0)

    @pl.when(i + 1 < n)
    def _():
        start_gather(i + 1, (i + 1) % 2)

    s = i % 2
    wait_gather(s)
    gt = gt_ref[...]
    a_lo, a_hi = _unpack_bf16_pair(ybuf[s, 0])
    b_lo, b_hi = _unpack_bf16_pair(ybuf[s, 1])
    g0, g1 = gt[:, 0:1], gt[:, 1:2]
    ffn = jnp.concatenate([g0 * a_lo + g1 * b_lo, g0 * a_hi + g1 * b_hi], axis=1)
    y = _ln_rows(alpha * x_ref[...] + ffn, g_ref[...], b_ref[...])
    o_ref[...] = y
    ob_ref[...] = y.astype(BF16)


def _combine_ln(slot, x, y, gates_t, g, b, layer, alpha, *, tm=256):
    t, d = x.shape
    tm = _tile(t, tm)
    row = pl.BlockSpec((tm, d), lambda i, sl: (i, 0))
    vec = pl.BlockSpec((None, 1, d), lambda i, sl: (layer, 0, 0))
    grid_spec = pltpu.PrefetchScalarGridSpec(
        num_scalar_prefetch=1,
        grid=(t // tm,),
        in_specs=[row, pl.BlockSpec(memory_space=pl.ANY),
                  pl.BlockSpec((tm, SUBLANES), lambda i, sl: (i, 0)), vec, vec],
        out_specs=[row, row],
        scratch_shapes=[pltpu.VMEM((2, TOP_K, tm, d // 2), jnp.uint32), pltpu.SemaphoreType.DMA((2,))],
    )
    return pl.pallas_call(
        functools.partial(_combine_kernel, alpha=alpha, tm=tm),
        grid_spec=grid_spec,
        out_shape=[jax.ShapeDtypeStruct((t, d), F32), jax.ShapeDtypeStruct((t, d), BF16)],
        compiler_params=_params(("arbitrary",)),
        name="combine_ln",
    )(slot, x, y, gates_t, g, b)


def _rope_tables(seq):
    axis_dim = HEAD_DIM // 2
    rows = seq // GRID_W
    inv = ROPE_THETA ** (-jnp.arange(0, axis_dim, 2, dtype=F32) / axis_dim)
    ang_r = jnp.repeat(jnp.arange(rows, dtype=F32), GRID_W)[:, None] * inv[None, :]
    ang_c = jnp.tile(jnp.arange(GRID_W, dtype=F32), rows)[:, None] * inv[None, :]
    cr, sr, cc, sc = jnp.cos(ang_r), jnp.sin(ang_r), jnp.cos(ang_c), jnp.sin(ang_c)
    cos_t = jnp.concatenate([cr, cr, cc, cc], axis=1)
    sin_t = jnp.concatenate([-sr, sr, -sc, sc], axis=1)
    return cos_t, sin_t


def _dispatch(idx, n_exp, tm):
    t = idx.shape[1]
    n_asg = t * TOP_K
    e_flat = idx.T.reshape(-1)
    onehot = (e_flat[:, None] == jnp.arange(n_exp, dtype=jnp.int32)[None, :]).astype(jnp.int32)
    csum = jnp.cumsum(onehot, axis=0)
    rank = jnp.sum(csum * onehot, axis=1) - 1
    counts = csum[-1]
    pcounts = (counts + tm - 1) // tm * tm
    pends = jnp.cumsum(pcounts)
    pstarts = pends - pcounts
    slot = (pstarts[e_flat] + rank).astype(jnp.int32)
    nblk = n_asg // tm + n_exp
    block_start = jnp.arange(nblk, dtype=jnp.int32) * tm
    block_e = jnp.minimum(jnp.sum(pends[None, :] <= block_start[:, None], axis=-1), n_exp - 1).astype(jnp.int32)
    n_used = (pends[-1] // tm).astype(jnp.int32).reshape(1)
    change_at = jnp.where(block_e[1:] != block_e[:-1], jnp.arange(1, nblk, dtype=jnp.int32), nblk)
    next_start = lax.cummin(jnp.concatenate([change_at, jnp.full((1,), nblk, jnp.int32)]), reverse=True)
    last_run = (next_start >= nblk).astype(jnp.int32)
    next_e = block_e[jnp.minimum(next_start, nblk - 1)]
    valid_rows = jnp.clip(pstarts[block_e] + counts[block_e] - block_start, 0, tm).astype(jnp.int32)
    return (slot, counts.astype(jnp.int32), pstarts.astype(jnp.int32), block_e, n_used, next_e, last_run,
            valid_rows, nblk * tm)


def kernel(x, w_in, q_norm_g, k_norm_g, conv_w, w_attn_proj, w_conv_proj, w_out, ln1_g, ln1_b,
           w_router, router_bias, w_gate, w_up, w_down, ln2_g, ln2_b):
    batch, seq, d = x.shape
    depth = w_in.shape[0]
    t = batch * seq
    attn_w = d // 2
    n_q = attn_w // HEAD_DIM
    n_kv = n_q // GQA_RATIO
    kv_w = n_kv * HEAD_DIM
    conv_wd = d // 2
    n_exp = w_router.shape[1]
    alpha = (2 * depth) ** 0.25
    q_end = attn_w
    k_end = q_end + kv_w
    v_end = k_end + kv_w
    cb_end = v_end + conv_wd
    cc_end = cb_end + conv_wd
    ch_end = cc_end + conv_wd
    ga_end = ch_end + d
    expert_tm = min(512, t)

    cos_t, sin_t = _rope_tables(seq)
    scale = HEAD_DIM ** -0.5 * math.log2(math.e)
    w_router_t = w_router.T
    bias_col = router_bias.reshape(n_exp, 1).astype(F32)
    ln1_g3, ln1_b3 = ln1_g.reshape(depth, 1, d), ln1_b.reshape(depth, 1, d)
    ln2_g3, ln2_b3 = ln2_g.reshape(depth, 1, d), ln2_b.reshape(depth, 1, d)

    xf = x.reshape(t, d)
    xb = xf.astype(BF16)
    for l in range(depth):
        p = _matmul(xb, w_in, l, BF16, name="in_proj")
        gains = jnp.concatenate([jnp.tile(q_norm_g[l] * scale, n_q), jnp.tile(k_norm_g[l], n_kv)]).reshape(1, k_end)
        qk = _qk_prep(p, gains.astype(F32), cos_t, sin_t, seq, k_end)
        attn = _attention(qk, p, batch, seq, n_kv, q_end // HEAD_DIM, k_end // HEAD_DIM)
        conv = _short_conv(p, conv_w, l, seq, v_end, cb_end, cc_end, conv_wd)
        merged = _merge(attn, conv, w_attn_proj, w_conv_proj, p, l, ch_end, ga_end)
        h1 = _matmul(merged, w_out, l, F32, res=xf, alpha=alpha, tm=1024, tn=512, name="out_proj")
        x1, x1p, idx, gate = _ln_route(h1, ln1_g3, ln1_b3, l, w_router_t, bias_col)
        slot, counts, pstarts, block_e, n_used, next_e, last_run, valid_rows, n_rows = _dispatch(
            idx[:TOP_K], n_exp, expert_tm)
        xs = _dispatch_rows(slot, counts, pstarts, n_used, x1p, n_rows, expert_tm)
        y = _experts(block_e, n_used, next_e, last_run, valid_rows, xs, w_gate, w_up, w_down, l, expert_tm)
        xf, xb = _combine_ln(slot, x1, y, gate.T, ln2_g3, ln2_b3, l, alpha)
    return xf.reshape(batch, seq, d)
```

```python
import functools
import math

import jax
import jax.numpy as jnp
from jax import lax
from jax.experimental import pallas as pl
from jax.experimental.pallas import tpu as pltpu

HEAD_DIM = 128
GQA_RATIO = 4
CONV_K = 3
GRID_W = 64
ROPE_THETA = 10000.0
N_GROUPS = 4
TOP_K = 2
RMS_EPS = 1e-6
LN_EPS = 1e-5
SUBLANES = 8
ONES_ROWS = 2 * SUBLANES
ISSUE_UNROLL = 8
VMEM_LIMIT = 56 * 1024 * 1024

F32 = jnp.float32
BF16 = jnp.bfloat16


def _params(semantics):
    return pltpu.CompilerParams(dimension_semantics=semantics, vmem_limit_bytes=VMEM_LIMIT)


def _tile(dim, pref):
    t = min(dim, pref)
    while dim % t:
        t //= 2
    return t


def _pack_bf16_pair(lo, hi):
    lo_b = lax.bitcast_convert_type(lo.astype(BF16).astype(F32), jnp.uint32) >> 16
    hi_b = lax.bitcast_convert_type(hi.astype(BF16).astype(F32), jnp.uint32) & jnp.uint32(0xFFFF0000)
    return hi_b | lo_b


def _unpack_bf16_pair(w):
    lo = lax.bitcast_convert_type(w << 16, F32)
    hi = lax.bitcast_convert_type(w & jnp.uint32(0xFFFF0000), F32)
    return lo, hi


def _stage_weight_tile(w_hbm, stage_ref, cache_ref, sem, layer, tn):
    j = pl.program_id(0)
    i = pl.program_id(1)

    def tile_copy(jj):
        cols = pl.ds(pl.multiple_of(jj * tn, tn), tn)
        return pltpu.make_async_copy(w_hbm.at[layer, :, cols], stage_ref, sem.at[0])

    @pl.when(jnp.logical_and(j == 0, i == 0))
    def _():
        tile_copy(0).start()

    @pl.when(i == 0)
    def _():
        tile_copy(j).wait()
        cache_ref[...] = stage_ref[...].astype(BF16)

        @pl.when(j + 1 < pl.num_programs(0))
        def _():
            tile_copy(j + 1).start()


def _mm_kernel(x_ref, w_hbm, o_ref, ws_ref, wb_ref, sem, *, layer, tn):
    _stage_weight_tile(w_hbm, ws_ref, wb_ref, sem, layer, tn)
    o_ref[...] = jnp.dot(x_ref[...], wb_ref[...], preferred_element_type=F32).astype(o_ref.dtype)


def _mm_res_kernel(x_ref, w_hbm, r_ref, o_ref, ws_ref, wb_ref, sem, *, layer, tn, alpha):
    _stage_weight_tile(w_hbm, ws_ref, wb_ref, sem, layer, tn)
    acc = jnp.dot(x_ref[...], wb_ref[...], preferred_element_type=F32)
    o_ref[...] = (alpha * r_ref[...] + acc).astype(o_ref.dtype)


def _matmul(x, w, layer, out_dtype, *, res=None, alpha=1.0, tm=1024, tn=1024, name="matmul"):
    m, k = x.shape
    n = w.shape[2]
    tm, tn = _tile(m, tm), _tile(n, tn)
    in_specs = [pl.BlockSpec((tm, k), lambda j, i: (i, 0)), pl.BlockSpec(memory_space=pl.ANY)]
    args = [x, w]
    body = functools.partial(_mm_kernel, layer=layer, tn=tn)
    if res is not None:
        in_specs.append(pl.BlockSpec((tm, tn), lambda j, i: (i, j)))
        args.append(res)
        body = functools.partial(_mm_res_kernel, layer=layer, tn=tn, alpha=alpha)
    return pl.pallas_call(
        body,
        grid=(n // tn, m // tm),
        in_specs=in_specs,
        out_specs=pl.BlockSpec((tm, tn), lambda j, i: (i, j)),
        out_shape=jax.ShapeDtypeStruct((m, n), out_dtype),
        scratch_shapes=[pltpu.VMEM((k, tn), F32), pltpu.VMEM((k, tn), BF16), pltpu.SemaphoreType.DMA((1,))],
        compiler_params=_params(("arbitrary", "arbitrary")),
        name=name,
    )(*args)


def _qk_prep_kernel(p_ref, g_ref, c_ref, s_ref, ones_ref, perm_ref, o_ref, *, heads):
    cos = c_ref[...]
    sin = s_ref[...]
    ones = ones_ref[...]
    perm = perm_ref[...]
    for h in range(heads):
        sl = slice(h * HEAD_DIM, (h + 1) * HEAD_DIM)
        xh = p_ref[:, sl].astype(F32)
        ssq = jnp.dot((xh * xh).astype(BF16), ones, preferred_element_type=F32)
        y = xh * lax.rsqrt(ssq * (1.0 / HEAD_DIM) + RMS_EPS) * g_ref[:, sl]
        swapped = jnp.dot(y.astype(BF16), perm, preferred_element_type=F32)
        o_ref[:, sl] = (y * cos + swapped * sin).astype(o_ref.dtype)


def _qk_prep(p, gains, cos_t, sin_t, seq, qk_width, *, tm=512, heads=32):
    t = p.shape[0]
    tm = _tile(seq, tm)
    n_heads = qk_width // HEAD_DIM
    heads = _tile(n_heads, heads)
    wblk = heads * HEAD_DIM
    sblocks = seq // tm
    ones = jnp.ones((HEAD_DIM, HEAD_DIM), BF16)
    axis_dim, half_dim = HEAD_DIM // 2, HEAD_DIM // 4
    lane = jnp.arange(HEAD_DIM)
    partner = jnp.where((lane % axis_dim) < half_dim, lane + half_dim, lane - half_dim)
    perm = (lane[:, None] == partner[None, :]).astype(BF16)
    return pl.pallas_call(
        functools.partial(_qk_prep_kernel, heads=heads),
        grid=(t // tm, n_heads // heads),
        in_specs=[
            pl.BlockSpec((tm, wblk), lambda i, j: (i, j)),
            pl.BlockSpec((1, wblk), lambda i, j: (0, j)),
            pl.BlockSpec((tm, HEAD_DIM), lambda i, j: (i % sblocks, 0)),
            pl.BlockSpec((tm, HEAD_DIM), lambda i, j: (i % sblocks, 0)),
            pl.BlockSpec((HEAD_DIM, HEAD_DIM), lambda i, j: (0, 0)),
            pl.BlockSpec((HEAD_DIM, HEAD_DIM), lambda i, j: (0, 0)),
        ],
        out_specs=pl.BlockSpec((tm, wblk), lambda i, j: (i, j)),
        out_shape=jax.ShapeDtypeStruct((t, qk_width), BF16),
        compiler_params=_params(("parallel", "parallel")),
        name="qk_prep",
    )(p, gains, cos_t, sin_t, ones, perm)


def _flash_kernel(q_ref, k_ref, v_ref, o_ref, qt_ref, vt_ref, *stat_refs, tq, tk, cb, seq, unroll, ahead):
    nkv = seq // tk
    ncb = GQA_RATIO * tq // cb
    acc_refs, m_refs = stat_refs[:ncb], stat_refs[ncb:]

    @pl.when(pl.program_id(2) == 0)
    def _():
        for c in range(nkv):
            vt_ref[c, :HEAD_DIM, :] = v_ref[c * tk:(c + 1) * tk, :].T
            vt_ref[c, HEAD_DIM:, :] = jnp.ones((ONES_ROWS, tk), BF16)

    for g in range(GQA_RATIO):
        qt_ref[:, g * tq:(g + 1) * tq] = q_ref[:, g * HEAD_DIM:(g + 1) * HEAD_DIM].T
    for c in range(ncb):
        m_refs[c][...] = jnp.full(m_refs[c].shape, -jnp.inf, F32)
        acc_refs[c][...] = jnp.zeros(acc_refs[c].shape, F32)

    def body(j, carry):
        tiles = [(u, c) for u in range(unroll) for c in range(ncb)]
        ks, vts = [], []
        for u in range(unroll):
            off = pl.multiple_of((j * unroll + u) * tk, tk)
            ks.append(k_ref[pl.ds(off, tk), :])
            vts.append(vt_ref[j * unroll + u])

        def scores(t):
            u, c = tiles[t]
            return jnp.dot(ks[u], qt_ref[:, c * cb:(c + 1) * cb], preferred_element_type=F32)

        pending = [scores(t) for t in range(min(ahead, len(tiles)))]
        for t, (u, c) in enumerate(tiles):
            st = pending.pop(0)
            if t + ahead < len(tiles):
                pending.append(scores(t + ahead))
            m_old = m_refs[c][...]
            m_new = jnp.maximum(m_old, jnp.max(st, axis=0, keepdims=True))
            alpha = jnp.exp2(m_old - m_new)
            pt = jnp.exp2(st - m_new).astype(BF16)
            m_refs[c][...] = m_new
            acc_refs[c][...] = alpha * acc_refs[c][...] + jnp.dot(vts[u], pt, preferred_element_type=F32)
        return carry

    lax.fori_loop(0, nkv // unroll, body, 0)
    per_g = tq // cb
    for c in range(ncb):
        g, r = divmod(c, per_g)
        acc = acc_refs[c][...]
        out_t = acc[:HEAD_DIM, :] / acc[HEAD_DIM:HEAD_DIM + 1, :]
        o_ref[r * cb:(r + 1) * cb, g * HEAD_DIM:(g + 1) * HEAD_DIM] = out_t.T.astype(o_ref.dtype)


def _attention(qk, p, batch, seq, n_kv, k_col0, v_col0, *, tq=512, tk=256, cb=256, unroll=8, ahead=5):
    tq, tk = _tile(seq, tq), _tile(seq, tk)
    rows = GQA_RATIO * tq
    cb = _tile(rows, cb)
    unroll = _tile(seq // tk, unroll)
    qk3 = qk.reshape(batch, seq, qk.shape[1])
    p3 = p.reshape(batch, seq, p.shape[1])
    gw = GQA_RATIO * HEAD_DIM
    out = pl.pallas_call(
        functools.partial(_flash_kernel, tq=tq, tk=tk, cb=cb, seq=seq, unroll=unroll, ahead=ahead),
        grid=(batch, n_kv, seq // tq),
        in_specs=[
            pl.BlockSpec((None, tq, gw), lambda b, h, i: (b, i, h)),
            pl.BlockSpec((None, seq, HEAD_DIM), lambda b, h, i: (b, 0, k_col0 + h)),
            pl.BlockSpec((None, seq, HEAD_DIM), lambda b, h, i: (b, 0, v_col0 + h)),
        ],
        out_specs=pl.BlockSpec((None, tq, gw), lambda b, h, i: (b, i, h)),
        out_shape=jax.ShapeDtypeStruct((batch, seq, n_kv * gw), BF16),
        scratch_shapes=[
            pltpu.VMEM((HEAD_DIM, rows), BF16),
            pltpu.VMEM((seq // tk, HEAD_DIM + ONES_ROWS, tk), BF16),
        ] + [pltpu.VMEM((HEAD_DIM + ONES_ROWS, cb), F32)] * (rows // cb)
        + [pltpu.VMEM((1, cb), F32)] * (rows // cb),
        compiler_params=_params(("parallel", "parallel", "arbitrary")),
        name="flash_gqa",
    )(qk3, qk3, p3)
    return out.reshape(batch * seq, n_kv * gw)


def _conv_kernel(cb_ref, cc_ref, ch_ref, cbp_ref, chp_ref, cbn_ref, chn_ref, w_ref, o_ref, *, tm, sblocks):
    i = pl.program_id(0)
    first = (i % sblocks) == 0
    last = (i % sblocks) == sblocks - 1
    bx = cb_ref[...].astype(F32) * ch_ref[...].astype(F32)
    prev_row = cbp_ref[SUBLANES - 1:SUBLANES, :].astype(F32) * chp_ref[SUBLANES - 1:SUBLANES, :].astype(F32)
    next_row = cbn_ref[0:1, :].astype(F32) * chn_ref[0:1, :].astype(F32)
    prev_row = jnp.where(first, 0.0, prev_row)
    next_row = jnp.where(last, 0.0, next_row)
    row = lax.broadcasted_iota(jnp.int32, bx.shape, 0)
    prev = jnp.where(row == 0, prev_row, pltpu.roll(bx, 1, axis=0))
    nxt = jnp.where(row == tm - 1, next_row, pltpu.roll(bx, tm - 1, axis=0))
    w = w_ref[...]
    y = w[0:1, :] * prev + w[1:2, :] * bx + w[2:3, :] * nxt
    o_ref[...] = (cc_ref[...].astype(F32) * y).astype(o_ref.dtype)


def _short_conv(p, conv_w, layer, seq, cb0, cc0, ch0, width, *, tm=512, tc=1024):
    t = p.shape[0]
    tm, tc = _tile(seq, tm), math.gcd(cb0, cc0, ch0, width, tc)
    sblocks = seq // tm
    rpb = tm // SUBLANES
    nrb = t // SUBLANES

    def main(c0):
        return pl.BlockSpec((tm, tc), lambda i, j: (i, c0 // tc + j))

    def halo_prev(c0):
        return pl.BlockSpec((SUBLANES, tc), lambda i, j: (jnp.maximum(i * rpb - 1, 0), c0 // tc + j))

    def halo_next(c0):
        return pl.BlockSpec((SUBLANES, tc), lambda i, j: (jnp.minimum((i + 1) * rpb, nrb - 1), c0 // tc + j))

    return pl.pallas_call(
        functools.partial(_conv_kernel, tm=tm, sblocks=sblocks),
        grid=(t // tm, width // tc),
        in_specs=[main(cb0), main(cc0), main(ch0), halo_prev(cb0), halo_prev(ch0),
                  halo_next(cb0), halo_next(ch0),
                  pl.BlockSpec((None, CONV_K, tc), lambda i, j: (layer, 0, j))],
        out_specs=pl.BlockSpec((tm, tc), lambda i, j: (i, j)),
        out_shape=jax.ShapeDtypeStruct((t, width), BF16),
        compiler_params=_params(("parallel", "parallel")),
        name="short_conv",
    )(p, p, p, p, p, p, p, conv_w)


def _merge_kernel(a_ref, c_ref, wa_ref, wc_ref, ga_ref, gc_ref, o_ref, wab_ref, wcb_ref):
    @pl.when(pl.program_id(1) == 0)
    def _():
        wab_ref[...] = wa_ref[...].astype(BF16)
        wcb_ref[...] = wc_ref[...].astype(BF16)

    ya = jnp.dot(a_ref[...], wab_ref[...], preferred_element_type=F32)
    yc = jnp.dot(c_ref[...], wcb_ref[...], preferred_element_type=F32)
    o_ref[...] = (jax.nn.sigmoid(ga_ref[...].astype(F32)) * ya
                  + jax.nn.sigmoid(gc_ref[...].astype(F32)) * yc).astype(o_ref.dtype)


def _merge(attn, conv, wa, wc, p, layer, ga0, gc0, *, tm=1024, tn=512):
    t, ka = attn.shape
    kc = conv.shape[1]
    d = wa.shape[2]
    tm, tn = _tile(t, tm), math.gcd(ga0, gc0, d, tn)
    return pl.pallas_call(
        _merge_kernel,
        grid=(d // tn, t // tm),
        in_specs=[
            pl.BlockSpec((tm, ka), lambda j, i: (i, 0)),
            pl.BlockSpec((tm, kc), lambda j, i: (i, 0)),
            pl.BlockSpec((None, ka, tn), lambda j, i: (layer, 0, j)),
            pl.BlockSpec((None, kc, tn), lambda j, i: (layer, 0, j)),
            pl.BlockSpec((tm, tn), lambda j, i: (i, ga0 // tn + j)),
            pl.BlockSpec((tm, tn), lambda j, i: (i, gc0 // tn + j)),
        ],
        out_specs=pl.BlockSpec((tm, tn), lambda j, i: (i, j)),
        out_shape=jax.ShapeDtypeStruct((t, d), BF16),
        scratch_shapes=[pltpu.VMEM((ka, tn), BF16), pltpu.VMEM((kc, tn), BF16)],
        compiler_params=_params(("parallel", "arbitrary")),
        name="gated_merge",
    )(attn, conv, wa, wc, p, p)


def _ln_rows(h, g, b):
    mu = jnp.mean(h, axis=-1, keepdims=True)
    hc = h - mu
    var = jnp.mean(hc * hc, axis=-1, keepdims=True)
    return hc * lax.rsqrt(var + LN_EPS) * g + b


def _route(logits, bias, n_exp):
    epg = n_exp // N_GROUPS
    mx = jnp.max(logits, axis=0, keepdims=True)
    ex = jnp.exp(logits - mx)
    scores = ex / jnp.sum(ex, axis=0, keepdims=True)
    sel = scores + bias
    rows_sel = [sel[e:e + 1, :] for e in range(n_exp)]
    rows_sc = [scores[e:e + 1, :] for e in range(n_exp)]
    best = None
    for g in range(N_GROUPS):
        mem = rows_sel[g * epg:(g + 1) * epg]
        gs = None
        for a in range(epg):
            for c in range(a + 1, epg):
                pair = mem[a] + mem[c]
                gs = pair if gs is None else jnp.maximum(gs, pair)
        if best is None:
            best, grp = gs, jnp.zeros(gs.shape, jnp.int32)
        else:
            upd = gs > best
            best = jnp.where(upd, gs, best)
            grp = jnp.where(upd, g, grp)
    cand_sel, cand_sc = [], []
    for j in range(epg):
        cs, cc = rows_sel[j], rows_sc[j]
        for g in range(1, N_GROUPS):
            cs = jnp.where(grp == g, rows_sel[g * epg + j], cs)
            cc = jnp.where(grp == g, rows_sc[g * epg + j], cc)
        cand_sel.append(cs)
        cand_sc.append(cc)

    def first_argmax(vals, skip):
        bv = bi = bs = None
        for j in range(epg):
            v = vals[j] if skip is None else jnp.where(skip == j, -jnp.inf, vals[j])
            if bv is None:
                bv, bi, bs = v, jnp.zeros(v.shape, jnp.int32), cand_sc[0]
            else:
                upd = v > bv
                bv = jnp.where(upd, v, bv)
                bi = jnp.where(upd, j, bi)
                bs = jnp.where(upd, cand_sc[j], bs)
        return bi, bs

    i1, s1 = first_argmax(cand_sel, None)
    i2, s2 = first_argmax(cand_sel, i1)
    tot = s1 + s2
    zi = jnp.zeros((SUBLANES - TOP_K,) + i1.shape[1:], jnp.int32)
    zf = jnp.zeros((SUBLANES - TOP_K,) + i1.shape[1:], F32)
    idx = jnp.concatenate([grp * epg + i1, grp * epg + i2, zi], axis=0)
    gate = jnp.concatenate([s1 / tot, s2 / tot, zf], axis=0)
    return idx, gate


def _ln_route_kernel(h_ref, g_ref, b_ref, wh_ref, wl_ref, rb_ref, o_ref, op_ref, idx_ref, gate_ref, *, n_exp):
    y = _ln_rows(h_ref[...], g_ref[...], b_ref[...])
    half = y.shape[1] // 2
    o_ref[...] = y
    op_ref[...] = _pack_bf16_pair(y[:, :half], y[:, half:])
    yh = y.astype(BF16)
    yl = (y - yh.astype(F32)).astype(BF16)
    nt = (((1,), (1,)), ((), ()))
    logits = (lax.dot_general(wh_ref[...], yh, nt, preferred_element_type=F32)
              + lax.dot_general(wh_ref[...], yl, nt, preferred_element_type=F32)
              + lax.dot_general(wl_ref[...], yh, nt, preferred_element_type=F32))
    idx_ref[...], gate_ref[...] = _route(logits, rb_ref[...], n_exp)


def _ln_route(h, g, b, layer, w_router_t, bias_col, *, tm=256):
    t, d = h.shape
    n_exp = w_router_t.shape[0]
    tm = _tile(t, tm)
    wh = w_router_t.astype(BF16)
    wl = (w_router_t - wh.astype(F32)).astype(BF16)
    row = pl.BlockSpec((tm, d), lambda i: (i, 0))
    vec = pl.BlockSpec((None, 1, d), lambda i: (layer, 0, 0))
    wsp = pl.BlockSpec((n_exp, d), lambda i: (0, 0))
    rsp = pl.BlockSpec((SUBLANES, tm), lambda i: (0, i))
    return pl.pallas_call(
        functools.partial(_ln_route_kernel, n_exp=n_exp),
        grid=(t // tm,),
        in_specs=[row, vec, vec, wsp, wsp, pl.BlockSpec((n_exp, 1), lambda i: (0, 0))],
        out_specs=[row, pl.BlockSpec((tm, d // 2), lambda i: (i, 0)), rsp, rsp],
        out_shape=[jax.ShapeDtypeStruct((t, d), F32), jax.ShapeDtypeStruct((t, d // 2), jnp.uint32),
                   jax.ShapeDtypeStruct((SUBLANES, t), jnp.int32), jax.ShapeDtypeStruct((SUBLANES, t), F32)],
        compiler_params=_params(("parallel",)),
        name="ln_route",
    )(h, g, b, wh, wl, bias_col)


def _dispatch_kernel(slot_ref, cnt_ref, pst_ref, nu_ref, x_ref, o_hbm, z_ref, sem, *, tm, n_exp, epp, nblk):
    i = pl.program_id(0)
    n = pl.num_programs(0)

    @pl.when(i == 0)
    def _():
        z_ref[...] = jnp.zeros_like(z_ref)

    def issue(g, c):
        for u in range(ISSUE_UNROLL):
            r = g * ISSUE_UNROLL + u
            for k in range(TOP_K):
                pltpu.make_async_copy(x_ref.at[pl.ds(r, 1), :],
                                      o_hbm.at[pl.ds(slot_ref[(i * tm + r) * TOP_K + k], 1), :],
                                      sem.at[0]).start(priority=k % 2)
        return c

    lax.fori_loop(0, tm // ISSUE_UNROLL, issue, 0)

    def wait_rows(rows):
        pltpu.make_async_copy(z_ref.at[pl.ds(0, rows), :], o_hbm.at[pl.ds(0, rows), :], sem.at[0]).wait()

    for j in range(epp):
        e = i * epp + j

        @pl.when(e < n_exp)
        def _():
            cnt = cnt_ref[e]
            first = pst_ref[e] + cnt
            npad = (tm - cnt % tm) % tm

            def zero(r, c):
                pltpu.make_async_copy(z_ref.at[pl.ds(0, 1), :], o_hbm.at[pl.ds(first + r, 1), :], sem.at[0]).start()
                return c

            def wait_one(r, c):
                wait_rows(1)
                return c

            lax.fori_loop(0, npad, zero, 0)
            lax.fori_loop(0, npad, wait_one, 0)

    for k in range(TOP_K):
        wait_rows(tm)

    @pl.when(i == n - 1)
    def _():
        def zero_block(b, c):
            cp = pltpu.make_async_copy(z_ref, o_hbm.at[pl.ds(pl.multiple_of(b * tm, tm), tm), :], sem.at[0])
            cp.start()
            cp.wait()
            return c

        lax.fori_loop(nu_ref[0], nblk, zero_block, 0)


def _dispatch_rows(slot, counts, pstarts, n_used, xp, n_rows, tm):
    t, w = xp.shape
    n_exp = counts.shape[0]
    assert t % tm == 0 and n_rows % tm == 0
    n_steps = t // tm
    epp = -(-n_exp // n_steps)
    grid_spec = pltpu.PrefetchScalarGridSpec(
        num_scalar_prefetch=4,
        grid=(n_steps,),
        in_specs=[pl.BlockSpec((tm, w), lambda i, sl, cn, ps, nu: (i, 0))],
        out_specs=pl.BlockSpec(memory_space=pl.ANY),
        scratch_shapes=[pltpu.VMEM((tm, w), xp.dtype), pltpu.SemaphoreType.DMA((1,))],
    )
    return pl.pallas_call(
        functools.partial(_dispatch_kernel, tm=tm, n_exp=n_exp, epp=epp, nblk=n_rows // tm),
        grid_spec=grid_spec,
        out_shape=jax.ShapeDtypeStruct((n_rows, w), xp.dtype),
        compiler_params=_params(("arbitrary",)),
        name="dispatch_rows",
    )(slot, counts, pstarts, n_used, xp)


def _new_expert(be_ref, i):
    return jnp.logical_or(i == 0, be_ref[i] != be_ref[jnp.maximum(i - 1, 0)])


def _stage_weights(be_ref, nx_ref, lr_ref, copies, stage_refs, cache_refs):
    t = pl.program_id(0)
    i = pl.program_id(1)
    n_t = pl.num_programs(0)

    def start(e, tile):
        for cp in copies(e, tile):
            cp.start()

    @pl.when(jnp.logical_and(t == 0, i == 0))
    def _():
        start(be_ref[0], 0)

    @pl.when(_new_expert(be_ref, i))
    def _():
        for cp in copies(be_ref[i], t):
            cp.wait()
        for st, ca in zip(stage_refs, cache_refs):
            ca[...] = st[...].astype(BF16)
        last_run = lr_ref[i] == 1

        @pl.when(jnp.logical_not(last_run))
        def _():
            start(nx_ref[i], t)

        @pl.when(jnp.logical_and(last_run, t + 1 < n_t))
        def _():
            start(be_ref[0], t + 1)


def _half_block_compute(valid, compute, o_ref):
    tm = o_ref.shape[0]
    hb = tm // 2

    @pl.when(valid > hb)
    def _():
        o_ref[...] = compute(slice(None))

    @pl.when(jnp.logical_and(valid > 0, valid <= hb))
    def _():
        o_ref[:hb, :] = compute(slice(0, hb))
        o_ref[hb:, :] = jnp.zeros((tm - hb, o_ref.shape[1]), o_ref.dtype)

    @pl.when(valid == 0)
    def _():
        o_ref[...] = jnp.zeros_like(o_ref)


def _expert_up_kernel(be_ref, nu_ref, nx_ref, lr_ref, vr_ref, x_ref, wg_hbm, wu_hbm, o_ref,
                      wgf_ref, wuf_ref, wgb_ref, wub_ref, sem, *, layer, tf):
    i = pl.program_id(1)

    def copies(e, tile):
        cols = pl.ds(pl.multiple_of(tile * tf, tf), tf)
        return (pltpu.make_async_copy(wg_hbm.at[layer, e, :, cols], wgf_ref, sem.at[0]),
                pltpu.make_async_copy(wu_hbm.at[layer, e, :, cols], wuf_ref, sem.at[1]))

    _stage_weights(be_ref, nx_ref, lr_ref, copies, (wgf_ref, wuf_ref), (wgb_ref, wub_ref))

    def hidden(rows):
        lo, hi = _unpack_bf16_pair(x_ref[rows, :])
        lo, hi = lo.astype(BF16), hi.astype(BF16)
        half = lo.shape[1]
        hg = (jnp.dot(lo, wgb_ref[:half, :], preferred_element_type=F32)
              + jnp.dot(hi, wgb_ref[half:, :], preferred_element_type=F32))
        hu = (jnp.dot(lo, wub_ref[:half, :], preferred_element_type=F32)
              + jnp.dot(hi, wub_ref[half:, :], preferred_element_type=F32))
        return (hg * jax.nn.sigmoid(hg) * hu).astype(o_ref.dtype)

    _half_block_compute(vr_ref[i], hidden, o_ref)


def _expert_down_kernel(be_ref, nu_ref, nx_ref, lr_ref, vr_ref, h_ref, wd_hbm, o_ref,
                        wlof_ref, whif_ref, wlob_ref, whib_ref, sem, *, layer, tn, hi0):
    i = pl.program_id(1)

    def copies(e, tile):
        lo = pl.ds(pl.multiple_of(tile * tn, tn), tn)
        hi = pl.ds(pl.multiple_of((hi0 + tile) * tn, tn), tn)
        return (pltpu.make_async_copy(wd_hbm.at[layer, e, :, lo], wlof_ref, sem.at[0]),
                pltpu.make_async_copy(wd_hbm.at[layer, e, :, hi], whif_ref, sem.at[1]))

    _stage_weights(be_ref, nx_ref, lr_ref, copies, (wlof_ref, whif_ref), (wlob_ref, whib_ref))

    def rows_out(rows):
        h = h_ref[rows, :]
        return _pack_bf16_pair(jnp.dot(h, wlob_ref[...], preferred_element_type=F32),
                               jnp.dot(h, whib_ref[...], preferred_element_type=F32))

    _half_block_compute(vr_ref[i], rows_out, o_ref)


def _experts(block_e, n_used, next_e, last_run, valid_rows, xs, wg, wu, wd, layer, tm, *, tf=512, tn=1024):
    r = xs.shape[0]
    d = 2 * xs.shape[1]
    ff = wg.shape[3]
    tf = _tile(ff, tf)
    nblk = r // tm
    hbm = pl.BlockSpec(memory_space=pl.ANY)
    hid = pl.pallas_call(
        functools.partial(_expert_up_kernel, layer=layer, tf=tf),
        grid_spec=pltpu.PrefetchScalarGridSpec(
            num_scalar_prefetch=5,
            grid=(ff // tf, nblk),
            in_specs=[
                pl.BlockSpec((tm, d // 2), lambda f, i, be, nu, nx, lr, vr: (jnp.minimum(i, nu[0] - 1), 0)),
                hbm, hbm,
            ],
            out_specs=pl.BlockSpec((tm, tf), lambda f, i, be, nu, nx, lr, vr: (i, f)),
            scratch_shapes=[pltpu.VMEM((d, tf), F32), pltpu.VMEM((d, tf), F32),
                            pltpu.VMEM((d, tf), BF16), pltpu.VMEM((d, tf), BF16),
                            pltpu.SemaphoreType.DMA((2,))],
        ),
        out_shape=jax.ShapeDtypeStruct((r, ff), BF16),
        compiler_params=_params(("arbitrary", "arbitrary")),
        name="expert_up",
    )(block_e, n_used, next_e, last_run, valid_rows, xs, wg, wu)
    half = d // 2
    tn = _tile(half, tn)
    hi0 = half // tn
    return pl.pallas_call(
        functools.partial(_expert_down_kernel, layer=layer, tn=tn, hi0=hi0),
        grid_spec=pltpu.PrefetchScalarGridSpec(
            num_scalar_prefetch=5,
            grid=(half // tn, nblk),
            in_specs=[pl.BlockSpec((tm, ff), lambda n, i, be, nu, nx, lr, vr: (i, 0)), hbm],
            out_specs=pl.BlockSpec((tm, tn), lambda n, i, be, nu, nx, lr, vr: (i, n)),
            scratch_shapes=[pltpu.VMEM((ff, tn), F32), pltpu.VMEM((ff, tn), F32),
                            pltpu.VMEM((ff, tn), BF16), pltpu.VMEM((ff, tn), BF16),
                            pltpu.SemaphoreType.DMA((2,))],
        ),
        out_shape=jax.ShapeDtypeStruct((r, half), jnp.uint32),
        compiler_params=_params(("arbitrary", "arbitrary")),
        name="expert_down",
    )(block_e, n_used, next_e, last_run, valid_rows, hid, wd)


def _combine_kernel(slot_ref, x_ref, y_hbm, gt_ref, g_ref, b_ref, o_ref, ob_ref, ybuf0, ybuf1, sem, *, alpha, tm):
    i = pl.program_id(0)
    n = pl.num_programs(0)
    ybufs = (ybuf0, ybuf1)

    def row_copy(row, s, k, r):
        return pltpu.make_async_copy(y_hbm.at[pl.ds(row, 1), :], ybufs[s].at[k, pl.ds(r, 1), :], sem.at[s])

    def wait_gather(s):
        for k in range(TOP_K):
            pltpu.make_async_copy(y_hbm.at[pl.ds(0, tm), :], ybufs[s].at[k], sem.at[s]).wait()

    @pl.when(i == 0)
    def _():
        def issue(r, c):
            for k in range(TOP_K):
                row_copy(slot_ref[r * TOP_K + k], 0, k, r).start(priority=k % 2)
            return c

        lax.fori_loop(0, tm, issue, 0, unroll=4)

    def step(s):
        base = jnp.minimum(i + 1, n - 1) * (tm * TOP_K)
        for r in range(tm):
            for k in range(TOP_K):
                row_copy(slot_ref[base + r * TOP_K + k], 1 - s, k, r).start(priority=k % 2)
        wait_gather(s)
        gt = gt_ref[...]
        a_lo, a_hi = _unpack_bf16_pair(ybufs[s][0])
        b_lo, b_hi = _unpack_bf16_pair(ybufs[s][1])
        g0, g1 = gt[:, 0:1], gt[:, 1:2]
        ffn = jnp.concatenate([g0 * a_lo + g1 * b_lo, g0 * a_hi + g1 * b_hi], axis=1)
        y = _ln_rows(alpha * x_ref[...] + ffn, g_ref[...], b_ref[...])
        o_ref[...] = y
        ob_ref[...] = y.astype(BF16)

        @pl.when(i == n - 1)
        def _():
            wait_gather(1 - s)

    for s in range(2):
        pl.when(i % 2 == s)(functools.partial(step, s))


def _combine_ln(slot, x, y, gates_t, g, b, layer, alpha, *, tm=256):
    t, d = x.shape
    tm = _tile(t, tm)
    row = pl.BlockSpec((tm, d), lambda i, sl: (i, 0))
    vec = pl.BlockSpec((None, 1, d), lambda i, sl: (layer, 0, 0))
    grid_spec = pltpu.PrefetchScalarGridSpec(
        num_scalar_prefetch=1,
        grid=(t // tm,),
        in_specs=[row, pl.BlockSpec(memory_space=pl.ANY),
                  pl.BlockSpec((tm, SUBLANES), lambda i, sl: (i, 0)), vec, vec],
        out_specs=[row, row],
        scratch_shapes=[pltpu.VMEM((TOP_K, tm, d // 2), jnp.uint32), pltpu.VMEM((TOP_K, tm, d // 2), jnp.uint32),
                        pltpu.SemaphoreType.DMA((2,))],
    )
    return pl.pallas_call(
        functools.partial(_combine_kernel, alpha=alpha, tm=tm),
        grid_spec=grid_spec,
        out_shape=[jax.ShapeDtypeStruct((t, d), F32), jax.ShapeDtypeStruct((t, d), BF16)],
        compiler_params=_params(("arbitrary",)),
        name="combine_ln",
    )(slot, x, y, gates_t, g, b)


def _rope_tables(seq):
    axis_dim = HEAD_DIM // 2
    rows = seq // GRID_W
    inv = ROPE_THETA ** (-jnp.arange(0, axis_dim, 2, dtype=F32) / axis_dim)
    ang_r = jnp.repeat(jnp.arange(rows, dtype=F32), GRID_W)[:, None] * inv[None, :]
    ang_c = jnp.tile(jnp.arange(GRID_W, dtype=F32), rows)[:, None] * inv[None, :]
    cr, sr, cc, sc = jnp.cos(ang_r), jnp.sin(ang_r), jnp.cos(ang_c), jnp.sin(ang_c)
    cos_t = jnp.concatenate([cr, cr, cc, cc], axis=1)
    sin_t = jnp.concatenate([-sr, sr, -sc, sc], axis=1)
    return cos_t, sin_t


def _dispatch(idx, n_exp, tm):
    t = idx.shape[1]
    n_asg = t * TOP_K
    e_flat = idx.T.reshape(-1)
    onehot = (e_flat[:, None] == jnp.arange(n_exp, dtype=jnp.int32)[None, :]).astype(jnp.int32)
    csum = jnp.cumsum(onehot, axis=0)
    rank = jnp.sum(csum * onehot, axis=1) - 1
    counts = csum[-1]
    pcounts = (counts + tm - 1) // tm * tm
    pends = jnp.cumsum(pcounts)
    pstarts = pends - pcounts
    slot = (pstarts[e_flat] + rank).astype(jnp.int32)
    nblk = n_asg // tm + n_exp
    block_start = jnp.arange(nblk, dtype=jnp.int32) * tm
    block_e = jnp.minimum(jnp.sum(pends[None, :] <= block_start[:, None], axis=-1), n_exp - 1).astype(jnp.int32)
    n_used = (pends[-1] // tm).astype(jnp.int32).reshape(1)
    change_at = jnp.where(block_e[1:] != block_e[:-1], jnp.arange(1, nblk, dtype=jnp.int32), nblk)
    next_start = lax.cummin(jnp.concatenate([change_at, jnp.full((1,), nblk, jnp.int32)]), reverse=True)
    last_run = (next_start >= nblk).astype(jnp.int32)
    next_e = block_e[jnp.minimum(next_start, nblk - 1)]
    valid_rows = jnp.clip(pstarts[block_e] + counts[block_e] - block_start, 0, tm).astype(jnp.int32)
    return (slot, counts.astype(jnp.int32), pstarts.astype(jnp.int32), block_e, n_used, next_e, last_run,
            valid_rows, nblk * tm)


def kernel(x, w_in, q_norm_g, k_norm_g, conv_w, w_attn_proj, w_conv_proj, w_out, ln1_g, ln1_b,
           w_router, router_bias, w_gate, w_up, w_down, ln2_g, ln2_b):
    batch, seq, d = x.shape
    depth = w_in.shape[0]
    t = batch * seq
    attn_w = d // 2
    n_q = attn_w // HEAD_DIM
    n_kv = n_q // GQA_RATIO
    kv_w = n_kv * HEAD_DIM
    conv_wd = d // 2
    n_exp = w_router.shape[1]
    alpha = (2 * depth) ** 0.25
    q_end = attn_w
    k_end = q_end + kv_w
    v_end = k_end + kv_w
    cb_end = v_end + conv_wd
    cc_end = cb_end + conv_wd
    ch_end = cc_end + conv_wd
    ga_end = ch_end + d
    expert_tm = min(512, t)

    cos_t, sin_t = _rope_tables(seq)
    scale = HEAD_DIM ** -0.5 * math.log2(math.e)
    w_router_t = w_router.T
    bias_col = router_bias.reshape(n_exp, 1).astype(F32)
    ln1_g3, ln1_b3 = ln1_g.reshape(depth, 1, d), ln1_b.reshape(depth, 1, d)
    ln2_g3, ln2_b3 = ln2_g.reshape(depth, 1, d), ln2_b.reshape(depth, 1, d)

    xf = x.reshape(t, d)
    xb = xf.astype(BF16)
    for l in range(depth):
        p = _matmul(xb, w_in, l, BF16, name="in_proj")
        gains = jnp.concatenate([jnp.tile(q_norm_g[l] * scale, n_q), jnp.tile(k_norm_g[l], n_kv)]).reshape(1, k_end)
        qk = _qk_prep(p, gains.astype(F32), cos_t, sin_t, seq, k_end)
        attn = _attention(qk, p, batch, seq, n_kv, q_end // HEAD_DIM, k_end // HEAD_DIM)
        conv = _short_conv(p, conv_w, l, seq, v_end, cb_end, cc_end, conv_wd)
        merged = _merge(attn, conv, w_attn_proj, w_conv_proj, p, l, ch_end, ga_end)
        h1 = _matmul(merged, w_out, l, F32, res=xf, alpha=alpha, tm=1024, tn=512, name="out_proj")
        x1, x1p, idx, gate = _ln_route(h1, ln1_g3, ln1_b3, l, w_router_t, bias_col)
        slot, counts, pstarts, block_e, n_used, next_e, last_run, valid_rows, n_rows = _dispatch(
            idx[:TOP_K], n_exp, expert_tm)
        xs = _dispatch_rows(slot, counts, pstarts, n_used, x1p, n_rows, expert_tm)
        y = _experts(block_e, n_used, next_e, last_run, valid_rows, xs, w_gate, w_up, w_down, l, expert_tm)
        xf, xb = _combine_ln(slot, x1, y, gate.T, ln2_g3, ln2_b3, l, alpha)
    return xf.reshape(batch, seq, d)
```

```python
import functools
import math

import jax
import jax.numpy as jnp
from jax import lax
from jax.experimental import pallas as pl
from jax.experimental.pallas import tpu as pltpu

HEAD_DIM = 128
GQA_RATIO = 4
CONV_K = 3
GRID_W = 64
ROPE_THETA = 10000.0
N_GROUPS = 4
TOP_K = 2
RMS_EPS = 1e-6
LN_EPS = 1e-5
SUBLANES = 8
ONES_ROWS = 2 * SUBLANES
ISSUE_UNROLL = 8
VMEM_LIMIT = 56 * 1024 * 1024

F32 = jnp.float32
BF16 = jnp.bfloat16


def _params(semantics):
    return pltpu.CompilerParams(dimension_semantics=semantics, vmem_limit_bytes=VMEM_LIMIT)


def _tile(dim, pref):
    t = min(dim, pref)
    while dim % t:
        t //= 2
    return t


def _pack_bf16_pair(lo, hi):
    lo_b = lax.bitcast_convert_type(lo.astype(BF16).astype(F32), jnp.uint32) >> 16
    hi_b = lax.bitcast_convert_type(hi.astype(BF16).astype(F32), jnp.uint32) & jnp.uint32(0xFFFF0000)
    return hi_b | lo_b


def _unpack_bf16_pair(w):
    lo = lax.bitcast_convert_type(w << 16, F32)
    hi = lax.bitcast_convert_type(w & jnp.uint32(0xFFFF0000), F32)
    return lo, hi


def _stage_weight_tile(w_hbm, stage_ref, cache_ref, sem, layer, tn):
    j = pl.program_id(0)
    i = pl.program_id(1)

    def tile_copy(jj):
        cols = pl.ds(pl.multiple_of(jj * tn, tn), tn)
        return pltpu.make_async_copy(w_hbm.at[layer, :, cols], stage_ref, sem.at[0])

    @pl.when(jnp.logical_and(j == 0, i == 0))
    def _():
        tile_copy(0).start()

    @pl.when(i == 0)
    def _():
        tile_copy(j).wait()
        cache_ref[...] = stage_ref[...].astype(BF16)

        @pl.when(j + 1 < pl.num_programs(0))
        def _():
            tile_copy(j + 1).start()


def _mm_kernel(x_ref, w_hbm, o_ref, ws_ref, wb_ref, sem, *, layer, tn):
    _stage_weight_tile(w_hbm, ws_ref, wb_ref, sem, layer, tn)
    o_ref[...] = jnp.dot(x_ref[...], wb_ref[...], preferred_element_type=F32).astype(o_ref.dtype)


def _mm_res_kernel(x_ref, w_hbm, r_ref, o_ref, ws_ref, wb_ref, sem, *, layer, tn, alpha):
    _stage_weight_tile(w_hbm, ws_ref, wb_ref, sem, layer, tn)
    acc = jnp.dot(x_ref[...], wb_ref[...], preferred_element_type=F32)
    o_ref[...] = (alpha * r_ref[...] + acc).astype(o_ref.dtype)


def _matmul(x, w, layer, out_dtype, *, res=None, alpha=1.0, tm=1024, tn=1024, name="matmul"):
    m, k = x.shape
    n = w.shape[2]
    tm, tn = _tile(m, tm), _tile(n, tn)
    in_specs = [pl.BlockSpec((tm, k), lambda j, i: (i, 0)), pl.BlockSpec(memory_space=pl.ANY)]
    args = [x, w]
    body = functools.partial(_mm_kernel, layer=layer, tn=tn)
    if res is not None:
        in_specs.append(pl.BlockSpec((tm, tn), lambda j, i: (i, j)))
        args.append(res)
        body = functools.partial(_mm_res_kernel, layer=layer, tn=tn, alpha=alpha)
    return pl.pallas_call(
        body,
        grid=(n // tn, m // tm),
        in_specs=in_specs,
        out_specs=pl.BlockSpec((tm, tn), lambda j, i: (i, j)),
        out_shape=jax.ShapeDtypeStruct((m, n), out_dtype),
        scratch_shapes=[pltpu.VMEM((k, tn), F32), pltpu.VMEM((k, tn), BF16), pltpu.SemaphoreType.DMA((1,))],
        compiler_params=_params(("arbitrary", "arbitrary")),
        name=name,
    )(*args)


def _qk_prep_kernel(p_ref, g_ref, c_ref, s_ref, ones_ref, perm_ref, o_ref, *, heads):
    cos = c_ref[...]
    sin = s_ref[...]
    ones = ones_ref[...]
    perm = perm_ref[...]
    for h in range(heads):
        sl = slice(h * HEAD_DIM, (h + 1) * HEAD_DIM)
        xh = p_ref[:, sl].astype(F32)
        ssq = jnp.dot((xh * xh).astype(BF16), ones, preferred_element_type=F32)
        y = xh * lax.rsqrt(ssq * (1.0 / HEAD_DIM) + RMS_EPS) * g_ref[:, sl]
        swapped = jnp.dot(y.astype(BF16), perm, preferred_element_type=F32)
        o_ref[:, sl] = (y * cos + swapped * sin).astype(o_ref.dtype)


def _qk_prep(p, gains, cos_t, sin_t, seq, qk_width, *, tm=512, heads=32):
    t = p.shape[0]
    tm = _tile(seq, tm)
    n_heads = qk_width // HEAD_DIM
    heads = _tile(n_heads, heads)
    wblk = heads * HEAD_DIM
    sblocks = seq // tm
    ones = jnp.ones((HEAD_DIM, HEAD_DIM), BF16)
    axis_dim, half_dim = HEAD_DIM // 2, HEAD_DIM // 4
    lane = jnp.arange(HEAD_DIM)
    partner = jnp.where((lane % axis_dim) < half_dim, lane + half_dim, lane - half_dim)
    perm = (lane[:, None] == partner[None, :]).astype(BF16)
    return pl.pallas_call(
        functools.partial(_qk_prep_kernel, heads=heads),
        grid=(t // tm, n_heads // heads),
        in_specs=[
            pl.BlockSpec((tm, wblk), lambda i, j: (i, j)),
            pl.BlockSpec((1, wblk), lambda i, j: (0, j)),
            pl.BlockSpec((tm, HEAD_DIM), lambda i, j: (i % sblocks, 0)),
            pl.BlockSpec((tm, HEAD_DIM), lambda i, j: (i % sblocks, 0)),
            pl.BlockSpec((HEAD_DIM, HEAD_DIM), lambda i, j: (0, 0)),
            pl.BlockSpec((HEAD_DIM, HEAD_DIM), lambda i, j: (0, 0)),
        ],
        out_specs=pl.BlockSpec((tm, wblk), lambda i, j: (i, j)),
        out_shape=jax.ShapeDtypeStruct((t, qk_width), BF16),
        compiler_params=_params(("parallel", "parallel")),
        name="qk_prep",
    )(p, gains, cos_t, sin_t, ones, perm)


def _flash_kernel(q_ref, k_ref, v_ref, o_ref, qt_ref, vt_ref, *stat_refs, tq, tk, cb, seq, unroll, ahead):
    nkv = seq // tk
    ncb = GQA_RATIO * tq // cb
    acc_refs, m_refs = stat_refs[:ncb], stat_refs[ncb:]

    @pl.when(pl.program_id(2) == 0)
    def _():
        for c in range(nkv):
            vt_ref[c, :HEAD_DIM, :] = v_ref[c * tk:(c + 1) * tk, :].T
            vt_ref[c, HEAD_DIM:, :] = jnp.ones((ONES_ROWS, tk), BF16)

    for g in range(GQA_RATIO):
        qt_ref[:, g * tq:(g + 1) * tq] = q_ref[:, g * HEAD_DIM:(g + 1) * HEAD_DIM].T
    for c in range(ncb):
        m_refs[c][...] = jnp.full(m_refs[c].shape, -jnp.inf, F32)
        acc_refs[c][...] = jnp.zeros(acc_refs[c].shape, F32)

    def body(j, carry):
        tiles = [(u, c) for u in range(unroll) for c in range(ncb)]
        ks, vts = [], []
        for u in range(unroll):
            off = pl.multiple_of((j * unroll + u) * tk, tk)
            ks.append(k_ref[pl.ds(off, tk), :])
            vts.append(vt_ref[j * unroll + u])

        def scores(t):
            u, c = tiles[t]
            return jnp.dot(ks[u], qt_ref[:, c * cb:(c + 1) * cb], preferred_element_type=F32)

        pending = [scores(t) for t in range(min(ahead, len(tiles)))]
        for t, (u, c) in enumerate(tiles):
            st = pending.pop(0)
            if t + ahead < len(tiles):
                pending.append(scores(t + ahead))
            m_old = m_refs[c][...]
            m_new = jnp.maximum(m_old, jnp.max(st, axis=0, keepdims=True))
            alpha = jnp.exp2(m_old - m_new)
            pt = jnp.exp2(st - m_new).astype(BF16)
            m_refs[c][...] = m_new
            acc_refs[c][...] = alpha * acc_refs[c][...] + jnp.dot(vts[u], pt, preferred_element_type=F32)
        return carry

    lax.fori_loop(0, nkv // unroll, body, 0)
    per_g = tq // cb
    for c in range(ncb):
        g, r = divmod(c, per_g)
        acc = acc_refs[c][...]
        out_t = acc[:HEAD_DIM, :] / acc[HEAD_DIM:HEAD_DIM + 1, :]
        o_ref[r * cb:(r + 1) * cb, g * HEAD_DIM:(g + 1) * HEAD_DIM] = out_t.T.astype(o_ref.dtype)


def _attention(qk, p, batch, seq, n_kv, k_col0, v_col0, *, tq=512, tk=256, cb=256, unroll=16, ahead=5):
    tq, tk = _tile(seq, tq), _tile(seq, tk)
    rows = GQA_RATIO * tq
    cb = _tile(rows, cb)
    unroll = _tile(seq // tk, unroll)
    qk3 = qk.reshape(batch, seq, qk.shape[1])
    p3 = p.reshape(batch, seq, p.shape[1])
    gw = GQA_RATIO * HEAD_DIM
    out = pl.pallas_call(
        functools.partial(_flash_kernel, tq=tq, tk=tk, cb=cb, seq=seq, unroll=unroll, ahead=ahead),
        grid=(batch, n_kv, seq // tq),
        in_specs=[
            pl.BlockSpec((None, tq, gw), lambda b, h, i: (b, i, h)),
            pl.BlockSpec((None, seq, HEAD_DIM), lambda b, h, i: (b, 0, k_col0 + h)),
            pl.BlockSpec((None, seq, HEAD_DIM), lambda b, h, i: (b, 0, v_col0 + h)),
        ],
        out_specs=pl.BlockSpec((None, tq, gw), lambda b, h, i: (b, i, h)),
        out_shape=jax.ShapeDtypeStruct((batch, seq, n_kv * gw), BF16),
        scratch_shapes=[
            pltpu.VMEM((HEAD_DIM, rows), BF16),
            pltpu.VMEM((seq // tk, HEAD_DIM + ONES_ROWS, tk), BF16),
        ] + [pltpu.VMEM((HEAD_DIM + ONES_ROWS, cb), F32)] * (rows // cb)
        + [pltpu.VMEM((1, cb), F32)] * (rows // cb),
        compiler_params=_params(("parallel", "parallel", "arbitrary")),
        name="flash_gqa",
    )(qk3, qk3, p3)
    return out.reshape(batch * seq, n_kv * gw)


def _conv_kernel(cb_ref, cc_ref, ch_ref, cbp_ref, chp_ref, cbn_ref, chn_ref, w_ref, o_ref, *, tm, sblocks):
    i = pl.program_id(0)
    first = (i % sblocks) == 0
    last = (i % sblocks) == sblocks - 1
    bx = cb_ref[...].astype(F32) * ch_ref[...].astype(F32)
    prev_row = cbp_ref[SUBLANES - 1:SUBLANES, :].astype(F32) * chp_ref[SUBLANES - 1:SUBLANES, :].astype(F32)
    next_row = cbn_ref[0:1, :].astype(F32) * chn_ref[0:1, :].astype(F32)
    prev_row = jnp.where(first, 0.0, prev_row)
    next_row = jnp.where(last, 0.0, next_row)
    row = lax.broadcasted_iota(jnp.int32, bx.shape, 0)
    prev = jnp.where(row == 0, prev_row, pltpu.roll(bx, 1, axis=0))
    nxt = jnp.where(row == tm - 1, next_row, pltpu.roll(bx, tm - 1, axis=0))
    w = w_ref[...]
    y = w[0:1, :] * prev + w[1:2, :] * bx + w[2:3, :] * nxt
    o_ref[...] = (cc_ref[...].astype(F32) * y).astype(o_ref.dtype)


def _short_conv(p, conv_w, layer, seq, cb0, cc0, ch0, width, *, tm=512, tc=1024):
    t = p.shape[0]
    tm, tc = _tile(seq, tm), math.gcd(cb0, cc0, ch0, width, tc)
    sblocks = seq // tm
    rpb = tm // SUBLANES
    nrb = t // SUBLANES

    def main(c0):
        return pl.BlockSpec((tm, tc), lambda i, j: (i, c0 // tc + j))

    def halo_prev(c0):
        return pl.BlockSpec((SUBLANES, tc), lambda i, j: (jnp.maximum(i * rpb - 1, 0), c0 // tc + j))

    def halo_next(c0):
        return pl.BlockSpec((SUBLANES, tc), lambda i, j: (jnp.minimum((i + 1) * rpb, nrb - 1), c0 // tc + j))

    return pl.pallas_call(
        functools.partial(_conv_kernel, tm=tm, sblocks=sblocks),
        grid=(t // tm, width // tc),
        in_specs=[main(cb0), main(cc0), main(ch0), halo_prev(cb0), halo_prev(ch0),
                  halo_next(cb0), halo_next(ch0),
                  pl.BlockSpec((None, CONV_K, tc), lambda i, j: (layer, 0, j))],
        out_specs=pl.BlockSpec((tm, tc), lambda i, j: (i, j)),
        out_shape=jax.ShapeDtypeStruct((t, width), BF16),
        compiler_params=_params(("parallel", "parallel")),
        name="short_conv",
    )(p, p, p, p, p, p, p, conv_w)


def _merge_kernel(a_ref, c_ref, wa_ref, wc_ref, ga_ref, gc_ref, o_ref, wab_ref, wcb_ref):
    @pl.when(pl.program_id(1) == 0)
    def _():
        wab_ref[...] = wa_ref[...].astype(BF16)
        wcb_ref[...] = wc_ref[...].astype(BF16)

    ya = jnp.dot(a_ref[...], wab_ref[...], preferred_element_type=F32)
    yc = jnp.dot(c_ref[...], wcb_ref[...], preferred_element_type=F32)
    o_ref[...] = (jax.nn.sigmoid(ga_ref[...].astype(F32)) * ya
                  + jax.nn.sigmoid(gc_ref[...].astype(F32)) * yc).astype(o_ref.dtype)


def _merge(attn, conv, wa, wc, p, layer, ga0, gc0, *, tm=1024, tn=512):
    t, ka = attn.shape
    kc = conv.shape[1]
    d = wa.shape[2]
    tm, tn = _tile(t, tm), math.gcd(ga0, gc0, d, tn)
    return pl.pallas_call(
        _merge_kernel,
        grid=(d // tn, t // tm),
        in_specs=[
            pl.BlockSpec((tm, ka), lambda j, i: (i, 0)),
            pl.BlockSpec((tm, kc), lambda j, i: (i, 0)),
            pl.BlockSpec((None, ka, tn), lambda j, i: (layer, 0, j)),
            pl.BlockSpec((None, kc, tn), lambda j, i: (layer, 0, j)),
            pl.BlockSpec((tm, tn), lambda j, i: (i, ga0 // tn + j)),
            pl.BlockSpec((tm, tn), lambda j, i: (i, gc0 // tn + j)),
        ],
        out_specs=pl.BlockSpec((tm, tn), lambda j, i: (i, j)),
        out_shape=jax.ShapeDtypeStruct((t, d), BF16),
        scratch_shapes=[pltpu.VMEM((ka, tn), BF16), pltpu.VMEM((kc, tn), BF16)],
        compiler_params=_params(("parallel", "arbitrary")),
        name="gated_merge",
    )(attn, conv, wa, wc, p, p)


def _ln_rows(h, g, b):
    mu = jnp.mean(h, axis=-1, keepdims=True)
    hc = h - mu
    var = jnp.mean(hc * hc, axis=-1, keepdims=True)
    return hc * lax.rsqrt(var + LN_EPS) * g + b


def _route(logits, bias, n_exp):
    epg = n_exp // N_GROUPS
    mx = jnp.max(logits, axis=0, keepdims=True)
    ex = jnp.exp(logits - mx)
    scores = ex / jnp.sum(ex, axis=0, keepdims=True)
    sel = scores + bias
    rows_sel = [sel[e:e + 1, :] for e in range(n_exp)]
    rows_sc = [scores[e:e + 1, :] for e in range(n_exp)]
    best = None
    for g in range(N_GROUPS):
        mem = rows_sel[g * epg:(g + 1) * epg]
        gs = None
        for a in range(epg):
            for c in range(a + 1, epg):
                pair = mem[a] + mem[c]
                gs = pair if gs is None else jnp.maximum(gs, pair)
        if best is None:
            best, grp = gs, jnp.zeros(gs.shape, jnp.int32)
        else:
            upd = gs > best
            best = jnp.where(upd, gs, best)
            grp = jnp.where(upd, g, grp)
    cand_sel, cand_sc = [], []
    for j in range(epg):
        cs, cc = rows_sel[j], rows_sc[j]
        for g in range(1, N_GROUPS):
            cs = jnp.where(grp == g, rows_sel[g * epg + j], cs)
            cc = jnp.where(grp == g, rows_sc[g * epg + j], cc)
        cand_sel.append(cs)
        cand_sc.append(cc)

    def first_argmax(vals, skip):
        bv = bi = bs = None
        for j in range(epg):
            v = vals[j] if skip is None else jnp.where(skip == j, -jnp.inf, vals[j])
            if bv is None:
                bv, bi, bs = v, jnp.zeros(v.shape, jnp.int32), cand_sc[0]
            else:
                upd = v > bv
                bv = jnp.where(upd, v, bv)
                bi = jnp.where(upd, j, bi)
                bs = jnp.where(upd, cand_sc[j], bs)
        return bi, bs

    i1, s1 = first_argmax(cand_sel, None)
    i2, s2 = first_argmax(cand_sel, i1)
    tot = s1 + s2
    zi = jnp.zeros((SUBLANES - TOP_K,) + i1.shape[1:], jnp.int32)
    zf = jnp.zeros((SUBLANES - TOP_K,) + i1.shape[1:], F32)
    idx = jnp.concatenate([grp * epg + i1, grp * epg + i2, zi], axis=0)
    gate = jnp.concatenate([s1 / tot, s2 / tot, zf], axis=0)
    return idx, gate


def _ln_route_kernel(h_ref, g_ref, b_ref, wh_ref, wl_ref, rb_ref, o_ref, op_ref, idx_ref, gate_ref, *, n_exp):
    y = _ln_rows(h_ref[...], g_ref[...], b_ref[...])
    half = y.shape[1] // 2
    o_ref[...] = y
    op_ref[...] = _pack_bf16_pair(y[:, :half], y[:, half:])
    yh = y.astype(BF16)
    yl = (y - yh.astype(F32)).astype(BF16)
    nt = (((1,), (1,)), ((), ()))
    logits = (lax.dot_general(wh_ref[...], yh, nt, preferred_element_type=F32)
              + lax.dot_general(wh_ref[...], yl, nt, preferred_element_type=F32)
              + lax.dot_general(wl_ref[...], yh, nt, preferred_element_type=F32))
    idx_ref[...], gate_ref[...] = _route(logits, rb_ref[...], n_exp)


def _ln_route(h, g, b, layer, w_router_t, bias_col, *, tm=256):
    t, d = h.shape
    n_exp = w_router_t.shape[0]
    tm = _tile(t, tm)
    wh = w_router_t.astype(BF16)
    wl = (w_router_t - wh.astype(F32)).astype(BF16)
    row = pl.BlockSpec((tm, d), lambda i: (i, 0))
    vec = pl.BlockSpec((None, 1, d), lambda i: (layer, 0, 0))
    wsp = pl.BlockSpec((n_exp, d), lambda i: (0, 0))
    rsp = pl.BlockSpec((SUBLANES, tm), lambda i: (0, i))
    return pl.pallas_call(
        functools.partial(_ln_route_kernel, n_exp=n_exp),
        grid=(t // tm,),
        in_specs=[row, vec, vec, wsp, wsp, pl.BlockSpec((n_exp, 1), lambda i: (0, 0))],
        out_specs=[row, pl.BlockSpec((tm, d // 2), lambda i: (i, 0)), rsp, rsp],
        out_shape=[jax.ShapeDtypeStruct((t, d), F32), jax.ShapeDtypeStruct((t, d // 2), jnp.uint32),
                   jax.ShapeDtypeStruct((SUBLANES, t), jnp.int32), jax.ShapeDtypeStruct((SUBLANES, t), F32)],
        compiler_params=_params(("parallel",)),
        name="ln_route",
    )(h, g, b, wh, wl, bias_col)


def _dispatch_kernel(slot_ref, cnt_ref, pst_ref, nu_ref, x_ref, o_hbm, z_ref, sem, *, tm, n_exp, epp, nblk):
    i = pl.program_id(0)
    n = pl.num_programs(0)

    @pl.when(i == 0)
    def _():
        z_ref[...] = jnp.zeros_like(z_ref)

    def issue(g, c):
        for u in range(ISSUE_UNROLL):
            r = g * ISSUE_UNROLL + u
            for k in range(TOP_K):
                pltpu.make_async_copy(x_ref.at[pl.ds(r, 1), :],
                                      o_hbm.at[pl.ds(slot_ref[(i * tm + r) * TOP_K + k], 1), :],
                                      sem.at[0]).start(priority=k % 2)
        return c

    lax.fori_loop(0, tm // ISSUE_UNROLL, issue, 0)

    def wait_rows(rows):
        pltpu.make_async_copy(z_ref.at[pl.ds(0, rows), :], o_hbm.at[pl.ds(0, rows), :], sem.at[0]).wait()

    for j in range(epp):
        e = i * epp + j

        @pl.when(e < n_exp)
        def _():
            cnt = cnt_ref[e]
            first = pst_ref[e] + cnt
            npad = (tm - cnt % tm) % tm

            def zero(r, c):
                pltpu.make_async_copy(z_ref.at[pl.ds(0, 1), :], o_hbm.at[pl.ds(first + r, 1), :], sem.at[0]).start()
                return c

            def wait_one(r, c):
                wait_rows(1)
                return c

            lax.fori_loop(0, npad, zero, 0)
            lax.fori_loop(0, npad, wait_one, 0)

    for k in range(TOP_K):
        wait_rows(tm)

    @pl.when(i == n - 1)
    def _():
        def zero_block(b, c):
            cp = pltpu.make_async_copy(z_ref, o_hbm.at[pl.ds(pl.multiple_of(b * tm, tm), tm), :], sem.at[0])
            cp.start()
            cp.wait()
            return c

        lax.fori_loop(nu_ref[0], nblk, zero_block, 0)


def _dispatch_rows(slot, counts, pstarts, n_used, xp, n_rows, tm):
    t, w = xp.shape
    n_exp = counts.shape[0]
    assert t % tm == 0 and n_rows % tm == 0
    n_steps = t // tm
    epp = -(-n_exp // n_steps)
    grid_spec = pltpu.PrefetchScalarGridSpec(
        num_scalar_prefetch=4,
        grid=(n_steps,),
        in_specs=[pl.BlockSpec((tm, w), lambda i, sl, cn, ps, nu: (i, 0))],
        out_specs=pl.BlockSpec(memory_space=pl.ANY),
        scratch_shapes=[pltpu.VMEM((tm, w), xp.dtype), pltpu.SemaphoreType.DMA((1,))],
    )
    return pl.pallas_call(
        functools.partial(_dispatch_kernel, tm=tm, n_exp=n_exp, epp=epp, nblk=n_rows // tm),
        grid_spec=grid_spec,
        out_shape=jax.ShapeDtypeStruct((n_rows, w), xp.dtype),
        compiler_params=_params(("arbitrary",)),
        name="dispatch_rows",
    )(slot, counts, pstarts, n_used, xp)


def _new_expert(be_ref, i):
    return jnp.logical_or(i == 0, be_ref[i] != be_ref[jnp.maximum(i - 1, 0)])


def _stage_weights(be_ref, nx_ref, lr_ref, copies, stage_refs, cache_refs):
    t = pl.program_id(0)
    i = pl.program_id(1)
    n_t = pl.num_programs(0)

    def start(e, tile):
        for cp in copies(e, tile):
            cp.start()

    @pl.when(jnp.logical_and(t == 0, i == 0))
    def _():
        start(be_ref[0], 0)

    @pl.when(_new_expert(be_ref, i))
    def _():
        for cp in copies(be_ref[i], t):
            cp.wait()
        for st, ca in zip(stage_refs, cache_refs):
            ca[...] = st[...].astype(BF16)
        last_run = lr_ref[i] == 1

        @pl.when(jnp.logical_not(last_run))
        def _():
            start(nx_ref[i], t)

        @pl.when(jnp.logical_and(last_run, t + 1 < n_t))
        def _():
            start(be_ref[0], t + 1)


def _half_block_compute(valid, compute, o_ref):
    tm = o_ref.shape[0]
    hb = tm // 2

    @pl.when(valid > hb)
    def _():
        o_ref[...] = compute(slice(None))

    @pl.when(jnp.logical_and(valid > 0, valid <= hb))
    def _():
        o_ref[:hb, :] = compute(slice(0, hb))
        o_ref[hb:, :] = jnp.zeros((tm - hb, o_ref.shape[1]), o_ref.dtype)

    @pl.when(valid == 0)
    def _():
        o_ref[...] = jnp.zeros_like(o_ref)


def _expert_up_kernel(be_ref, nu_ref, nx_ref, lr_ref, vr_ref, x_ref, wg_hbm, wu_hbm, o_ref,
                      wgf_ref, wuf_ref, wgb_ref, wub_ref, sem, *, layer, tf):
    i = pl.program_id(1)

    def copies(e, tile):
        cols = pl.ds(pl.multiple_of(tile * tf, tf), tf)
        return (pltpu.make_async_copy(wg_hbm.at[layer, e, :, cols], wgf_ref, sem.at[0]),
                pltpu.make_async_copy(wu_hbm.at[layer, e, :, cols], wuf_ref, sem.at[1]))

    _stage_weights(be_ref, nx_ref, lr_ref, copies, (wgf_ref, wuf_ref), (wgb_ref, wub_ref))

    def hidden(rows):
        lo, hi = _unpack_bf16_pair(x_ref[rows, :])
        lo, hi = lo.astype(BF16), hi.astype(BF16)
        half = lo.shape[1]
        hg = (jnp.dot(lo, wgb_ref[:half, :], preferred_element_type=F32)
              + jnp.dot(hi, wgb_ref[half:, :], preferred_element_type=F32))
        hu = (jnp.dot(lo, wub_ref[:half, :], preferred_element_type=F32)
              + jnp.dot(hi, wub_ref[half:, :], preferred_element_type=F32))
        return (hg * jax.nn.sigmoid(hg) * hu).astype(o_ref.dtype)

    _half_block_compute(vr_ref[i], hidden, o_ref)


def _expert_down_kernel(be_ref, nu_ref, nx_ref, lr_ref, vr_ref, h_ref, wd_hbm, o_ref,
                        wlof_ref, whif_ref, wlob_ref, whib_ref, sem, *, layer, tn, hi0):
    i = pl.program_id(1)

    def copies(e, tile):
        lo = pl.ds(pl.multiple_of(tile * tn, tn), tn)
        hi = pl.ds(pl.multiple_of((hi0 + tile) * tn, tn), tn)
        return (pltpu.make_async_copy(wd_hbm.at[layer, e, :, lo], wlof_ref, sem.at[0]),
                pltpu.make_async_copy(wd_hbm.at[layer, e, :, hi], whif_ref, sem.at[1]))

    _stage_weights(be_ref, nx_ref, lr_ref, copies, (wlof_ref, whif_ref), (wlob_ref, whib_ref))

    def rows_out(rows):
        h = h_ref[rows, :]
        return _pack_bf16_pair(jnp.dot(h, wlob_ref[...], preferred_element_type=F32),
                               jnp.dot(h, whib_ref[...], preferred_element_type=F32))

    _half_block_compute(vr_ref[i], rows_out, o_ref)


def _experts(block_e, n_used, next_e, last_run, valid_rows, xs, wg, wu, wd, layer, tm, *, tf=512, tn=1024):
    r = xs.shape[0]
    d = 2 * xs.shape[1]
    ff = wg.shape[3]
    tf = _tile(ff, tf)
    nblk = r // tm
    hbm = pl.BlockSpec(memory_space=pl.ANY)
    hid = pl.pallas_call(
        functools.partial(_expert_up_kernel, layer=layer, tf=tf),
        grid_spec=pltpu.PrefetchScalarGridSpec(
            num_scalar_prefetch=5,
            grid=(ff // tf, nblk),
            in_specs=[
                pl.BlockSpec((tm, d // 2), lambda f, i, be, nu, nx, lr, vr: (jnp.minimum(i, nu[0] - 1), 0)),
                hbm, hbm,
            ],
            out_specs=pl.BlockSpec((tm, tf), lambda f, i, be, nu, nx, lr, vr: (i, f)),
            scratch_shapes=[pltpu.VMEM((d, tf), F32), pltpu.VMEM((d, tf), F32),
                            pltpu.VMEM((d, tf), BF16), pltpu.VMEM((d, tf), BF16),
                            pltpu.SemaphoreType.DMA((2,))],
        ),
        out_shape=jax.ShapeDtypeStruct((r, ff), BF16),
        compiler_params=_params(("arbitrary", "arbitrary")),
        name="expert_up",
    )(block_e, n_used, next_e, last_run, valid_rows, xs, wg, wu)
    half = d // 2
    tn = _tile(half, tn)
    hi0 = half // tn
    return pl.pallas_call(
        functools.partial(_expert_down_kernel, layer=layer, tn=tn, hi0=hi0),
        grid_spec=pltpu.PrefetchScalarGridSpec(
            num_scalar_prefetch=5,
            grid=(half // tn, nblk),
            in_specs=[pl.BlockSpec((tm, ff), lambda n, i, be, nu, nx, lr, vr: (i, 0)), hbm],
            out_specs=pl.BlockSpec((tm, tn), lambda n, i, be, nu, nx, lr, vr: (i, n)),
            scratch_shapes=[pltpu.VMEM((ff, tn), F32), pltpu.VMEM((ff, tn), F32),
                            pltpu.VMEM((ff, tn), BF16), pltpu.VMEM((ff, tn), BF16),
                            pltpu.SemaphoreType.DMA((2,))],
        ),
        out_shape=jax.ShapeDtypeStruct((r, half), jnp.uint32),
        compiler_params=_params(("arbitrary", "arbitrary")),
        name="expert_down",
    )(block_e, n_used, next_e, last_run, valid_rows, hid, wd)


def _combine_kernel(slot_ref, x_ref, y_hbm, gt_ref, g_ref, b_ref, o_ref, ob_ref, ybuf0, ybuf1, sem, *, alpha, tm):
    i = pl.program_id(0)
    n = pl.num_programs(0)
    ybufs = (ybuf0, ybuf1)

    def row_copy(row, s, k, r):
        return pltpu.make_async_copy(y_hbm.at[pl.ds(row, 1), :], ybufs[s].at[k, pl.ds(r, 1), :], sem.at[s])

    def wait_gather(s):
        for k in range(TOP_K):
            pltpu.make_async_copy(y_hbm.at[pl.ds(0, tm), :], ybufs[s].at[k], sem.at[s]).wait()

    @pl.when(i == 0)
    def _():
        def issue(r, c):
            for k in range(TOP_K):
                row_copy(slot_ref[r * TOP_K + k], 0, k, r).start(priority=k % 2)
            return c

        lax.fori_loop(0, tm, issue, 0, unroll=4)

    def step(s):
        base = jnp.minimum(i + 1, n - 1) * (tm * TOP_K)
        for r in range(tm):
            for k in range(TOP_K):
                row_copy(slot_ref[base + r * TOP_K + k], 1 - s, k, r).start(priority=k % 2)
        wait_gather(s)
        gt = gt_ref[...]
        a_lo, a_hi = _unpack_bf16_pair(ybufs[s][0])
        b_lo, b_hi = _unpack_bf16_pair(ybufs[s][1])
        g0, g1 = gt[:, 0:1], gt[:, 1:2]
        ffn = jnp.concatenate([g0 * a_lo + g1 * b_lo, g0 * a_hi + g1 * b_hi], axis=1)
        y = _ln_rows(alpha * x_ref[...] + ffn, g_ref[...], b_ref[...])
        o_ref[...] = y
        ob_ref[...] = y.astype(BF16)

        @pl.when(i == n - 1)
        def _():
            wait_gather(1 - s)

    for s in range(2):
        pl.when(i % 2 == s)(functools.partial(step, s))


def _combine_ln(slot, x, y, gates_t, g, b, layer, alpha, *, tm=256):
    t, d = x.shape
    tm = _tile(t, tm)
    row = pl.BlockSpec((tm, d), lambda i, sl: (i, 0))
    vec = pl.BlockSpec((None, 1, d), lambda i, sl: (layer, 0, 0))
    grid_spec = pltpu.PrefetchScalarGridSpec(
        num_scalar_prefetch=1,
        grid=(t // tm,),
        in_specs=[row, pl.BlockSpec(memory_space=pl.ANY),
                  pl.BlockSpec((tm, SUBLANES), lambda i, sl: (i, 0)), vec, vec],
        out_specs=[row, row],
        scratch_shapes=[pltpu.VMEM((TOP_K, tm, d // 2), jnp.uint32), pltpu.VMEM((TOP_K, tm, d // 2), jnp.uint32),
                        pltpu.SemaphoreType.DMA((2,))],
    )
    return pl.pallas_call(
        functools.partial(_combine_kernel, alpha=alpha, tm=tm),
        grid_spec=grid_spec,
        out_shape=[jax.ShapeDtypeStruct((t, d), F32), jax.ShapeDtypeStruct((t, d), BF16)],
        compiler_params=_params(("arbitrary",)),
        name="combine_ln",
    )(slot, x, y, gates_t, g, b)


def _rope_tables(seq):
    axis_dim = HEAD_DIM // 2
    rows = seq // GRID_W
    inv = ROPE_THETA ** (-jnp.arange(0, axis_dim, 2, dtype=F32) / axis_dim)
    ang_r = jnp.repeat(jnp.arange(rows, dtype=F32), GRID_W)[:, None] * inv[None, :]
    ang_c = jnp.tile(jnp.arange(GRID_W, dtype=F32), rows)[:, None] * inv[None, :]
    cr, sr, cc, sc = jnp.cos(ang_r), jnp.sin(ang_r), jnp.cos(ang_c), jnp.sin(ang_c)
    cos_t = jnp.concatenate([cr, cr, cc, cc], axis=1)
    sin_t = jnp.concatenate([-sr, sr, -sc, sc], axis=1)
    return cos_t, sin_t


def _dispatch(idx, n_exp, tm):
    t = idx.shape[1]
    n_asg = t * TOP_K
    e_flat = idx.T.reshape(-1)
    onehot = (e_flat[:, None] == jnp.arange(n_exp, dtype=jnp.int32)[None, :]).astype(jnp.int32)
    csum = jnp.cumsum(onehot, axis=0)
    rank = jnp.sum(csum * onehot, axis=1) - 1
    counts = csum[-1]
    pcounts = (counts + tm - 1) // tm * tm
    pends = jnp.cumsum(pcounts)
    pstarts = pends - pcounts
    slot = (pstarts[e_flat] + rank).astype(jnp.int32)
    nblk = n_asg // tm + n_exp
    block_start = jnp.arange(nblk, dtype=jnp.int32) * tm
    block_e = jnp.minimum(jnp.sum(pends[None, :] <= block_start[:, None], axis=-1), n_exp - 1).astype(jnp.int32)
    n_used = (pends[-1] // tm).astype(jnp.int32).reshape(1)
    change_at = jnp.where(block_e[1:] != block_e[:-1], jnp.arange(1, nblk, dtype=jnp.int32), nblk)
    next_start = lax.cummin(jnp.concatenate([change_at, jnp.full((1,), nblk, jnp.int32)]), reverse=True)
    last_run = (next_start >= nblk).astype(jnp.int32)
    next_e = block_e[jnp.minimum(next_start, nblk - 1)]
    valid_rows = jnp.clip(pstarts[block_e] + counts[block_e] - block_start, 0, tm).astype(jnp.int32)
    return (slot, counts.astype(jnp.int32), pstarts.astype(jnp.int32), block_e, n_used, next_e, last_run,
            valid_rows, nblk * tm)


def kernel(x, w_in, q_norm_g, k_norm_g, conv_w, w_attn_proj, w_conv_proj, w_out, ln1_g, ln1_b,
           w_router, router_bias, w_gate, w_up, w_down, ln2_g, ln2_b):
    batch, seq, d = x.shape
    depth = w_in.shape[0]
    t = batch * seq
    attn_w = d // 2
    n_q = attn_w // HEAD_DIM
    n_kv = n_q // GQA_RATIO
    kv_w = n_kv * HEAD_DIM
    conv_wd = d // 2
    n_exp = w_router.shape[1]
    alpha = (2 * depth) ** 0.25
    q_end = attn_w
    k_end = q_end + kv_w
    v_end = k_end + kv_w
    cb_end = v_end + conv_wd
    cc_end = cb_end + conv_wd
    ch_end = cc_end + conv_wd
    ga_end = ch_end + d
    expert_tm = min(512, t)

    cos_t, sin_t = _rope_tables(seq)
    scale = HEAD_DIM ** -0.5 * math.log2(math.e)
    w_router_t = w_router.T
    bias_col = router_bias.reshape(n_exp, 1).astype(F32)
    ln1_g3, ln1_b3 = ln1_g.reshape(depth, 1, d), ln1_b.reshape(depth, 1, d)
    ln2_g3, ln2_b3 = ln2_g.reshape(depth, 1, d), ln2_b.reshape(depth, 1, d)

    xf = x.reshape(t, d)
    xb = xf.astype(BF16)
    for l in range(depth):
        p = _matmul(xb, w_in, l, BF16, name="in_proj")
        gains = jnp.concatenate([jnp.tile(q_norm_g[l] * scale, n_q), jnp.tile(k_norm_g[l], n_kv)]).reshape(1, k_end)
        qk = _qk_prep(p, gains.astype(F32), cos_t, sin_t, seq, k_end)
        attn = _attention(qk, p, batch, seq, n_kv, q_end // HEAD_DIM, k_end // HEAD_DIM)
        conv = _short_conv(p, conv_w, l, seq, v_end, cb_end, cc_end, conv_wd)
        merged = _merge(attn, conv, w_attn_proj, w_conv_proj, p, l, ch_end, ga_end)
        h1 = _matmul(merged, w_out, l, F32, res=xf, alpha=alpha, tm=1024, tn=512, name="out_proj")
        x1, x1p, idx, gate = _ln_route(h1, ln1_g3, ln1_b3, l, w_router_t, bias_col)
        slot, counts, pstarts, block_e, n_used, next_e, last_run, valid_rows, n_rows = _dispatch(
            idx[:TOP_K], n_exp, expert_tm)
        xs = _dispatch_rows(slot, counts, pstarts, n_used, x1p, n_rows, expert_tm)
        y = _experts(block_e, n_used, next_e, last_run, valid_rows, xs, w_gate, w_up, w_down, l, expert_tm)
        xf, xb = _combine_ln(slot, x1, y, gate.T, ln2_g3, ln2_b3, l, alpha)
    return xf.reshape(batch, seq, d)
```
